```python
import math
import jax, jax.numpy as jnp
from jax import lax
import numpy as np

D_MODEL = 1024
BATCH = 2
SEQ = 16384
DEPTH = 2

D_MIX = D_MODEL
HEAD_DIM = 64
D_ATT = 3 * D_MIX // 4
N_HEADS = D_ATT // HEAD_DIM
N_KV = 3
GQA = N_HEADS // N_KV
D_SSM = D_MIX - D_ATT
SSM_CH = 16
N_SSM_GROUPS = D_SSM // SSM_CH
SSM_STATE = 64
L_CMP = 32
STRIDE_CMP = 16
CMP_HID = 256
L_SEL = 64
TOP_K = 16
WINDOW = 512
Q_BLK = 128
BIG = 1e9
TINY = 1e-30
D_FF = -(-8 * D_MODEL // (3 * 256)) * 256

P_KV = N_KV * HEAD_DIM
P_GATE = 3 * N_HEADS
P_SIZES = [D_ATT] + [P_KV] * 6 + [P_GATE, D_SSM]
P_TOTAL = D_ATT + 6 * P_KV + P_GATE + D_SSM

kernel_name = 'hymba_nsa_s5_hybrid_trunk'


def rmsnorm(x, g, eps=1e-6):
    xf = x.astype(jnp.float32)
    y = xf * lax.rsqrt(jnp.mean(xf * xf, axis=-1, keepdims=True) + eps)
    return (y * g.astype(jnp.float32)).astype(x.dtype)


def masked_softmax(s, mask):
    s = jnp.where(mask, s.astype(jnp.float32), -jnp.inf)
    m = jnp.max(s, axis=-1, keepdims=True)
    m = jnp.where(jnp.isfinite(m), m, 0.0)
    p = jnp.exp(s - m)
    return p / jnp.maximum(jnp.sum(p, axis=-1, keepdims=True), TINY)


def split_q_heads(t):
    b, s, _ = t.shape
    return t.reshape(b, s, N_KV, GQA, HEAD_DIM).transpose(0, 2, 3, 1, 4)


def split_kv_heads(t):
    b, s, _ = t.shape
    return t.reshape(b, s, N_KV, HEAD_DIM).transpose(0, 2, 1, 3)


def compress(k, pe, w1, b1, w2, b2):
    b, h, s, dh = k.shape
    n_chunk = s // STRIDE_CMP
    r = L_CMP // STRIDE_CMP
    n_cmp = n_chunk - r + 1
    ch = k.reshape(b, h, n_chunk, STRIDE_CMP, dh)
    blocks = jnp.concatenate([ch[:, :, j:j + n_cmp] for j in range(r)], axis=3)
    blocks = (blocks + pe).reshape(b, h, n_cmp, L_CMP * dh)
    hid = jax.nn.gelu(blocks @ w1 + b1)
    return hid @ w2 + b2


def nsa_attention(q, k_c, v_c, k_s, v_s, k_w, v_w, gates):
    b, h, g, s, dh = q.shape
    n_cmp = k_c.shape[2]
    n_sel = s // L_SEL
    k_top = min(TOP_K, n_sel)
    cmp_start = jnp.arange(n_cmp) * STRIDE_CMP
    cmp_end = cmp_start + L_CMP - 1
    sel_start = jnp.arange(n_sel) * L_SEL
    overlap = ((cmp_start[:, None] < sel_start[None, :] + L_SEL)
               & (cmp_start[:, None] + L_CMP > sel_start[None, :])).astype(jnp.float32)
    ks_blk = k_s.reshape(b, h, n_sel, L_SEL, dh)
    vs_blk = v_s.reshape(b, h, n_sel, L_SEL, dh)
    kw_pad = jnp.pad(k_w, ((0, 0), (0, 0), (WINDOW, 0), (0, 0)))
    vw_pad = jnp.pad(v_w, ((0, 0), (0, 0), (WINDOW, 0), (0, 0)))
    b_ix = jnp.arange(b)[:, None, None, None]
    h_ix = jnp.arange(h)[None, :, None, None]
    blk_ids = jnp.arange(n_sel)

    def block(i):
        q0 = i * Q_BLK
        qb = lax.dynamic_slice_in_dim(q, q0, Q_BLK, axis=3)
        gb = lax.dynamic_slice_in_dim(gates, q0, Q_BLK, axis=3)
        t = q0 + jnp.arange(Q_BLK)
        s_c = jnp.einsum('bhgqd,bhnd->bhgqn', qb, k_c)
        p_c = masked_softmax(s_c, cmp_end[None, :] <= t[:, None])
        o_c = jnp.einsum('bhgqn,bhnd->bhgqd', p_c.astype(v_c.dtype), v_c)
        imp = jnp.einsum('bhgqn,ns->bhqs', p_c, overlap)
        cur = t // L_SEL
        forced = ((blk_ids[None, :] == 0) | (blk_ids[None, :] == cur[:, None])
                  | (blk_ids[None, :] == cur[:, None] - 1))
        valid = blk_ids[None, :] <= cur[:, None]
        imp = jnp.where(forced, BIG, jnp.where(valid, imp, -BIG))
        _, idx = lax.top_k(imp, k_top)
        ksel = ks_blk[b_ix, h_ix, idx].reshape(b, h, Q_BLK, k_top * L_SEL, dh)
        vsel = vs_blk[b_ix, h_ix, idx].reshape(b, h, Q_BLK, k_top * L_SEL, dh)
        pos = (idx[..., None] * L_SEL + jnp.arange(L_SEL)).reshape(b, h, Q_BLK, k_top * L_SEL)
        s_s = jnp.einsum('bhgqd,bhqkd->bhgqk', qb, ksel)
        p_s = masked_softmax(s_s, (pos <= t[:, None])[:, :, None])
        o_s = jnp.einsum('bhgqk,bhqkd->bhgqd', p_s.astype(vsel.dtype), vsel)
        kwb = lax.dynamic_slice_in_dim(kw_pad, q0, Q_BLK + WINDOW, axis=2)
        vwb = lax.dynamic_slice_in_dim(vw_pad, q0, Q_BLK + WINDOW, axis=2)
        s_pos = q0 - WINDOW + jnp.arange(Q_BLK + WINDOW)
        wmask = ((s_pos[None, :] <= t[:, None]) & (s_pos[None, :] > t[:, None] - WINDOW)
                 & (s_pos[None, :] >= 0))
        s_w = jnp.einsum('bhgqd,bhkd->bhgqk', qb, kwb)
        p_w = masked_softmax(s_w, wmask)
        o_w = jnp.einsum('bhgqk,bhkd->bhgqd', p_w.astype(vwb.dtype), vwb)
        return gb[..., 0:1] * o_c + gb[..., 1:2] * o_s + gb[..., 2:3] * o_w

    out = lax.map(block, jnp.arange(s // Q_BLK))
    return out.transpose(1, 0, 4, 2, 3, 5).reshape(b, s, h * g * dh)


def _ssm_combine(e1, e2):
    ar1, ai1, br1, bi1 = e1
    ar2, ai2, br2, bi2 = e2
    return (ar2 * ar1 - ai2 * ai1,
            ar2 * ai1 + ai2 * ar1,
            ar2 * br1 - ai2 * bi1 + br2,
            ar2 * bi1 + ai2 * br1 + bi2)


def s5_mixer(u, log_dt, a_re, a_im, b_re, b_im, c_re, c_im, d_skip, w_glu, b_glu):
    f32 = jnp.float32
    bsz, s, _ = u.shape
    uf = u.astype(f32).reshape(bsz, s, N_SSM_GROUPS, SSM_CH)
    a_re = a_re.astype(f32)
    a_im = a_im.astype(f32)
    dt = jnp.exp(log_dt.astype(f32))[:, None]
    mag = jnp.exp(dt * a_re)
    abar_re = mag * jnp.cos(dt * a_im)
    abar_im = mag * jnp.sin(dt * a_im)
    den = a_re * a_re + a_im * a_im
    nr = abar_re - 1.0
    f_re = (nr * a_re + abar_im * a_im) / den
    f_im = (abar_im * a_re - nr * a_im) / den
    b_re = b_re.astype(f32)
    b_im = b_im.astype(f32)
    bbar_re = f_re[..., None] * b_re - f_im[..., None] * b_im
    bbar_im = f_re[..., None] * b_im + f_im[..., None] * b_re
    bu_re = jnp.einsum('bsgc,gnc->bsgn', uf, bbar_re)
    bu_im = jnp.einsum('bsgc,gnc->bsgn', uf, bbar_im)
    elems = (jnp.broadcast_to(abar_re, bu_re.shape), jnp.broadcast_to(abar_im, bu_re.shape), bu_re, bu_im)
    _, _, h_re, h_im = lax.associative_scan(_ssm_combine, elems, axis=1)
    y = (jnp.einsum('gcn,bsgn->bsgc', c_re.astype(f32), h_re)
         - jnp.einsum('gcn,bsgn->bsgc', c_im.astype(f32), h_im))
    y = y.reshape(bsz, s, D_SSM) + d_skip.astype(f32) * uf.reshape(bsz, s, D_SSM)
    y = jax.nn.gelu(y)
    z = y @ w_glu.astype(f32) + b_glu.astype(f32)
    val, gate = jnp.split(z, 2, axis=-1)
    return (val * jax.nn.sigmoid(gate)).astype(u.dtype)


def setup_inputs(seed: int = 0) -> dict:
    key = jax.random.key(seed)
    ks = jax.random.split(key, 32)
    f32 = jnp.float32
    L, D, G, N, C = DEPTH, D_MODEL, N_SSM_GROUPS, SSM_STATE, SSM_CH

    def nrm(k, shape, scale):
        return scale * jax.random.normal(k, shape, f32)

    return {
        'x': nrm(ks[0], (BATCH, SEQ, D), 1.0),
        'attn_norm_g': 1.0 + nrm(ks[1], (L, D), 0.02),
        'w_in': nrm(ks[2], (L, D, P_TOTAL), D ** -0.5),
        'cmp_pe': nrm(ks[3], (L, 2, L_CMP, HEAD_DIM), 0.02),
        'cmp_w1': nrm(ks[4], (L, 2, L_CMP * HEAD_DIM, CMP_HID), (L_CMP * HEAD_DIM) ** -0.5),
        'cmp_b1': nrm(ks[5], (L, 2, CMP_HID), 0.01),
        'cmp_w2': nrm(ks[6], (L, 2, CMP_HID, HEAD_DIM), CMP_HID ** -0.5),
        'cmp_b2': nrm(ks[7], (L, 2, HEAD_DIM), 0.01),
        'log_dt': jax.random.uniform(ks[8], (L, G), f32, math.log(1e-3), math.log(1e-1)),
        'a_re': -0.5 + nrm(ks[9], (L, G, N), 0.01),
        'a_im': math.pi * jnp.arange(N, dtype=f32) + nrm(ks[10], (L, G, N), 0.01),
        'b_re': nrm(ks[11], (L, G, N, C), (2 * C) ** -0.5),
        'b_im': nrm(ks[12], (L, G, N, C), (2 * C) ** -0.5),
        'c_re': nrm(ks[13], (L, G, C, N), N ** -0.5),
        'c_im': nrm(ks[14], (L, G, C, N), N ** -0.5),
        'd_skip': nrm(ks[15], (L, D_SSM), 1.0),
        'w_glu': nrm(ks[16], (L, D_SSM, 2 * D_SSM), D_SSM ** -0.5),
        'b_glu': nrm(ks[17], (L, 2 * D_SSM), 0.01),
        'mix_norm_att': 1.0 + nrm(ks[18], (L, D_ATT), 0.02),
        'mix_norm_ssm': 1.0 + nrm(ks[19], (L, D_SSM), 0.02),
        'w_out': nrm(ks[20], (L, D_MIX, D), D_MIX ** -0.5),
        'ffn_norm_g': 1.0 + nrm(ks[21], (L, D), 0.02),
        'w_gate': nrm(ks[22], (L, D, D_FF), D ** -0.5),
        'w_up': nrm(ks[23], (L, D, D_FF), D ** -0.5),
        'w_down': nrm(ks[24], (L, D_FF, D), D_FF ** -0.5),
        'final_norm_g': 1.0 + nrm(ks[25], (D,), 0.02),
    }


def reference(x, attn_norm_g, w_in, cmp_pe, cmp_w1, cmp_b1, cmp_w2, cmp_b2, log_dt, a_re, a_im,
              b_re, b_im, c_re, c_im, d_skip, w_glu, b_glu, mix_norm_att, mix_norm_ssm, w_out,
              ffn_norm_g, w_gate, w_up, w_down, final_norm_g):
    bsz, s, _ = x.shape
    split_at = np.cumsum(P_SIZES)[:-1].tolist()
    for l in range(DEPTH):
        h = rmsnorm(x, attn_norm_g[l])
        proj = h @ w_in[l]
        q, kc, vc, ksl, vsl, kwn, vwn, g, u = jnp.split(proj, split_at, axis=-1)
        q = split_q_heads(q) * (HEAD_DIM ** -0.5)
        k_c = compress(split_kv_heads(kc), cmp_pe[l, 0], cmp_w1[l, 0], cmp_b1[l, 0], cmp_w2[l, 0], cmp_b2[l, 0])
        v_c = compress(split_kv_heads(vc), cmp_pe[l, 1], cmp_w1[l, 1], cmp_b1[l, 1], cmp_w2[l, 1], cmp_b2[l, 1])
        gates = jax.nn.sigmoid(g.astype(jnp.float32)).astype(x.dtype)
        gates = gates.reshape(bsz, s, N_KV, GQA, 3).transpose(0, 2, 3, 1, 4)
        o_att = nsa_attention(q, k_c, v_c, split_kv_heads(ksl), split_kv_heads(vsl),
                              split_kv_heads(kwn), split_kv_heads(vwn), gates)
        o_ssm = s5_mixer(u, log_dt[l], a_re[l], a_im[l], b_re[l], b_im[l], c_re[l], c_im[l],
                         d_skip[l], w_glu[l], b_glu[l])
        mixed = jnp.concatenate([rmsnorm(o_att, mix_norm_att[l]), rmsnorm(o_ssm, mix_norm_ssm[l])], axis=-1)
        x = x + mixed @ w_out[l]
        h = rmsnorm(x, ffn_norm_g[l])
        x = x + (jax.nn.silu(h @ w_gate[l]) * (h @ w_up[l])) @ w_down[l]
    return rmsnorm(x, final_norm_g)
```

```python
import functools
import math

import jax
import jax.numpy as jnp
from jax import lax
from jax.experimental import pallas as pl
from jax.experimental.pallas import tpu as pltpu

HEAD_DIM = 64
N_KV = 3
GQA = 4
N_HEADS = N_KV * GQA
D_ATT = N_HEADS * HEAD_DIM
SSM_CH = 16
SSM_STATE = 64
L_CMP = 32
STRIDE_CMP = 16
CMP_HID = 256
L_SEL = 64
TOP_K = 16
WINDOW = 512
Q_BLK = 128
BIG = 1e9
TINY = 1e-30
EPS = 1e-6

SSM_CHUNK = 64
SEL_TILE = 256
VMEM_LIMIT = 56 * 1024 * 1024

F32 = jnp.float32
BF16 = jnp.bfloat16


def _dot(a, b):
    return jnp.dot(a, b, preferred_element_type=F32)


def _dot_nt(a, b):
    return lax.dot_general(a, b, (((1,), (1,)), ((), ())), preferred_element_type=F32)


def _dot_split(a, b):
    a_hi = a.astype(BF16)
    a_lo = (a - a_hi.astype(F32)).astype(BF16)
    b_hi = b.astype(BF16)
    b_lo = (b - b_hi.astype(F32)).astype(BF16)
    return _dot(a_hi, b_hi) + _dot(a_hi, b_lo) + _dot(a_lo, b_hi)


def _params(*sem):
    return pltpu.CompilerParams(dimension_semantics=sem, vmem_limit_bytes=VMEM_LIMIT)


def _row_tile(t):
    for tm in (512, 256, 128, 64, 32, 16, 8):
        if t % tm == 0:
            return tm
    raise ValueError(f"token count {t} must be a multiple of 8")


def _rms(x, g):
    ms = jnp.mean(x * x, axis=-1, keepdims=True)
    return x * lax.rsqrt(ms + EPS) * g


def _in_proj_kernel(x_ref, g_ref, w_ref, o_ref):
    h = _rms(x_ref[...], g_ref[...]).astype(BF16)
    o_ref[...] = _dot(h, w_ref[...])


def in_proj(x2, g, w):
    t, d = x2.shape
    p = w.shape[1]
    tm = _row_tile(t)
    return pl.pallas_call(
        _in_proj_kernel,
        grid=(t // tm,),
        in_specs=[pl.BlockSpec((tm, d), lambda i: (i, 0)),
                  pl.BlockSpec((1, d), lambda i: (0, 0)),
                  pl.BlockSpec((d, p), lambda i: (0, 0))],
        out_specs=pl.BlockSpec((tm, p), lambda i: (i, 0)),
        out_shape=jax.ShapeDtypeStruct((t, p), F32),
        compiler_params=_params("parallel"),
        name="in_proj",
    )(x2, g.reshape(1, d), w)


def _compress_kernel(ch_ref, pe_ref, w1a_ref, w1b_ref, b1_ref, w2_ref, b2_ref, o_ref, *, transposed):
    c = ch_ref[0]
    n = c.shape[0]
    a = _dot((c + pe_ref[0:1, :]).astype(BF16), w1a_ref[...])
    b = _dot((c + pe_ref[1:2, :]).astype(BF16), w1b_ref[...])
    hid = a + pltpu.roll(b, n - 1, axis=0) + b1_ref[...]
    hid = jax.nn.gelu(hid).astype(BF16)
    if transposed:
        o_ref[0] = _dot_nt(w2_ref[...], hid) + b2_ref[...]
    else:
        o_ref[0] = _dot(hid, w2_ref[...]) + b2_ref[...]


def compress(ch, pe, w1, b1, w2, b2, transposed):
    bh, n, width = ch.shape
    half = width
    w1a = w1[:half].astype(BF16)
    w1b = w1[half:].astype(BF16)
    pe2 = pe.reshape(2, half)
    if transposed:
        w2k = w2.T.astype(BF16)
        b2k = b2.reshape(HEAD_DIM, 1)
        out_block, out_shape = (1, HEAD_DIM, n), (bh, HEAD_DIM, n)
    else:
        w2k = w2.astype(BF16)
        b2k = b2.reshape(1, HEAD_DIM)
        out_block, out_shape = (1, n, HEAD_DIM), (bh, n, HEAD_DIM)
    full = lambda a: pl.BlockSpec(a.shape, lambda i: (0,) * a.ndim)
    b1k = b1.reshape(1, CMP_HID)
    return pl.pallas_call(
        functools.partial(_compress_kernel, transposed=transposed),
        grid=(bh,),
        in_specs=[pl.BlockSpec((1, n, width), lambda i: (i, 0, 0)),
                  full(pe2), full(w1a), full(w1b), full(b1k), full(w2k), full(b2k)],
        out_specs=pl.BlockSpec(out_block, lambda i: (i, 0, 0)),
        out_shape=jax.ShapeDtypeStruct(out_shape, F32),
        compiler_params=_params("parallel"),
        name="compress_t" if transposed else "compress_n",
    )(ch, pe2, w1a, w1b, b1k, w2k, b2k)


def _softmax_cols(s, mask):
    s = jnp.where(mask, s, -jnp.inf)
    m = jnp.max(s, axis=0, keepdims=True)
    m = jnp.where(m == -jnp.inf, 0.0, m)
    p = jnp.exp(s - m)
    l = jnp.sum(p, axis=0, keepdims=True)
    return p, jnp.maximum(l, TINY)


def _attn_kernel(qt_ref, kc_ref, vct_ref, ks_ref, vst_ref, kw_ref, vwt_ref, gt_ref, ovt_ref,
                 o_ref, sel_ref, *, seq):
    qi = pl.program_id(1)
    n_cmp = seq // STRIDE_CMP
    n_sel = seq // L_SEL
    lanes = GQA * Q_BLK
    win_keys = min(WINDOW + Q_BLK, seq)
    blocks_per_tile = SEL_TILE // L_SEL

    qt = jnp.concatenate([qt_ref[0, g] for g in range(GQA)], axis=1)
    q0 = qi * Q_BLK
    t_q = q0 + lax.broadcasted_iota(jnp.int32, (1, Q_BLK), 1)
    t_all = jnp.concatenate([t_q] * GQA, axis=1)

    s_c = _dot(kc_ref[0], qt)
    cmp_end = lax.broadcasted_iota(jnp.int32, (n_cmp, lanes), 0) * STRIDE_CMP + (L_CMP - 1)
    p_c, l_c = _softmax_cols(s_c, cmp_end <= t_all)
    p_c = p_c / l_c
    o_c = _dot(vct_ref[0], p_c.astype(BF16))
    p_sum = p_c[:, 0:Q_BLK]
    for g in range(1, GQA):
        p_sum = p_sum + p_c[:, g * Q_BLK:(g + 1) * Q_BLK]
    ps_hi = p_sum.astype(BF16)
    ps_lo = (p_sum - ps_hi.astype(F32)).astype(BF16)
    imp2 = _dot(ovt_ref[...], jnp.concatenate([ps_hi, ps_lo], axis=1))
    imp = imp2[:, :Q_BLK] + imp2[:, Q_BLK:]

    blk = lax.broadcasted_iota(jnp.int32, (n_sel, Q_BLK), 0).astype(F32)
    cur = (t_q // L_SEL).astype(F32)
    forced = (blk == 0.0) | (blk == cur) | (blk == cur - 1.0)
    val = jnp.where(forced, BIG, jnp.where(blk <= cur, imp, -BIG))
    sel = jnp.zeros((n_sel, Q_BLK), F32)
    for _ in range(min(TOP_K, n_sel)):
        mx = jnp.max(val, axis=0, keepdims=True)
        first = jnp.min(jnp.where(val == mx, blk, float(n_sel)), axis=0, keepdims=True)
        pick = blk == first
        sel = jnp.where(pick, 1.0, sel)
        val = jnp.where(pick, -jnp.inf, val)
    sel_ref[...] = sel

    n_tiles = (q0 + Q_BLK + SEL_TILE - 1) // SEL_TILE
    k_iota = lax.broadcasted_iota(jnp.int32, (SEL_TILE, Q_BLK), 0)

    def sel_tile(kt, carry):
        m_prev, l_prev, acc_prev = carry
        k0 = pl.multiple_of(kt * SEL_TILE, SEL_TILE)
        s_t = _dot(ks_ref[0, pl.ds(k0, SEL_TILE), :], qt)
        vt = vst_ref[0, :, pl.ds(k0, SEL_TILE)]
        rows = [jnp.broadcast_to(sel_ref[pl.ds(kt * blocks_per_tile + j, 1), :], (L_SEL, Q_BLK))
                for j in range(blocks_per_tile)]
        mask = (jnp.concatenate(rows, axis=0) > 0.5) & ((k0 + k_iota) <= t_q)
        m_out, l_out, acc_out = [], [], []
        for g in range(GQA):
            sl = slice(g * Q_BLK, (g + 1) * Q_BLK)
            s_g = jnp.where(mask, s_t[:, sl], -jnp.inf)
            m_old = m_prev[:, sl]
            m_new = jnp.maximum(m_old, jnp.max(s_g, axis=0, keepdims=True))
            m_safe = jnp.where(m_new == -jnp.inf, 0.0, m_new)
            alpha = jnp.exp(m_old - m_safe)
            p = jnp.exp(s_g - m_safe)
            l_out.append(alpha * l_prev[:, sl] + jnp.sum(p, axis=0, keepdims=True))
            acc_out.append(alpha * acc_prev[:, sl] + _dot(vt, p.astype(BF16)))
            m_out.append(m_new)
        return (jnp.concatenate(m_out, axis=1), jnp.concatenate(l_out, axis=1),
                jnp.concatenate(acc_out, axis=1))

    init = (jnp.full((1, lanes), -jnp.inf, F32), jnp.zeros((1, lanes), F32),
            jnp.zeros((HEAD_DIM, lanes), F32))
    _, l_s, acc_s = lax.fori_loop(0, n_tiles, sel_tile, init)
    o_s = acc_s / jnp.maximum(l_s, TINY)

    start = pl.multiple_of(jnp.maximum(q0 + Q_BLK - win_keys, 0), Q_BLK)
    s_w = _dot(kw_ref[0, pl.ds(start, win_keys), :], qt)
    kpos = start + lax.broadcasted_iota(jnp.int32, (win_keys, lanes), 0)
    p_w, l_w = _softmax_cols(s_w, (kpos <= t_all) & (kpos > t_all - WINDOW))
    o_w = _dot(vwt_ref[0, :, pl.ds(start, win_keys)], p_w.astype(BF16)) / l_w

    for g in range(GQA):
        sl = slice(g * Q_BLK, (g + 1) * Q_BLK)
        gate = jax.nn.sigmoid(gt_ref[0, g])
        o_ref[0, g] = gate[0:1] * o_c[:, sl] + gate[1:2] * o_s[:, sl] + gate[2:3] * o_w[:, sl]


def nsa_attention(qt, kc, vct, ks, vst, kw, vwt, gt):
    bh, _, _, seq = qt.shape
    n_cmp = seq // STRIDE_CMP
    n_sel = seq // L_SEL
    cmp_start = jnp.arange(n_cmp) * STRIDE_CMP
    sel_start = jnp.arange(n_sel) * L_SEL
    ovt = ((cmp_start[None, :] < sel_start[:, None] + L_SEL)
           & (cmp_start[None, :] + L_CMP > sel_start[:, None])).astype(BF16)
    per_bh = lambda shape: pl.BlockSpec((1,) + shape, lambda b, i: (b,) + (0,) * len(shape))
    return pl.pallas_call(
        functools.partial(_attn_kernel, seq=seq),
        grid=(bh, seq // Q_BLK),
        in_specs=[pl.BlockSpec((1, GQA, HEAD_DIM, Q_BLK), lambda b, i: (b, 0, 0, i)),
                  per_bh((n_cmp, HEAD_DIM)), per_bh((HEAD_DIM, n_cmp)),
                  per_bh((seq, HEAD_DIM)), per_bh((HEAD_DIM, seq)),
                  per_bh((seq, HEAD_DIM)), per_bh((HEAD_DIM, seq)),
                  pl.BlockSpec((1, GQA, 3, Q_BLK), lambda b, i: (b, 0, 0, i)),
                  pl.BlockSpec((n_sel, n_cmp), lambda b, i: (0, 0))],
        out_specs=pl.BlockSpec((1, GQA, HEAD_DIM, Q_BLK), lambda b, i: (b, 0, 0, i)),
        out_shape=jax.ShapeDtypeStruct((bh, GQA, HEAD_DIM, seq), F32),
        scratch_shapes=[pltpu.VMEM((n_sel, Q_BLK), F32)],
        compiler_params=_params("parallel", "arbitrary"),
        name="nsa_attention",
    )(qt, kc, vct, ks, vst, kw, vwt, gt, ovt)


def _ssm_kernel(u_ref, pt_ref, bq_ref, w_ref, v_ref, a1_ref, a2_ref, o_ref, mt_ref, *, chunks_per_seq):
    width = SSM_CHUNK * SSM_CH
    u = u_ref[0]
    n_chunks = u.shape[0]
    kt = _dot_split(bq_ref[0], pt_ref[0])
    col = lax.broadcasted_iota(jnp.int32, (SSM_CH, width), 1)
    for s in range(SSM_CHUNK):
        shifted = kt if s == 0 else pltpu.roll(kt, SSM_CH * s, axis=1)
        mt_ref[s * SSM_CH:(s + 1) * SSM_CH, :] = jnp.where(col >= SSM_CH * s, shifted, 0.0).astype(BF16)
    y = _dot(u, mt_ref[...])
    x = _dot(u, w_ref[0])
    row = lax.broadcasted_iota(jnp.int32, x.shape, 0) % chunks_per_seq
    steps = int(math.log2(chunks_per_seq))
    for j in range(steps):
        d = 1 << j
        sh = jnp.where(row >= d, pltpu.roll(x, d, axis=0), 0.0)
        x = x + a1_ref[0, j:j + 1, :] * sh + a2_ref[0, j:j + 1, :] * pltpu.roll(sh, SSM_STATE, axis=1)
    prev = jnp.where(row >= 1, pltpu.roll(x, 1, axis=0), 0.0)
    o_ref[0] = y + _dot(prev.astype(BF16), v_ref[0])


def s5_scan(u2, pt, bq, w, v, a1, a2, chunks_per_seq):
    g, n_chunks, width = u2.shape
    per_g = lambda a: pl.BlockSpec((1,) + a.shape[1:], lambda i: (i,) + (0,) * (a.ndim - 1))
    return pl.pallas_call(
        functools.partial(_ssm_kernel, chunks_per_seq=chunks_per_seq),
        grid=(g,),
        in_specs=[per_g(u2), per_g(pt), per_g(bq), per_g(w), per_g(v), per_g(a1), per_g(a2)],
        out_specs=pl.BlockSpec((1, n_chunks, width), lambda i: (i, 0, 0)),
        out_shape=jax.ShapeDtypeStruct((g, n_chunks, width), F32),
        scratch_shapes=[pltpu.VMEM((width, width), BF16)],
        compiler_params=_params("parallel"),
        name="s5_scan",
    )(u2, pt, bq, w, v, a1, a2)


def _s5_tables(log_dt, a_re, a_im, b_re, b_im, c_re, c_im, chunks_per_seq):
    g, n = a_re.shape
    dt = jnp.exp(log_dt)[:, None]
    lam_re, lam_im = dt * a_re, dt * a_im

    def power(k):
        k = k.astype(F32)[None, :, None]
        mag = jnp.exp(k * lam_re[:, None, :])
        return mag * jnp.cos(k * lam_im[:, None, :]), mag * jnp.sin(k * lam_im[:, None, :])

    abar_re, abar_im = (x[:, 0] for x in power(jnp.ones((1,))))
    den = a_re * a_re + a_im * a_im
    nr = abar_re - 1.0
    f_re = (nr * a_re + abar_im * a_im) / den
    f_im = (abar_im * a_re - nr * a_im) / den
    bb_re = f_re[..., None] * b_re - f_im[..., None] * b_im
    bb_im = f_re[..., None] * b_im + f_im[..., None] * b_re
    steps = jnp.arange(SSM_CHUNK)
    width = SSM_CHUNK * SSM_CH
    e_re, e_im = power(steps)
    ca_re = c_re[:, None] * e_re[:, :, None, :] - c_im[:, None] * e_im[:, :, None, :]
    ca_im = c_re[:, None] * e_im[:, :, None, :] + c_im[:, None] * e_re[:, :, None, :]
    pt = jnp.concatenate([ca_re, -ca_im], axis=-1).reshape(g, width, 2 * n).transpose(0, 2, 1)
    bq = jnp.concatenate([bb_re, bb_im], axis=1).transpose(0, 2, 1)
    r_re, r_im = power(SSM_CHUNK - 1 - steps)
    w_re = r_re[:, :, None, :] * bb_re.transpose(0, 2, 1)[:, None] - r_im[:, :, None, :] * bb_im.transpose(0, 2, 1)[:, None]
    w_im = r_re[:, :, None, :] * bb_im.transpose(0, 2, 1)[:, None] + r_im[:, :, None, :] * bb_re.transpose(0, 2, 1)[:, None]
    w = jnp.concatenate([w_re, w_im], axis=-1).reshape(g, width, 2 * n).astype(BF16)
    n_re, n_im = power(steps + 1)
    cv_re = c_re[:, None] * n_re[:, :, None, :] - c_im[:, None] * n_im[:, :, None, :]
    cv_im = c_re[:, None] * n_im[:, :, None, :] + c_im[:, None] * n_re[:, :, None, :]
    v = jnp.concatenate([cv_re, -cv_im], axis=-1).reshape(g, width, 2 * n).transpose(0, 2, 1).astype(BF16)
    n_steps = int(math.log2(chunks_per_seq))
    s_re, s_im = power(SSM_CHUNK * (2 ** jnp.arange(max(n_steps, 1))))
    a1 = jnp.concatenate([s_re, s_re], axis=-1)
    a2 = jnp.concatenate([-s_im, s_im], axis=-1)
    return pt, bq, w, v, a1, a2


def _mix_out_kernel(x_ref, oa_ref, ys_ref, u_ref, dsk_ref, wglu_ref, bglu_ref, na_ref, ns_ref,
                    woa_ref, wos_ref, o_ref):
    d_ssm = u_ref.shape[1]
    y = jax.nn.gelu(ys_ref[...] + dsk_ref[...] * u_ref[...])
    z = _dot(y.astype(BF16), wglu_ref[...]) + bglu_ref[...]
    o_ssm = z[:, :d_ssm] * jax.nn.sigmoid(z[:, d_ssm:])
    att_n = _rms(oa_ref[...], na_ref[...]).astype(BF16)
    ssm_n = _rms(o_ssm, ns_ref[...]).astype(BF16)
    o_ref[...] = x_ref[...] + _dot(att_n, woa_ref[...]) + _dot(ssm_n, wos_ref[...])


def mix_out(x2, o_att, y_ssm, u, d_skip, w_glu, b_glu, n_att, n_ssm, w_out):
    t, d = x2.shape
    d_att = o_att.shape[1]
    d_ssm = u.shape[1]
    tm = _row_tile(t)
    rows = lambda w: pl.BlockSpec((tm, w), lambda i: (i, 0))
    full = lambda a: pl.BlockSpec(a.shape, lambda i: (0,) * a.ndim)
    consts = [d_skip.reshape(1, d_ssm), w_glu.astype(BF16), b_glu.reshape(1, 2 * d_ssm),
              n_att.reshape(1, d_att), n_ssm.reshape(1, d_ssm),
              w_out[:d_att].astype(BF16), w_out[d_att:].astype(BF16)]
    return pl.pallas_call(
        _mix_out_kernel,
        grid=(t // tm,),
        in_specs=[rows(d), rows(d_att), rows(d_ssm), rows(d_ssm)] + [full(c) for c in consts],
        out_specs=rows(d),
        out_shape=jax.ShapeDtypeStruct((t, d), F32),
        compiler_params=_params("parallel"),
        name="mix_out",
    )(x2, o_att, y_ssm, u, *consts)


def _ffn_kernel(x_ref, g_ref, wg_ref, wu_ref, wd_ref, fg_ref, o_ref, acc_ref, *, f_chunk, final_norm):
    x = x_ref[...]
    h = _rms(x, g_ref[...]).astype(BF16)
    d_ff = wg_ref.shape[1]
    acc_ref[...] = x
    for c in range(d_ff // f_chunk):
        sl = slice(c * f_chunk, (c + 1) * f_chunk)
        a = jax.nn.silu(_dot(h, wg_ref[:, sl])) * _dot(h, wu_ref[:, sl])
        acc_ref[...] += _dot(a.astype(BF16), wd_ref[sl, :])
    out = acc_ref[...]
    if final_norm:
        out = _rms(out, fg_ref[...])
    o_ref[...] = out


def ffn(x2, g, w_gate, w_up, w_down, final_g, final_norm):
    t, d = x2.shape
    d_ff = w_gate.shape[1]
    tm = _row_tile(t)
    f_chunk = 256 if d_ff % 256 == 0 else d_ff
    full = lambda a: pl.BlockSpec(a.shape, lambda i: (0,) * a.ndim)
    consts = [g.reshape(1, d), w_gate.astype(BF16), w_up.astype(BF16), w_down.astype(BF16),
              final_g.reshape(1, d)]
    return pl.pallas_call(
        functools.partial(_ffn_kernel, f_chunk=f_chunk, final_norm=final_norm),
        grid=(t // tm,),
        in_specs=[pl.BlockSpec((tm, d), lambda i: (i, 0))] + [full(c) for c in consts],
        out_specs=pl.BlockSpec((tm, d), lambda i: (i, 0)),
        out_shape=jax.ShapeDtypeStruct((t, d), F32),
        scratch_shapes=[pltpu.VMEM((tm, d), F32)],
        compiler_params=_params("parallel"),
        name="ffn",
    )(x2, *consts)


def _mixer(x2, bsz, seq, attn_norm_g, w_in, cmp_pe, cmp_w1, cmp_b1, cmp_w2, cmp_b2, log_dt, a_re, a_im,
           b_re, b_im, c_re, c_im, d_skip, w_glu, b_glu, mix_norm_att, mix_norm_ssm, w_out):
    t, d = x2.shape
    p_kv = N_KV * HEAD_DIM
    p_gate = 3 * N_HEADS
    d_ssm = d - D_ATT
    p_total = w_in.shape[1]
    p_pad = -(-p_total // 256) * 256
    w_pad = jnp.pad(w_in, ((0, 0), (0, p_pad - p_total))).astype(BF16)
    proj = in_proj(x2, attn_norm_g, w_pad)

    bh = bsz * N_KV
    offs = [0, D_ATT]
    for _ in range(6):
        offs.append(offs[-1] + p_kv)
    offs.append(offs[-1] + p_gate)
    offs.append(offs[-1] + d_ssm)
    q, kc, vc, ksl, vsl, kwn, vwn, g, u = (proj[:, offs[i]:offs[i + 1]] for i in range(9))

    def heads_nat(a):
        return a.reshape(bsz, seq, N_KV, HEAD_DIM).transpose(0, 2, 1, 3).reshape(bh, seq, HEAD_DIM)

    def heads_t(a):
        return a.reshape(bsz, seq, N_KV, HEAD_DIM).transpose(0, 2, 3, 1).reshape(bh, HEAD_DIM, seq)

    qt = (q * (HEAD_DIM ** -0.5)).reshape(bsz, seq, N_KV, GQA, HEAD_DIM).transpose(0, 2, 3, 4, 1)
    qt = qt.reshape(bh, GQA, HEAD_DIM, seq).astype(BF16)
    gt = g.reshape(bsz, seq, N_KV, GQA, 3).transpose(0, 2, 3, 4, 1).reshape(bh, GQA, 3, seq)
    n_chunk = seq // STRIDE_CMP
    kc_ch = heads_nat(kc).reshape(bh, n_chunk, STRIDE_CMP * HEAD_DIM)
    vc_ch = heads_nat(vc).reshape(bh, n_chunk, STRIDE_CMP * HEAD_DIM)
    k_c = compress(kc_ch, cmp_pe[0], cmp_w1[0], cmp_b1[0], cmp_w2[0], cmp_b2[0], transposed=False)
    v_ct = compress(vc_ch, cmp_pe[1], cmp_w1[1], cmp_b1[1], cmp_w2[1], cmp_b2[1], transposed=True)
    o_t = nsa_attention(qt, k_c.astype(BF16), v_ct.astype(BF16),
                        heads_nat(ksl).astype(BF16), heads_t(vsl).astype(BF16),
                        heads_nat(kwn).astype(BF16), heads_t(vwn).astype(BF16), gt)
    o_att = o_t.reshape(bsz, N_KV, GQA, HEAD_DIM, seq).transpose(0, 4, 1, 2, 3).reshape(t, D_ATT)

    n_groups = d_ssm // SSM_CH
    chunks_per_seq = seq // SSM_CHUNK
    n_chunks = t // SSM_CHUNK
    width = SSM_CHUNK * SSM_CH
    u2 = u.reshape(n_chunks, SSM_CHUNK, n_groups, SSM_CH).transpose(2, 0, 1, 3).reshape(n_groups, n_chunks, width)
    tables = _s5_tables(log_dt, a_re, a_im, b_re, b_im, c_re, c_im, chunks_per_seq)
    y2 = s5_scan(u2.astype(BF16), *tables, chunks_per_seq)
    y_ssm = y2.reshape(n_groups, n_chunks, SSM_CHUNK, SSM_CH).transpose(1, 2, 0, 3).reshape(t, d_ssm)

    return mix_out(x2, o_att, y_ssm, u, d_skip, w_glu, b_glu, mix_norm_att, mix_norm_ssm, w_out)


def kernel(x, attn_norm_g, w_in, cmp_pe, cmp_w1, cmp_b1, cmp_w2, cmp_b2, log_dt, a_re, a_im, b_re, b_im, c_re, c_im, d_skip, w_glu, b_glu, mix_norm_att, mix_norm_ssm, w_out, ffn_norm_g, w_gate, w_up, w_down, final_norm_g):
    bsz, seq, d = x.shape
    depth = w_in.shape[0]
    x2 = x.reshape(bsz * seq, d)
    for l in range(depth):
        x2 = _mixer(x2, bsz, seq, attn_norm_g[l], w_in[l], cmp_pe[l], cmp_w1[l], cmp_b1[l], cmp_w2[l],
                    cmp_b2[l], log_dt[l], a_re[l], a_im[l], b_re[l], b_im[l], c_re[l], c_im[l], d_skip[l],
                    w_glu[l], b_glu[l], mix_norm_att[l], mix_norm_ssm[l], w_out[l])
        x2 = ffn(x2, ffn_norm_g[l], w_gate[l], w_up[l], w_down[l], final_norm_g, final_norm=(l == depth - 1))
    return x2.reshape(bsz, seq, d)
```

```python
import functools
import math

import jax
import jax.numpy as jnp
from jax import lax
from jax.experimental import pallas as pl
from jax.experimental.pallas import tpu as pltpu

HEAD_DIM = 64
N_KV = 3
GQA = 4
N_HEADS = N_KV * GQA
D_ATT = N_HEADS * HEAD_DIM
SSM_CH = 16
SSM_STATE = 64
L_CMP = 32
STRIDE_CMP = 16
CMP_HID = 256
L_SEL = 64
TOP_K = 16
N_FORCED = 3
WINDOW = 512
Q_BLK = 128
BIG = 1e9
TINY = 1e-30
EPS = 1e-6
LOG2E = 1.4426950408889634

SUBLANES = 8
V_ROWS = HEAD_DIM + 2 * SUBLANES
SSM_CHUNK = 64
SEL_TILE = 2048
SEL_SUB = 512
VMEM_LIMIT = 56 * 1024 * 1024

F32 = jnp.float32
BF16 = jnp.bfloat16
NEG_INF = float("-inf")


def _dot(a, b):
    return jnp.dot(a, b, preferred_element_type=F32)


def _dot_nt(a, b):
    return lax.dot_general(a, b, (((1,), (1,)), ((), ())), preferred_element_type=F32)


def _dot_split(a, b):
    a_hi = a.astype(BF16)
    a_lo = (a - a_hi.astype(F32)).astype(BF16)
    b_hi = b.astype(BF16)
    b_lo = (b - b_hi.astype(F32)).astype(BF16)
    return _dot(a_hi, b_hi) + _dot(a_hi, b_lo) + _dot(a_lo, b_hi)


def _params(*sem):
    return pltpu.CompilerParams(dimension_semantics=sem, vmem_limit_bytes=VMEM_LIMIT)


def _row_tile(t):
    for tm in (512, 256, 128, 64, 32, 16, 8):
        if t % tm == 0:
            return tm
    raise ValueError(f"token count {t} must be a multiple of 8")


def _rms(x, g):
    ms = jnp.mean(x * x, axis=-1, keepdims=True)
    return x * lax.rsqrt(ms + EPS) * g


def _in_proj_kernel(x_ref, g_ref, w_ref, o_ref):
    h = _rms(x_ref[...], g_ref[...]).astype(BF16)
    o_ref[...] = _dot(h, w_ref[...])


def in_proj(x2, g, w):
    t, d = x2.shape
    p = w.shape[1]
    tm = _row_tile(t)
    return pl.pallas_call(
        _in_proj_kernel,
        grid=(t // tm,),
        in_specs=[pl.BlockSpec((tm, d), lambda i: (i, 0)),
                  pl.BlockSpec((1, d), lambda i: (0, 0)),
                  pl.BlockSpec((d, p), lambda i: (0, 0))],
        out_specs=pl.BlockSpec((tm, p), lambda i: (i, 0)),
        out_shape=jax.ShapeDtypeStruct((t, p), F32),
        compiler_params=_params("parallel"),
        name="in_proj",
    )(x2, g.reshape(1, d), w)


def _compress_kernel(ch_ref, pe_ref, w1a_ref, w1b_ref, b1_ref, w2_ref, b2_ref, o_ref, *, transposed):
    c = ch_ref[0]
    n = c.shape[0]
    a = _dot((c + pe_ref[0:1, :]).astype(BF16), w1a_ref[...])
    b = _dot((c + pe_ref[1:2, :]).astype(BF16), w1b_ref[...])
    hid = a + pltpu.roll(b, n - 1, axis=0) + b1_ref[...]
    hid = jax.nn.gelu(hid).astype(BF16)
    if transposed:
        o_ref[0] = (_dot_nt(w2_ref[...], hid) + b2_ref[...]).astype(o_ref.dtype)
    else:
        o_ref[0] = (_dot(hid, w2_ref[...]) + b2_ref[...]).astype(o_ref.dtype)


def compress(ch, pe, w1, b1, w2, b2, transposed):
    bh, n, width = ch.shape
    half = width
    w1a = w1[:half].astype(BF16)
    w1b = w1[half:].astype(BF16)
    pe2 = pe.reshape(2, half)
    if transposed:
        pad = V_ROWS - HEAD_DIM
        w2k = jnp.pad(w2.T, ((0, pad), (0, 0))).astype(BF16)
        b2k = jnp.concatenate([b2, jnp.ones((1,), F32), jnp.zeros((pad - 1,), F32)]).reshape(V_ROWS, 1)
        out_block, out_shape = (1, V_ROWS, n), (bh, V_ROWS, n)
    else:
        w2k = w2.astype(BF16)
        b2k = b2.reshape(1, HEAD_DIM)
        out_block, out_shape = (1, n, HEAD_DIM), (bh, n, HEAD_DIM)
    full = lambda a: pl.BlockSpec(a.shape, lambda i: (0,) * a.ndim)
    b1k = b1.reshape(1, CMP_HID)
    return pl.pallas_call(
        functools.partial(_compress_kernel, transposed=transposed),
        grid=(bh,),
        in_specs=[pl.BlockSpec((1, n, width), lambda i: (i, 0, 0)),
                  full(pe2), full(w1a), full(w1b), full(b1k), full(w2k), full(b2k)],
        out_specs=pl.BlockSpec(out_block, lambda i: (i, 0, 0)),
        out_shape=jax.ShapeDtypeStruct(out_shape, BF16),
        compiler_params=_params("parallel"),
        name="compress_t" if transposed else "compress_n",
    )(ch, pe2, w1a, w1b, b1k, w2k, b2k)


def _heads(x):
    return [x[:, g * Q_BLK:(g + 1) * Q_BLK] for g in range(GQA)]


def _exp2_safe_max(m):
    return jnp.where(m == NEG_INF, 0.0, m)


def _normalise(pv):
    return pv[:HEAD_DIM] / jnp.maximum(pv[HEAD_DIM:HEAD_DIM + 1], TINY)


def _attn_kernel(qt_ref, kc_ref, vct_ref, ks_ref, vst_ref, kw_ref, vwt_ref, gt_ref, ovt_ref, cthr_ref,
                 o_ref, selb_ref, s_ref, *, seq):
    qi = pl.program_id(1)
    n_sel = seq // L_SEL
    lanes = GQA * Q_BLK
    win_keys = WINDOW + Q_BLK

    qt = jnp.concatenate([qt_ref[0, g] for g in range(GQA)], axis=1)
    q0 = qi * Q_BLK
    lane_q = lax.broadcasted_iota(jnp.int32, (1, Q_BLK), 1)
    t_q = q0 + lane_q
    t_all = jnp.concatenate([t_q] * GQA, axis=1)

    start = pl.multiple_of(jnp.maximum(q0 - WINDOW, 0), Q_BLK)
    s_w = _dot(kw_ref[0, pl.ds(start, win_keys), :], qt)
    r_minus_c = (lax.broadcasted_iota(jnp.int32, (Q_BLK, Q_BLK), 0)
                 - lax.broadcasted_iota(jnp.int32, (Q_BLK, Q_BLK), 1))
    chunks = []
    for i in range(win_keys // Q_BLK):
        hi = q0 - start - i * Q_BLK
        mask = (r_minus_c <= hi) & (r_minus_c > hi - WINDOW)
        chunks.append(jnp.concatenate(
            [jnp.where(mask, x, NEG_INF) for x in _heads(s_w[i * Q_BLK:(i + 1) * Q_BLK])], axis=1))
    m_w = _exp2_safe_max(functools.reduce(jnp.maximum, [jnp.max(x, axis=0, keepdims=True) for x in chunks]))
    p_w = jnp.concatenate([jnp.exp2(x - m_w).astype(BF16) for x in chunks], axis=0)
    o_w = _normalise(_dot(vwt_ref[0, :, pl.ds(start, win_keys)], p_w))

    s_c = _dot(kc_ref[0], qt)

    def sel_scores(k0):
        return _dot(ks_ref[0, pl.ds(k0, SEL_SUB), :], qt)

    s_ref[...] = sel_scores(0)
    vis = cthr_ref[...] <= q0
    p_heads = []
    for s_g in _heads(s_c):
        s_g = jnp.where(vis, s_g, NEG_INF)
        m_g = _exp2_safe_max(jnp.max(s_g, axis=0, keepdims=True))
        p_heads.append(jnp.exp2(s_g - m_g).astype(BF16))
    p_c = jnp.concatenate(p_heads, axis=1)
    pv_c = _dot(vct_ref[0], p_c)
    r_c = 1.0 / jnp.maximum(pv_c[HEAD_DIM:HEAD_DIM + 1], TINY)
    o_c = pv_c[:HEAD_DIM] * r_c
    imp_all = _dot(ovt_ref[...], p_c) * r_c
    imp = functools.reduce(jnp.add, _heads(imp_all))

    blk = lax.broadcasted_iota(jnp.int32, (n_sel, Q_BLK), 0).astype(F32)
    cur = (t_q // L_SEL).astype(F32)
    forced = (blk == 0.0) | (blk == cur) | (blk == cur - 1.0)
    val = jnp.where(forced, NEG_INF, jnp.where(blk <= cur, imp, -BIG))
    for _ in range(min(TOP_K, n_sel) - N_FORCED):
        mx = jnp.max(val, axis=0, keepdims=True)
        first = jnp.min(jnp.where(val == mx, blk, float(n_sel)), axis=0, keepdims=True)
        val = jnp.where(blk == first, NEG_INF, val)
    selb_ref[...] = jnp.where(val == NEG_INF, 0.0, NEG_INF)

    row_iota = lax.broadcasted_iota(jnp.int32, (L_SEL, lanes), 0)

    def sel_update(s_t, k0, m_prev, acc_prev, causal):
        blk0 = k0 // L_SEL
        blocks, biases, part_max = [], [], None
        for j in range(SEL_SUB // L_SEL):
            bias = jnp.concatenate([selb_ref[pl.ds(blk0 + j, 1), :]] * GQA, axis=1)
            s_b = s_t[j * L_SEL:(j + 1) * L_SEL]
            if causal:
                s_b = jnp.where(k0 + j * L_SEL + row_iota <= t_all, s_b, NEG_INF)
            s8 = functools.reduce(jnp.maximum, [s_b[r:r + SUBLANES] for r in range(0, L_SEL, SUBLANES)])
            cand = s8 + bias
            part_max = cand if part_max is None else jnp.maximum(part_max, cand)
            blocks.append(s_b)
            biases.append(bias)
        m_new = jnp.maximum(m_prev, jnp.max(part_max, axis=0, keepdims=True))
        m_safe = _exp2_safe_max(m_new)
        alpha = jnp.exp2(m_prev - m_safe)
        p = jnp.concatenate([jnp.exp2(s_b - (m_safe - bias)).astype(BF16)
                             for s_b, bias in zip(blocks, biases)], axis=0)
        pv = _dot(vst_ref[0, :, pl.ds(k0, SEL_SUB)], p)
        return m_new, alpha * acc_prev + pv

    def sel_tile(k0, m_prev, acc_prev, causal):
        n_sub = SEL_TILE // SEL_SUB
        s_cur = s_ref[...]
        for i in range(n_sub):
            prefetch = not (causal and i == n_sub - 1)
            if prefetch:
                s_next = sel_scores(pl.multiple_of(k0 + (i + 1) * SEL_SUB, SEL_SUB))
            m_prev, acc_prev = sel_update(s_cur, pl.multiple_of(k0 + i * SEL_SUB, SEL_SUB),
                                          m_prev, acc_prev, causal)
            if prefetch:
                s_cur = s_next
        if not causal:
            s_ref[...] = s_cur
        return m_prev, acc_prev

    init = (jnp.full((1, lanes), NEG_INF, F32), jnp.zeros((V_ROWS, lanes), F32))
    n_full = q0 // SEL_TILE
    m_s, acc_s = lax.fori_loop(
        0, n_full, lambda kt, c: sel_tile(pl.multiple_of(kt * SEL_TILE, SEL_TILE), *c, causal=False), init)
    _, acc_s = sel_tile(pl.multiple_of(n_full * SEL_TILE, SEL_TILE), m_s, acc_s, causal=True)
    o_s = _normalise(acc_s)

    for g in range(GQA):
        sl = slice(g * Q_BLK, (g + 1) * Q_BLK)
        gate = jax.nn.sigmoid(gt_ref[0, g])
        o_ref[0, g] = gate[0:1] * o_c[:, sl] + gate[1:2] * o_s[:, sl] + gate[2:3] * o_w[:, sl]


def nsa_attention(qt, kc, vct, ks, vst, kw, vwt, gt):
    bh, _, _, seq = qt.shape
    assert seq % SEL_TILE == 0 and seq >= WINDOW + Q_BLK
    n_cmp = seq // STRIDE_CMP
    n_sel = seq // L_SEL
    cmp_start = jnp.arange(n_cmp) * STRIDE_CMP
    sel_start = jnp.arange(n_sel) * L_SEL
    ovt = ((cmp_start[None, :] < sel_start[:, None] + L_SEL)
           & (cmp_start[None, :] + L_CMP > sel_start[:, None])).astype(BF16)
    cthr = (cmp_start + L_CMP - 1)[:, None] - jnp.arange(Q_BLK)[None, :]
    per_bh = lambda shape: pl.BlockSpec((1,) + shape, lambda b, i: (b,) + (0,) * len(shape))
    return pl.pallas_call(
        functools.partial(_attn_kernel, seq=seq),
        grid=(bh, seq // Q_BLK),
        in_specs=[pl.BlockSpec((1, GQA, HEAD_DIM, Q_BLK), lambda b, i: (b, 0, 0, i)),
                  per_bh((n_cmp, HEAD_DIM)), per_bh((V_ROWS, n_cmp)),
                  per_bh((seq, HEAD_DIM)), per_bh((V_ROWS, seq)),
                  per_bh((seq, HEAD_DIM)), per_bh((V_ROWS, seq)),
                  pl.BlockSpec((1, GQA, 3, Q_BLK), lambda b, i: (b, 0, 0, i)),
                  pl.BlockSpec((n_sel, n_cmp), lambda b, i: (0, 0)),
                  pl.BlockSpec((n_cmp, Q_BLK), lambda b, i: (0, 0))],
        out_specs=pl.BlockSpec((1, GQA, HEAD_DIM, Q_BLK), lambda b, i: (b, 0, 0, i)),
        out_shape=jax.ShapeDtypeStruct((bh, GQA, HEAD_DIM, seq), F32),
        scratch_shapes=[pltpu.VMEM((n_sel, Q_BLK), F32), pltpu.VMEM((SEL_SUB, GQA * Q_BLK), F32)],
        compiler_params=_params("parallel", "arbitrary"),
        name="nsa_attention",
    )(qt, kc, vct, ks, vst, kw, vwt, gt, ovt, cthr.astype(jnp.int32))


def _ssm_kernel(u_ref, pt_ref, bq_ref, w_ref, v_ref, a1_ref, a2_ref, o_ref, mt_ref, *, chunks_per_seq):
    width = SSM_CHUNK * SSM_CH
    u = u_ref[0]
    kt = _dot_split(bq_ref[0], pt_ref[0])
    col = lax.broadcasted_iota(jnp.int32, (SSM_CH, width), 1)
    for s in range(SSM_CHUNK):
        shifted = kt if s == 0 else pltpu.roll(kt, SSM_CH * s, axis=1)
        mt_ref[s * SSM_CH:(s + 1) * SSM_CH, :] = jnp.where(col >= SSM_CH * s, shifted, 0.0).astype(BF16)
    y = _dot(u, mt_ref[...])
    x = _dot(u, w_ref[0])
    row = lax.broadcasted_iota(jnp.int32, x.shape, 0) % chunks_per_seq
    steps = int(math.log2(chunks_per_seq))
    for j in range(steps):
        d = 1 << j
        sh = jnp.where(row >= d, pltpu.roll(x, d, axis=0), 0.0)
        x = x + a1_ref[0, j:j + 1, :] * sh + a2_ref[0, j:j + 1, :] * pltpu.roll(sh, SSM_STATE, axis=1)
    prev = jnp.where(row >= 1, pltpu.roll(x, 1, axis=0), 0.0)
    o_ref[0] = y + _dot(prev.astype(BF16), v_ref[0])


def s5_scan(u2, pt, bq, w, v, a1, a2, chunks_per_seq):
    g, n_chunks, width = u2.shape
    per_g = lambda a: pl.BlockSpec((1,) + a.shape[1:], lambda i: (i,) + (0,) * (a.ndim - 1))
    return pl.pallas_call(
        functools.partial(_ssm_kernel, chunks_per_seq=chunks_per_seq),
        grid=(g,),
        in_specs=[per_g(u2), per_g(pt), per_g(bq), per_g(w), per_g(v), per_g(a1), per_g(a2)],
        out_specs=pl.BlockSpec((1, n_chunks, width), lambda i: (i, 0, 0)),
        out_shape=jax.ShapeDtypeStruct((g, n_chunks, width), F32),
        scratch_shapes=[pltpu.VMEM((width, width), BF16)],
        compiler_params=_params("parallel"),
        name="s5_scan",
    )(u2, pt, bq, w, v, a1, a2)


def _s5_tables(log_dt, a_re, a_im, b_re, b_im, c_re, c_im, chunks_per_seq):
    g, n = a_re.shape
    dt = jnp.exp(log_dt)[:, None]
    lam_re, lam_im = dt * a_re, dt * a_im

    def power(k):
        k = k.astype(F32)[None, :, None]
        mag = jnp.exp(k * lam_re[:, None, :])
        return mag * jnp.cos(k * lam_im[:, None, :]), mag * jnp.sin(k * lam_im[:, None, :])

    abar_re, abar_im = (x[:, 0] for x in power(jnp.ones((1,))))
    den = a_re * a_re + a_im * a_im
    nr = abar_re - 1.0
    f_re = (nr * a_re + abar_im * a_im) / den
    f_im = (abar_im * a_re - nr * a_im) / den
    bb_re = f_re[..., None] * b_re - f_im[..., None] * b_im
    bb_im = f_re[..., None] * b_im + f_im[..., None] * b_re
    steps = jnp.arange(SSM_CHUNK)
    width = SSM_CHUNK * SSM_CH
    e_re, e_im = power(steps)
    ca_re = c_re[:, None] * e_re[:, :, None, :] - c_im[:, None] * e_im[:, :, None, :]
    ca_im = c_re[:, None] * e_im[:, :, None, :] + c_im[:, None] * e_re[:, :, None, :]
    pt = jnp.concatenate([ca_re, -ca_im], axis=-1).reshape(g, width, 2 * n).transpose(0, 2, 1)
    bq = jnp.concatenate([bb_re, bb_im], axis=1).transpose(0, 2, 1)
    r_re, r_im = power(SSM_CHUNK - 1 - steps)
    bt_re, bt_im = bb_re.transpose(0, 2, 1)[:, None], bb_im.transpose(0, 2, 1)[:, None]
    w_re = r_re[:, :, None, :] * bt_re - r_im[:, :, None, :] * bt_im
    w_im = r_re[:, :, None, :] * bt_im + r_im[:, :, None, :] * bt_re
    w = jnp.concatenate([w_re, w_im], axis=-1).reshape(g, width, 2 * n).astype(BF16)
    n_re, n_im = power(steps + 1)
    cv_re = c_re[:, None] * n_re[:, :, None, :] - c_im[:, None] * n_im[:, :, None, :]
    cv_im = c_re[:, None] * n_im[:, :, None, :] + c_im[:, None] * n_re[:, :, None, :]
    v = jnp.concatenate([cv_re, -cv_im], axis=-1).reshape(g, width, 2 * n).transpose(0, 2, 1).astype(BF16)
    n_steps = int(math.log2(chunks_per_seq))
    s_re, s_im = power(SSM_CHUNK * (2 ** jnp.arange(max(n_steps, 1))))
    a1 = jnp.concatenate([s_re, s_re], axis=-1)
    a2 = jnp.concatenate([-s_im, s_im], axis=-1)
    return pt, bq, w, v, a1, a2


def _mix_out_kernel(x_ref, oa_ref, ys_ref, u_ref, dsk_ref, wglu_ref, bglu_ref, na_ref, ns_ref,
                    woa_ref, wos_ref, o_ref):
    d_ssm = u_ref.shape[1]
    y = jax.nn.gelu(ys_ref[...] + dsk_ref[...] * u_ref[...])
    z = _dot(y.astype(BF16), wglu_ref[...]) + bglu_ref[...]
    o_ssm = z[:, :d_ssm] * jax.nn.sigmoid(z[:, d_ssm:])
    att_n = _rms(oa_ref[...], na_ref[...]).astype(BF16)
    ssm_n = _rms(o_ssm, ns_ref[...]).astype(BF16)
    o_ref[...] = x_ref[...] + _dot(att_n, woa_ref[...]) + _dot(ssm_n, wos_ref[...])


def mix_out(x2, o_att, y_ssm, u, d_skip, w_glu, b_glu, n_att, n_ssm, w_out):
    t, d = x2.shape
    d_att = o_att.shape[1]
    d_ssm = u.shape[1]
    tm = _row_tile(t)
    rows = lambda w: pl.BlockSpec((tm, w), lambda i: (i, 0))
    full = lambda a: pl.BlockSpec(a.shape, lambda i: (0,) * a.ndim)
    consts = [d_skip.reshape(1, d_ssm), w_glu.astype(BF16), b_glu.reshape(1, 2 * d_ssm),
              n_att.reshape(1, d_att), n_ssm.reshape(1, d_ssm),
              w_out[:d_att].astype(BF16), w_out[d_att:].astype(BF16)]
    return pl.pallas_call(
        _mix_out_kernel,
        grid=(t // tm,),
        in_specs=[rows(d), rows(d_att), rows(d_ssm), rows(d_ssm)] + [full(c) for c in consts],
        out_specs=rows(d),
        out_shape=jax.ShapeDtypeStruct((t, d), F32),
        compiler_params=_params("parallel"),
        name="mix_out",
    )(x2, o_att, y_ssm, u, *consts)


def _ffn_kernel(x_ref, g_ref, wg_ref, wu_ref, wd_ref, fg_ref, o_ref, acc_ref, *, f_chunk, final_norm):
    x = x_ref[...]
    h = _rms(x, g_ref[...]).astype(BF16)
    d_ff = wg_ref.shape[1]
    acc_ref[...] = x
    for c in range(d_ff // f_chunk):
        sl = slice(c * f_chunk, (c + 1) * f_chunk)
        a = jax.nn.silu(_dot(h, wg_ref[:, sl])) * _dot(h, wu_ref[:, sl])
        acc_ref[...] += _dot(a.astype(BF16), wd_ref[sl, :])
    out = acc_ref[...]
    if final_norm:
        out = _rms(out, fg_ref[...])
    o_ref[...] = out


def ffn(x2, g, w_gate, w_up, w_down, final_g, final_norm):
    t, d = x2.shape
    d_ff = w_gate.shape[1]
    tm = _row_tile(t)
    f_chunk = 256 if d_ff % 256 == 0 else d_ff
    full = lambda a: pl.BlockSpec(a.shape, lambda i: (0,) * a.ndim)
    consts = [g.reshape(1, d), w_gate.astype(BF16), w_up.astype(BF16), w_down.astype(BF16),
              final_g.reshape(1, d)]
    return pl.pallas_call(
        functools.partial(_ffn_kernel, f_chunk=f_chunk, final_norm=final_norm),
        grid=(t // tm,),
        in_specs=[pl.BlockSpec((tm, d), lambda i: (i, 0))] + [full(c) for c in consts],
        out_specs=pl.BlockSpec((tm, d), lambda i: (i, 0)),
        out_shape=jax.ShapeDtypeStruct((t, d), F32),
        scratch_shapes=[pltpu.VMEM((tm, d), F32)],
        compiler_params=_params("parallel"),
        name="ffn",
    )(x2, *consts)


def _mixer(x2, bsz, seq, attn_norm_g, w_in, cmp_pe, cmp_w1, cmp_b1, cmp_w2, cmp_b2, log_dt, a_re, a_im,
           b_re, b_im, c_re, c_im, d_skip, w_glu, b_glu, mix_norm_att, mix_norm_ssm, w_out):
    t, d = x2.shape
    p_kv = N_KV * HEAD_DIM
    p_gate = 3 * N_HEADS
    d_ssm = d - D_ATT
    p_total = w_in.shape[1]
    p_pad = -(-p_total // 256) * 256
    w_pad = jnp.pad(w_in, ((0, 0), (0, p_pad - p_total))).astype(BF16)
    proj = in_proj(x2, attn_norm_g, w_pad)

    bh = bsz * N_KV
    offs = [0, D_ATT]
    for _ in range(6):
        offs.append(offs[-1] + p_kv)
    offs.append(offs[-1] + p_gate)
    offs.append(offs[-1] + d_ssm)
    q, kc, vc, ksl, vsl, kwn, vwn, g, u = (proj[:, offs[i]:offs[i + 1]] for i in range(9))

    def heads_nat(a):
        return a.reshape(bsz, seq, N_KV, HEAD_DIM).transpose(0, 2, 1, 3).reshape(bh, seq, HEAD_DIM)

    def heads_t_ones(a):
        vt = a.reshape(bsz, seq, N_KV, HEAD_DIM).transpose(0, 2, 3, 1).reshape(bh, HEAD_DIM, seq)
        extra = jnp.zeros((bh, V_ROWS - HEAD_DIM, seq), F32).at[:, 0].set(1.0)
        return jnp.concatenate([vt, extra], axis=1).astype(BF16)

    qt = (q * (LOG2E * HEAD_DIM ** -0.5)).reshape(bsz, seq, N_KV, GQA, HEAD_DIM).transpose(0, 2, 3, 4, 1)
    qt = qt.reshape(bh, GQA, HEAD_DIM, seq).astype(BF16)
    gt = g.reshape(bsz, seq, N_KV, GQA, 3).transpose(0, 2, 3, 4, 1).reshape(bh, GQA, 3, seq)
    n_chunk = seq // STRIDE_CMP
    kc_ch = heads_nat(kc).reshape(bh, n_chunk, STRIDE_CMP * HEAD_DIM)
    vc_ch = heads_nat(vc).reshape(bh, n_chunk, STRIDE_CMP * HEAD_DIM)
    k_c = compress(kc_ch, cmp_pe[0], cmp_w1[0], cmp_b1[0], cmp_w2[0], cmp_b2[0], transposed=False)
    v_ct = compress(vc_ch, cmp_pe[1], cmp_w1[1], cmp_b1[1], cmp_w2[1], cmp_b2[1], transposed=True)
    o_t = nsa_attention(qt, k_c, v_ct,
                        heads_nat(ksl).astype(BF16), heads_t_ones(vsl),
                        heads_nat(kwn).astype(BF16), heads_t_ones(vwn), gt)
    o_att = o_t.reshape(bsz, N_KV, GQA, HEAD_DIM, seq).transpose(0, 4, 1, 2, 3).reshape(t, D_ATT)

    n_groups = d_ssm // SSM_CH
    chunks_per_seq = seq // SSM_CHUNK
    n_chunks = t // SSM_CHUNK
    width = SSM_CHUNK * SSM_CH
    u2 = u.reshape(n_chunks, SSM_CHUNK, n_groups, SSM_CH).transpose(2, 0, 1, 3).reshape(n_groups, n_chunks, width)
    tables = _s5_tables(log_dt, a_re, a_im, b_re, b_im, c_re, c_im, chunks_per_seq)
    y2 = s5_scan(u2.astype(BF16), *tables, chunks_per_seq)
    y_ssm = y2.reshape(n_groups, n_chunks, SSM_CHUNK, SSM_CH).transpose(1, 2, 0, 3).reshape(t, d_ssm)

    return mix_out(x2, o_att, y_ssm, u, d_skip, w_glu, b_glu, mix_norm_att, mix_norm_ssm, w_out)


def kernel(x, attn_norm_g, w_in, cmp_pe, cmp_w1, cmp_b1, cmp_w2, cmp_b2, log_dt, a_re, a_im, b_re, b_im, c_re, c_im, d_skip, w_glu, b_glu, mix_norm_att, mix_norm_ssm, w_out, ffn_norm_g, w_gate, w_up, w_down, final_norm_g):
    bsz, seq, d = x.shape
    depth = w_in.shape[0]
    x2 = x.reshape(bsz * seq, d)
    for l in range(depth):
        x2 = _mixer(x2, bsz, seq, attn_norm_g[l], w_in[l], cmp_pe[l], cmp_w1[l], cmp_b1[l], cmp_w2[l],
                    cmp_b2[l], log_dt[l], a_re[l], a_im[l], b_re[l], b_im[l], c_re[l], c_im[l], d_skip[l],
                    w_glu[l], b_glu[l], mix_norm_att[l], mix_norm_ssm[l], w_out[l])
        x2 = ffn(x2, ffn_norm_g[l], w_gate[l], w_up[l], w_down[l], final_norm_g, final_norm=(l == depth - 1))
    return x2.reshape(bsz, seq, d)
```

```python
import functools
import math

import jax
import jax.numpy as jnp
from jax import lax
from jax.experimental import pallas as pl
from jax.experimental.pallas import tpu as pltpu

HEAD_DIM = 64
N_KV = 3
GQA = 4
N_HEADS = N_KV * GQA
D_ATT = N_HEADS * HEAD_DIM
SSM_CH = 16
SSM_STATE = 64
L_CMP = 32
STRIDE_CMP = 16
CMP_HID = 256
L_SEL = 64
TOP_K = 16
N_FORCED = 3
WINDOW = 512
Q_BLK = 128
BIG = 1e9
TINY = 1e-30
EPS = 1e-6
LOG2E = 1.4426950408889634

SUBLANES = 8
V_ROWS = HEAD_DIM + 2 * SUBLANES
SSM_CHUNK = 64
SEL_TILE = 2048
SEL_SUB = 512
VMEM_LIMIT = 56 * 1024 * 1024

F32 = jnp.float32
BF16 = jnp.bfloat16
NEG_INF = float("-inf")


def _dot(a, b):
    return jnp.dot(a, b, preferred_element_type=F32)


def _dot_nt(a, b):
    return lax.dot_general(a, b, (((1,), (1,)), ((), ())), preferred_element_type=F32)


def _dot_split(a, b):
    a_hi = a.astype(BF16)
    a_lo = (a - a_hi.astype(F32)).astype(BF16)
    b_hi = b.astype(BF16)
    b_lo = (b - b_hi.astype(F32)).astype(BF16)
    return _dot(a_hi, b_hi) + _dot(a_hi, b_lo) + _dot(a_lo, b_hi)


def _params(*sem):
    return pltpu.CompilerParams(dimension_semantics=sem, vmem_limit_bytes=VMEM_LIMIT)


def _row_tile(t):
    for tm in (512, 256, 128, 64, 32, 16, 8):
        if t % tm == 0:
            return tm
    raise ValueError(f"token count {t} must be a multiple of 8")


def _rms(x, g):
    ms = jnp.mean(x * x, axis=-1, keepdims=True)
    return x * lax.rsqrt(ms + EPS) * g


def _in_proj_kernel(x_ref, g_ref, wn_ref, wt_ref, qt_ref, vst_ref, vwt_ref, gt_ref, kc_ref, vc_ref,
                    ks_ref, kw_ref, u_ref):
    p_kv = N_KV * HEAD_DIM
    h = _rms(x_ref[0], g_ref[...]).astype(BF16)
    tm = h.shape[0]
    nat = _dot(h, wn_ref[...])
    tr = _dot_nt(wt_ref[...], h)
    qt_ref[0] = (tr[:D_ATT] * (LOG2E * HEAD_DIM ** -0.5)).astype(BF16)
    ones_rows = (lax.broadcasted_iota(jnp.int32, (V_ROWS - HEAD_DIM, tm), 0) == 0).astype(BF16)
    for hd in range(N_KV):
        rows = slice(hd * HEAD_DIM, (hd + 1) * HEAD_DIM)
        for k, ref in enumerate((vst_ref, vwt_ref)):
            off = D_ATT + k * p_kv
            ref[0, hd, :HEAD_DIM, :] = tr[off + hd * HEAD_DIM:off + (hd + 1) * HEAD_DIM].astype(BF16)
            ref[0, hd, HEAD_DIM:, :] = ones_rows
        kc_ref[0, hd] = nat[:, rows]
        vc_ref[0, hd] = nat[:, p_kv + hd * HEAD_DIM:p_kv + (hd + 1) * HEAD_DIM]
        ks_ref[0, hd] = nat[:, 2 * p_kv + hd * HEAD_DIM:2 * p_kv + (hd + 1) * HEAD_DIM].astype(BF16)
        kw_ref[0, hd] = nat[:, 3 * p_kv + hd * HEAD_DIM:3 * p_kv + (hd + 1) * HEAD_DIM].astype(BF16)
    gt_ref[0] = tr[D_ATT + 2 * p_kv:D_ATT + 2 * p_kv + 3 * N_HEADS]
    u_ref[0] = nat[:, 4 * p_kv:]


def in_proj(x3, g, w_in):
    bsz, seq, d = x3.shape
    p_kv = N_KV * HEAD_DIM
    p_gate = 3 * N_HEADS
    d_ssm = d - D_ATT
    o = [0, D_ATT]
    for width in [p_kv] * 6 + [p_gate, d_ssm]:
        o.append(o[-1] + width)
    q, kc, vc, ksl, vsl, kwn, vwn, gl, u = (w_in[:, o[i]:o[i + 1]] for i in range(9))
    w_nat = jnp.concatenate([kc, vc, ksl, kwn, u], axis=1).astype(BF16)
    n_tr = D_ATT + 2 * p_kv + p_gate
    w_tr = jnp.pad(jnp.concatenate([q, vsl, vwn, gl], axis=1), ((0, 0), (0, -n_tr % SUBLANES))).T.astype(BF16)
    tm = _row_tile(seq)
    full = lambda a: pl.BlockSpec(a.shape, lambda b, i: (0,) * a.ndim)
    feat = lambda rows: pl.BlockSpec((1, rows, tm), lambda b, i: (b, 0, i))
    vals = pl.BlockSpec((1, N_KV, V_ROWS, tm), lambda b, i: (b, 0, 0, i))
    keys = pl.BlockSpec((1, N_KV, tm, HEAD_DIM), lambda b, i: (b, 0, i, 0))
    sds = jax.ShapeDtypeStruct
    g2 = g.reshape(1, d)
    return pl.pallas_call(
        _in_proj_kernel,
        grid=(bsz, seq // tm),
        in_specs=[pl.BlockSpec((1, tm, d), lambda b, i: (b, i, 0)), full(g2), full(w_nat), full(w_tr)],
        out_specs=[feat(D_ATT), vals, vals, feat(p_gate), keys, keys, keys, keys,
                   pl.BlockSpec((1, tm, d_ssm), lambda b, i: (b, i, 0))],
        out_shape=[sds((bsz, D_ATT, seq), BF16), sds((bsz, N_KV, V_ROWS, seq), BF16),
                   sds((bsz, N_KV, V_ROWS, seq), BF16), sds((bsz, p_gate, seq), F32),
                   sds((bsz, N_KV, seq, HEAD_DIM), F32), sds((bsz, N_KV, seq, HEAD_DIM), F32),
                   sds((bsz, N_KV, seq, HEAD_DIM), BF16), sds((bsz, N_KV, seq, HEAD_DIM), BF16),
                   sds((bsz, seq, d_ssm), F32)],
        compiler_params=_params("parallel", "parallel"),
        name="in_proj",
    )(x3, g2, w_nat, w_tr)


def _compress_kernel(x_ref, pe_ref, w1_ref, b1_ref, w2_ref, b2_ref, o_ref, *, transposed):
    seq = x_ref.shape[2]
    n = seq // STRIDE_CMP
    first = jnp.zeros((n, CMP_HID), F32)
    second = jnp.zeros((n, CMP_HID), F32)
    for j in range(STRIDE_CMP):
        tok = x_ref[0, 0, pl.ds(j, n, stride=STRIDE_CMP), :]
        w_a = w1_ref[j * HEAD_DIM:(j + 1) * HEAD_DIM, :]
        w_b = w1_ref[(STRIDE_CMP + j) * HEAD_DIM:(STRIDE_CMP + j + 1) * HEAD_DIM, :]
        first = first + _dot((tok + pe_ref[j:j + 1, :]).astype(BF16), w_a)
        second = second + _dot((tok + pe_ref[STRIDE_CMP + j:STRIDE_CMP + j + 1, :]).astype(BF16), w_b)
    hid = first + pltpu.roll(second, n - 1, axis=0) + b1_ref[...]
    hid = jax.nn.gelu(hid).astype(BF16)
    if transposed:
        o_ref[0] = (_dot_nt(w2_ref[...], hid) + b2_ref[...]).astype(o_ref.dtype)
    else:
        o_ref[0] = (_dot(hid, w2_ref[...]) + b2_ref[...]).astype(o_ref.dtype)


def compress(x, pe, w1, b1, w2, b2, transposed):
    bsz, n_kv, seq, _ = x.shape
    bh = bsz * n_kv
    n = seq // STRIDE_CMP
    w1k = w1.astype(BF16)
    if transposed:
        pad = V_ROWS - HEAD_DIM
        w2k = jnp.pad(w2.T, ((0, pad), (0, 0))).astype(BF16)
        b2k = jnp.concatenate([b2, jnp.ones((1,), F32), jnp.zeros((pad - 1,), F32)]).reshape(V_ROWS, 1)
        out_block, out_shape = (1, V_ROWS, n), (bh, V_ROWS, n)
    else:
        w2k = w2.astype(BF16)
        b2k = b2.reshape(1, HEAD_DIM)
        out_block, out_shape = (1, n, HEAD_DIM), (bh, n, HEAD_DIM)
    full = lambda a: pl.BlockSpec(a.shape, lambda i: (0,) * a.ndim)
    b1k = b1.reshape(1, CMP_HID)
    return pl.pallas_call(
        functools.partial(_compress_kernel, transposed=transposed),
        grid=(bh,),
        in_specs=[pl.BlockSpec((1, 1, seq, HEAD_DIM), lambda i: (i // n_kv, i % n_kv, 0, 0)),
                  full(pe), full(w1k), full(b1k), full(w2k), full(b2k)],
        out_specs=pl.BlockSpec(out_block, lambda i: (i, 0, 0)),
        out_shape=jax.ShapeDtypeStruct(out_shape, BF16),
        compiler_params=_params("parallel"),
        name="compress_t" if transposed else "compress_n",
    )(x, pe, w1k, b1k, w2k, b2k)


def _heads(x):
    return [x[:, g * Q_BLK:(g + 1) * Q_BLK] for g in range(GQA)]


def _exp2_safe_max(m):
    return jnp.where(m == NEG_INF, 0.0, m)


def _normalise(pv):
    return pv[:HEAD_DIM] / jnp.maximum(pv[HEAD_DIM:HEAD_DIM + 1], TINY)


def _attn_kernel(qt_ref, kc_ref, vct_ref, ks_ref, vst_ref, kw_ref, vwt_ref, gt_ref, ovt_ref, cthr_ref,
                 o_ref, selb_ref, s_ref, *, seq):
    qi = pl.program_id(1)
    n_sel = seq // L_SEL
    lanes = GQA * Q_BLK
    win_keys = WINDOW + Q_BLK

    qt = jnp.concatenate([qt_ref[0, g] for g in range(GQA)], axis=1)
    q0 = qi * Q_BLK
    lane_q = lax.broadcasted_iota(jnp.int32, (1, Q_BLK), 1)
    t_q = q0 + lane_q
    t_all = jnp.concatenate([t_q] * GQA, axis=1)

    s_c = _dot(kc_ref[0], qt)

    def sel_scores(k0):
        return _dot(ks_ref[0, pl.ds(k0, SEL_SUB), :], qt)

    s_ref[...] = sel_scores(0)
    vis = cthr_ref[...] <= q0
    p_heads = []
    for s_g in _heads(s_c):
        s_g = jnp.where(vis, s_g, NEG_INF)
        m_g = _exp2_safe_max(jnp.max(s_g, axis=0, keepdims=True))
        p_heads.append(jnp.exp2(s_g - m_g).astype(BF16))
    p_c = jnp.concatenate(p_heads, axis=1)
    pv_c = _dot(vct_ref[0], p_c)
    r_c = 1.0 / jnp.maximum(pv_c[HEAD_DIM:HEAD_DIM + 1], TINY)
    o_c = pv_c[:HEAD_DIM] * r_c
    imp_all = _dot(ovt_ref[...], p_c) * r_c
    imp = functools.reduce(jnp.add, _heads(imp_all))

    blk = lax.broadcasted_iota(jnp.int32, (n_sel, Q_BLK), 0).astype(F32)
    cur = (t_q // L_SEL).astype(F32)
    forced = (blk == 0.0) | (blk == cur) | (blk == cur - 1.0)
    val = jnp.where(forced, NEG_INF, jnp.where(blk <= cur, imp, -BIG))
    for _ in range(min(TOP_K, n_sel) - N_FORCED):
        mx = jnp.max(val, axis=0, keepdims=True)
        first = jnp.min(jnp.where(val == mx, blk, float(n_sel)), axis=0, keepdims=True)
        val = jnp.where(blk == first, NEG_INF, val)
    selb_ref[...] = jnp.where(val == NEG_INF, 0.0, NEG_INF)

    start = pl.multiple_of(jnp.maximum(q0 - WINDOW, 0), Q_BLK)
    s_w = _dot(kw_ref[0, pl.ds(start, win_keys), :], qt)
    r_minus_c = (lax.broadcasted_iota(jnp.int32, (Q_BLK, Q_BLK), 0)
                 - lax.broadcasted_iota(jnp.int32, (Q_BLK, Q_BLK), 1))
    chunks = []
    for i in range(win_keys // Q_BLK):
        hi = q0 - start - i * Q_BLK
        mask = (r_minus_c <= hi) & (r_minus_c > hi - WINDOW)
        chunks.append(jnp.concatenate(
            [jnp.where(mask, x, NEG_INF) for x in _heads(s_w[i * Q_BLK:(i + 1) * Q_BLK])], axis=1))
    m_w = _exp2_safe_max(functools.reduce(jnp.maximum, [jnp.max(x, axis=0, keepdims=True) for x in chunks]))
    p_w = jnp.concatenate([jnp.exp2(x - m_w).astype(BF16) for x in chunks], axis=0)
    o_w = _normalise(_dot(vwt_ref[0, :, pl.ds(start, win_keys)], p_w))

    row_iota = lax.broadcasted_iota(jnp.int32, (L_SEL, lanes), 0)

    def sel_update(s_t, k0, m_prev, acc_prev, causal):
        blk0 = k0 // L_SEL
        blocks, biases, part_max = [], [], None
        for j in range(SEL_SUB // L_SEL):
            bias = jnp.concatenate([selb_ref[pl.ds(blk0 + j, 1), :]] * GQA, axis=1)
            s_b = s_t[j * L_SEL:(j + 1) * L_SEL]
            if causal:
                s_b = jnp.where(k0 + j * L_SEL + row_iota <= t_all, s_b, NEG_INF)
            s8 = functools.reduce(jnp.maximum, [s_b[r:r + SUBLANES] for r in range(0, L_SEL, SUBLANES)])
            cand = s8 + bias
            part_max = cand if part_max is None else jnp.maximum(part_max, cand)
            blocks.append(s_b)
            biases.append(bias)
        m_new = jnp.maximum(m_prev, jnp.max(part_max, axis=0, keepdims=True))
        m_safe = _exp2_safe_max(m_new)
        alpha = jnp.exp2(m_prev - m_safe)
        p = jnp.concatenate([jnp.exp2(s_b - (m_safe - bias)).astype(BF16)
                             for s_b, bias in zip(blocks, biases)], axis=0)
        pv = _dot(vst_ref[0, :, pl.ds(k0, SEL_SUB)], p)
        return m_new, alpha * acc_prev + pv

    def sel_tile(k0, m_prev, acc_prev, causal):
        n_sub = SEL_TILE // SEL_SUB
        s_cur = s_ref[...]
        for i in range(n_sub):
            prefetch = not (causal and i == n_sub - 1)
            if prefetch:
                s_next = sel_scores(pl.multiple_of(k0 + (i + 1) * SEL_SUB, SEL_SUB))
            m_prev, acc_prev = sel_update(s_cur, pl.multiple_of(k0 + i * SEL_SUB, SEL_SUB),
                                          m_prev, acc_prev, causal)
            if prefetch:
                s_cur = s_next
        if not causal:
            s_ref[...] = s_cur
        return m_prev, acc_prev

    init = (jnp.full((1, lanes), NEG_INF, F32), jnp.zeros((V_ROWS, lanes), F32))
    n_full = q0 // SEL_TILE
    m_s, acc_s = lax.fori_loop(
        0, n_full, lambda kt, c: sel_tile(pl.multiple_of(kt * SEL_TILE, SEL_TILE), *c, causal=False), init)
    _, acc_s = sel_tile(pl.multiple_of(n_full * SEL_TILE, SEL_TILE), m_s, acc_s, causal=True)
    o_s = _normalise(acc_s)

    outs = []
    for g in range(GQA):
        sl = slice(g * Q_BLK, (g + 1) * Q_BLK)
        gate = jax.nn.sigmoid(gt_ref[0, g])
        outs.append(gate[0:1] * o_c[:, sl] + gate[1:2] * o_s[:, sl] + gate[2:3] * o_w[:, sl])
    pairs = [jnp.concatenate(outs[g:g + 2], axis=0).T for g in range(0, GQA, 2)]
    o_ref[0] = jnp.concatenate(pairs, axis=1)


def nsa_attention(qt, kc, vct, ks, vst, kw, vwt, gt, bsz):
    bh, _, _, seq = qt.shape
    assert seq % SEL_TILE == 0 and seq >= WINDOW + Q_BLK
    n_cmp = seq // STRIDE_CMP
    n_sel = seq // L_SEL
    cmp_start = jnp.arange(n_cmp) * STRIDE_CMP
    sel_start = jnp.arange(n_sel) * L_SEL
    ovt = ((cmp_start[None, :] < sel_start[:, None] + L_SEL)
           & (cmp_start[None, :] + L_CMP > sel_start[:, None])).astype(BF16)
    cthr = (cmp_start + L_CMP - 1)[:, None] - jnp.arange(Q_BLK)[None, :]
    per_bh = lambda shape: pl.BlockSpec((1,) + shape, lambda b, i: (b,) + (0,) * len(shape))
    return pl.pallas_call(
        functools.partial(_attn_kernel, seq=seq),
        grid=(bh, seq // Q_BLK),
        in_specs=[pl.BlockSpec((1, GQA, HEAD_DIM, Q_BLK), lambda b, i: (b, 0, 0, i)),
                  per_bh((n_cmp, HEAD_DIM)), per_bh((V_ROWS, n_cmp)),
                  per_bh((seq, HEAD_DIM)), per_bh((V_ROWS, seq)),
                  per_bh((seq, HEAD_DIM)), per_bh((V_ROWS, seq)),
                  pl.BlockSpec((1, GQA, 3, Q_BLK), lambda b, i: (b, 0, 0, i)),
                  pl.BlockSpec((n_sel, n_cmp), lambda b, i: (0, 0)),
                  pl.BlockSpec((n_cmp, Q_BLK), lambda b, i: (0, 0))],
        out_specs=pl.BlockSpec((1, Q_BLK, GQA * HEAD_DIM), lambda b, i: (b // N_KV, i, b % N_KV)),
        out_shape=jax.ShapeDtypeStruct((bsz, seq, N_KV * GQA * HEAD_DIM), F32),
        scratch_shapes=[pltpu.VMEM((n_sel, Q_BLK), F32), pltpu.VMEM((SEL_SUB, GQA * Q_BLK), F32)],
        compiler_params=_params("parallel", "arbitrary"),
        name="nsa_attention",
    )(qt, kc, vct, ks, vst, kw, vwt, gt, ovt, cthr.astype(jnp.int32))


def _ssm_kernel(u_ref, pt_ref, bq_ref, w_ref, v_ref, a1_ref, a2_ref, o_ref, mt_ref, *, chunks_per_seq):
    width = SSM_CHUNK * SSM_CH
    u = u_ref[0]
    kt = _dot_split(bq_ref[0], pt_ref[0])
    col = lax.broadcasted_iota(jnp.int32, (SSM_CH, width), 1)
    for s in range(SSM_CHUNK):
        shifted = kt if s == 0 else pltpu.roll(kt, SSM_CH * s, axis=1)
        mt_ref[s * SSM_CH:(s + 1) * SSM_CH, :] = jnp.where(col >= SSM_CH * s, shifted, 0.0).astype(BF16)
    y = _dot(u, mt_ref[...])
    x = _dot(u, w_ref[0])
    row = lax.broadcasted_iota(jnp.int32, x.shape, 0) % chunks_per_seq
    steps = int(math.log2(chunks_per_seq))
    for j in range(steps):
        d = 1 << j
        sh = jnp.where(row >= d, pltpu.roll(x, d, axis=0), 0.0)
        x = x + a1_ref[0, j:j + 1, :] * sh + a2_ref[0, j:j + 1, :] * pltpu.roll(sh, SSM_STATE, axis=1)
    prev = jnp.where(row >= 1, pltpu.roll(x, 1, axis=0), 0.0)
    o_ref[0] = y + _dot(prev.astype(BF16), v_ref[0])


def s5_scan(u2, pt, bq, w, v, a1, a2, chunks_per_seq):
    g, n_chunks, width = u2.shape
    per_g = lambda a: pl.BlockSpec((1,) + a.shape[1:], lambda i: (i,) + (0,) * (a.ndim - 1))
    return pl.pallas_call(
        functools.partial(_ssm_kernel, chunks_per_seq=chunks_per_seq),
        grid=(g,),
        in_specs=[per_g(u2), per_g(pt), per_g(bq), per_g(w), per_g(v), per_g(a1), per_g(a2)],
        out_specs=pl.BlockSpec((1, n_chunks, width), lambda i: (i, 0, 0)),
        out_shape=jax.ShapeDtypeStruct((g, n_chunks, width), F32),
        scratch_shapes=[pltpu.VMEM((width, width), BF16)],
        compiler_params=_params("parallel"),
        name="s5_scan",
    )(u2, pt, bq, w, v, a1, a2)


def _s5_tables(log_dt, a_re, a_im, b_re, b_im, c_re, c_im, chunks_per_seq):
    g, n = a_re.shape
    dt = jnp.exp(log_dt)[:, None]
    lam_re, lam_im = dt * a_re, dt * a_im

    def power(k):
        k = k.astype(F32)[None, :, None]
        mag = jnp.exp(k * lam_re[:, None, :])
        return mag * jnp.cos(k * lam_im[:, None, :]), mag * jnp.sin(k * lam_im[:, None, :])

    abar_re, abar_im = (x[:, 0] for x in power(jnp.ones((1,))))
    den = a_re * a_re + a_im * a_im
    nr = abar_re - 1.0
    f_re = (nr * a_re + abar_im * a_im) / den
    f_im = (abar_im * a_re - nr * a_im) / den
    bb_re = f_re[..., None] * b_re - f_im[..., None] * b_im
    bb_im = f_re[..., None] * b_im + f_im[..., None] * b_re
    steps = jnp.arange(SSM_CHUNK)
    width = SSM_CHUNK * SSM_CH
    e_re, e_im = power(steps)
    ca_re = c_re[:, None] * e_re[:, :, None, :] - c_im[:, None] * e_im[:, :, None, :]
    ca_im = c_re[:, None] * e_im[:, :, None, :] + c_im[:, None] * e_re[:, :, None, :]
    pt = jnp.concatenate([ca_re, -ca_im], axis=-1).reshape(g, width, 2 * n).transpose(0, 2, 1)
    bq = jnp.concatenate([bb_re, bb_im], axis=1).transpose(0, 2, 1)
    r_re, r_im = power(SSM_CHUNK - 1 - steps)
    bt_re, bt_im = bb_re.transpose(0, 2, 1)[:, None], bb_im.transpose(0, 2, 1)[:, None]
    w_re = r_re[:, :, None, :] * bt_re - r_im[:, :, None, :] * bt_im
    w_im = r_re[:, :, None, :] * bt_im + r_im[:, :, None, :] * bt_re
    w = jnp.concatenate([w_re, w_im], axis=-1).reshape(g, width, 2 * n).astype(BF16)
    n_re, n_im = power(steps + 1)
    cv_re = c_re[:, None] * n_re[:, :, None, :] - c_im[:, None] * n_im[:, :, None, :]
    cv_im = c_re[:, None] * n_im[:, :, None, :] + c_im[:, None] * n_re[:, :, None, :]
    v = jnp.concatenate([cv_re, -cv_im], axis=-1).reshape(g, width, 2 * n).transpose(0, 2, 1).astype(BF16)
    n_steps = int(math.log2(chunks_per_seq))
    s_re, s_im = power(SSM_CHUNK * (2 ** jnp.arange(max(n_steps, 1))))
    a1 = jnp.concatenate([s_re, s_re], axis=-1)
    a2 = jnp.concatenate([-s_im, s_im], axis=-1)
    return pt, bq, w, v, a1, a2


def _mix_out_kernel(x_ref, oa_ref, ys_ref, u_ref, dsk_ref, wglu_ref, bglu_ref, na_ref, ns_ref,
                    woa_ref, wos_ref, o_ref):
    d_ssm = u_ref.shape[1]
    y = jax.nn.gelu(ys_ref[...] + dsk_ref[...] * u_ref[...])
    z = _dot(y.astype(BF16), wglu_ref[...]) + bglu_ref[...]
    o_ssm = z[:, :d_ssm] * jax.nn.sigmoid(z[:, d_ssm:])
    att_n = _rms(oa_ref[...], na_ref[...]).astype(BF16)
    ssm_n = _rms(o_ssm, ns_ref[...]).astype(BF16)
    o_ref[...] = x_ref[...] + _dot(att_n, woa_ref[...]) + _dot(ssm_n, wos_ref[...])


def mix_out(x2, o_att, y_ssm, u, d_skip, w_glu, b_glu, n_att, n_ssm, w_out):
    t, d = x2.shape
    d_att = o_att.shape[1]
    d_ssm = u.shape[1]
    tm = _row_tile(t)
    rows = lambda w: pl.BlockSpec((tm, w), lambda i: (i, 0))
    full = lambda a: pl.BlockSpec(a.shape, lambda i: (0,) * a.ndim)
    consts = [d_skip.reshape(1, d_ssm), w_glu.astype(BF16), b_glu.reshape(1, 2 * d_ssm),
              n_att.reshape(1, d_att), n_ssm.reshape(1, d_ssm),
              w_out[:d_att].astype(BF16), w_out[d_att:].astype(BF16)]
    return pl.pallas_call(
        _mix_out_kernel,
        grid=(t // tm,),
        in_specs=[rows(d), rows(d_att), rows(d_ssm), rows(d_ssm)] + [full(c) for c in consts],
        out_specs=rows(d),
        out_shape=jax.ShapeDtypeStruct((t, d), F32),
        compiler_params=_params("parallel"),
        name="mix_out",
    )(x2, o_att, y_ssm, u, *consts)


def _ffn_kernel(x_ref, g_ref, wg_ref, wu_ref, wd_ref, fg_ref, o_ref, acc_ref, *, f_chunk, final_norm):
    x = x_ref[...]
    h = _rms(x, g_ref[...]).astype(BF16)
    d_ff = wg_ref.shape[1]
    acc_ref[...] = x
    for c in range(d_ff // f_chunk):
        sl = slice(c * f_chunk, (c + 1) * f_chunk)
        a = jax.nn.silu(_dot(h, wg_ref[:, sl])) * _dot(h, wu_ref[:, sl])
        acc_ref[...] += _dot(a.astype(BF16), wd_ref[sl, :])
    out = acc_ref[...]
    if final_norm:
        out = _rms(out, fg_ref[...])
    o_ref[...] = out


def ffn(x2, g, w_gate, w_up, w_down, final_g, final_norm):
    t, d = x2.shape
    d_ff = w_gate.shape[1]
    tm = _row_tile(t)
    f_chunk = 256 if d_ff % 256 == 0 else d_ff
    full = lambda a: pl.BlockSpec(a.shape, lambda i: (0,) * a.ndim)
    consts = [g.reshape(1, d), w_gate.astype(BF16), w_up.astype(BF16), w_down.astype(BF16),
              final_g.reshape(1, d)]
    return pl.pallas_call(
        functools.partial(_ffn_kernel, f_chunk=f_chunk, final_norm=final_norm),
        grid=(t // tm,),
        in_specs=[pl.BlockSpec((tm, d), lambda i: (i, 0))] + [full(c) for c in consts],
        out_specs=pl.BlockSpec((tm, d), lambda i: (i, 0)),
        out_shape=jax.ShapeDtypeStruct((t, d), F32),
        scratch_shapes=[pltpu.VMEM((tm, d), F32)],
        compiler_params=_params("parallel"),
        name="ffn",
    )(x2, *consts)


def _mixer(x2, bsz, seq, attn_norm_g, w_in, cmp_pe, cmp_w1, cmp_b1, cmp_w2, cmp_b2, log_dt, a_re, a_im,
           b_re, b_im, c_re, c_im, d_skip, w_glu, b_glu, mix_norm_att, mix_norm_ssm, w_out):
    t, d = x2.shape
    d_ssm = d - D_ATT
    bh = bsz * N_KV
    qt, vst, vwt, gt, kc, vc, ks, kw, u = in_proj(x2.reshape(bsz, seq, d), attn_norm_g, w_in)
    k_c = compress(kc, cmp_pe[0], cmp_w1[0], cmp_b1[0], cmp_w2[0], cmp_b2[0], transposed=False)
    v_ct = compress(vc, cmp_pe[1], cmp_w1[1], cmp_b1[1], cmp_w2[1], cmp_b2[1], transposed=True)
    o_att = nsa_attention(qt.reshape(bh, GQA, HEAD_DIM, seq), k_c, v_ct,
                          ks.reshape(bh, seq, HEAD_DIM), vst.reshape(bh, V_ROWS, seq),
                          kw.reshape(bh, seq, HEAD_DIM), vwt.reshape(bh, V_ROWS, seq),
                          gt.reshape(bh, GQA, 3, seq), bsz).reshape(t, D_ATT)

    u = u.reshape(t, d_ssm)
    n_groups = d_ssm // SSM_CH
    chunks_per_seq = seq // SSM_CHUNK
    n_chunks = t // SSM_CHUNK
    width = SSM_CHUNK * SSM_CH
    u2 = u.astype(BF16).reshape(n_chunks, SSM_CHUNK, n_groups, SSM_CH).transpose(2, 0, 1, 3)
    tables = _s5_tables(log_dt, a_re, a_im, b_re, b_im, c_re, c_im, chunks_per_seq)
    y2 = s5_scan(u2.reshape(n_groups, n_chunks, width), *tables, chunks_per_seq)
    y_ssm = y2.reshape(n_groups, n_chunks, SSM_CHUNK, SSM_CH).transpose(1, 2, 0, 3).reshape(t, d_ssm)

    return mix_out(x2, o_att, y_ssm, u, d_skip, w_glu, b_glu, mix_norm_att, mix_norm_ssm, w_out)


def kernel(x, attn_norm_g, w_in, cmp_pe, cmp_w1, cmp_b1, cmp_w2, cmp_b2, log_dt, a_re, a_im, b_re, b_im, c_re, c_im, d_skip, w_glu, b_glu, mix_norm_att, mix_norm_ssm, w_out, ffn_norm_g, w_gate, w_up, w_down, final_norm_g):
    bsz, seq, d = x.shape
    depth = w_in.shape[0]
    x2 = x.reshape(bsz * seq, d)
    for l in range(depth):
        x2 = _mixer(x2, bsz, seq, attn_norm_g[l], w_in[l], cmp_pe[l], cmp_w1[l], cmp_b1[l], cmp_w2[l],
                    cmp_b2[l], log_dt[l], a_re[l], a_im[l], b_re[l], b_im[l], c_re[l], c_im[l], d_skip[l],
                    w_glu[l], b_glu[l], mix_norm_att[l], mix_norm_ssm[l], w_out[l])
        x2 = ffn(x2, ffn_norm_g[l], w_gate[l], w_up[l], w_down[l], final_norm_g, final_norm=(l == depth - 1))
    return x2.reshape(bsz, seq, d)
```

```python
import functools
import math

import jax
import jax.numpy as jnp
from jax import lax
from jax.experimental import pallas as pl
from jax.experimental.pallas import tpu as pltpu

HEAD_DIM = 64
N_KV = 3
GQA = 4
N_HEADS = N_KV * GQA
D_ATT = N_HEADS * HEAD_DIM
SSM_CH = 16
SSM_STATE = 64
L_CMP = 32
STRIDE_CMP = 16
CMP_HID = 256
L_SEL = 64
TOP_K = 16
N_FORCED = 3
WINDOW = 512
Q_BLK = 128
BIG = 1e9
TINY = 1e-30
EPS = 1e-6
LOG2E = 1.4426950408889634

SUBLANES = 8
V_ROWS = HEAD_DIM + 2 * SUBLANES
SSM_CHUNK = 64
SEL_TILE = 2048
SEL_SUB = 512
VMEM_LIMIT = 56 * 1024 * 1024

F32 = jnp.float32
BF16 = jnp.bfloat16
NEG_INF = float("-inf")


def _dot(a, b):
    return jnp.dot(a, b, preferred_element_type=F32)


def _dot_nt(a, b):
    return lax.dot_general(a, b, (((1,), (1,)), ((), ())), preferred_element_type=F32)


def _dot_split(a, b):
    a_hi = a.astype(BF16)
    a_lo = (a - a_hi.astype(F32)).astype(BF16)
    b_hi = b.astype(BF16)
    b_lo = (b - b_hi.astype(F32)).astype(BF16)
    return _dot(a_hi, b_hi) + _dot(a_hi, b_lo) + _dot(a_lo, b_hi)


def _params(*sem):
    return pltpu.CompilerParams(dimension_semantics=sem, vmem_limit_bytes=VMEM_LIMIT)


def _row_tile(t):
    for tm in (512, 256, 128, 64, 32, 16, 8):
        if t % tm == 0:
            return tm
    raise ValueError(f"token count {t} must be a multiple of 8")


def _rms(x, g):
    ms = jnp.mean(x * x, axis=-1, keepdims=True)
    return x * lax.rsqrt(ms + EPS) * g


def _in_proj_kernel(x_ref, g_ref, wn_ref, wt_ref, qt_ref, vst_ref, vwt_ref, gt_ref, kc_ref, vc_ref,
                    ks_ref, kw_ref, u_ref):
    p_kv = N_KV * HEAD_DIM
    h = _rms(x_ref[0], g_ref[...]).astype(BF16)
    tm = h.shape[0]
    nat = _dot(h, wn_ref[...])
    tr = _dot_nt(wt_ref[...], h)
    qt_ref[0] = (tr[:D_ATT] * (LOG2E * HEAD_DIM ** -0.5)).astype(BF16)
    ones_rows = (lax.broadcasted_iota(jnp.int32, (V_ROWS - HEAD_DIM, tm), 0) == 0).astype(BF16)
    for hd in range(N_KV):
        rows = slice(hd * HEAD_DIM, (hd + 1) * HEAD_DIM)
        for k, ref in enumerate((vst_ref, vwt_ref)):
            off = D_ATT + k * p_kv
            ref[0, hd, :HEAD_DIM, :] = tr[off + hd * HEAD_DIM:off + (hd + 1) * HEAD_DIM].astype(BF16)
            ref[0, hd, HEAD_DIM:, :] = ones_rows
        kc_ref[0, hd] = nat[:, rows]
        vc_ref[0, hd] = nat[:, p_kv + hd * HEAD_DIM:p_kv + (hd + 1) * HEAD_DIM]
        ks_ref[0, hd] = nat[:, 2 * p_kv + hd * HEAD_DIM:2 * p_kv + (hd + 1) * HEAD_DIM].astype(BF16)
        kw_ref[0, hd] = nat[:, 3 * p_kv + hd * HEAD_DIM:3 * p_kv + (hd + 1) * HEAD_DIM].astype(BF16)
    gt_ref[0] = tr[D_ATT + 2 * p_kv:D_ATT + 2 * p_kv + 3 * N_HEADS]
    u_ref[0] = nat[:, 4 * p_kv:]


def in_proj(x3, g, w_in):
    bsz, seq, d = x3.shape
    p_kv = N_KV * HEAD_DIM
    p_gate = 3 * N_HEADS
    d_ssm = d - D_ATT
    o = [0, D_ATT]
    for width in [p_kv] * 6 + [p_gate, d_ssm]:
        o.append(o[-1] + width)
    q, kc, vc, ksl, vsl, kwn, vwn, gl, u = (w_in[:, o[i]:o[i + 1]] for i in range(9))
    w_nat = jnp.concatenate([kc, vc, ksl, kwn, u], axis=1).astype(BF16)
    n_tr = D_ATT + 2 * p_kv + p_gate
    w_tr = jnp.pad(jnp.concatenate([q, vsl, vwn, gl], axis=1), ((0, 0), (0, -n_tr % SUBLANES))).T.astype(BF16)
    tm = _row_tile(seq)
    full = lambda a: pl.BlockSpec(a.shape, lambda b, i: (0,) * a.ndim)
    feat = lambda rows: pl.BlockSpec((1, rows, tm), lambda b, i: (b, 0, i))
    vals = pl.BlockSpec((1, N_KV, V_ROWS, tm), lambda b, i: (b, 0, 0, i))
    keys = pl.BlockSpec((1, N_KV, tm, HEAD_DIM), lambda b, i: (b, 0, i, 0))
    sds = jax.ShapeDtypeStruct
    g2 = g.reshape(1, d)
    return pl.pallas_call(
        _in_proj_kernel,
        grid=(bsz, seq // tm),
        in_specs=[pl.BlockSpec((1, tm, d), lambda b, i: (b, i, 0)), full(g2), full(w_nat), full(w_tr)],
        out_specs=[feat(D_ATT), vals, vals, feat(p_gate), keys, keys, keys, keys,
                   pl.BlockSpec((1, tm, d_ssm), lambda b, i: (b, i, 0))],
        out_shape=[sds((bsz, D_ATT, seq), BF16), sds((bsz, N_KV, V_ROWS, seq), BF16),
                   sds((bsz, N_KV, V_ROWS, seq), BF16), sds((bsz, p_gate, seq), F32),
                   sds((bsz, N_KV, seq, HEAD_DIM), F32), sds((bsz, N_KV, seq, HEAD_DIM), F32),
                   sds((bsz, N_KV, seq, HEAD_DIM), BF16), sds((bsz, N_KV, seq, HEAD_DIM), BF16),
                   sds((bsz, seq, d_ssm), F32)],
        compiler_params=_params("parallel", "parallel"),
        name="in_proj",
    )(x3, g2, w_nat, w_tr)


def _compress_kernel(x_ref, pe_ref, w1_ref, b1_ref, w2_ref, b2_ref, o_ref, *, transposed):
    seq = x_ref.shape[2]
    n = seq // STRIDE_CMP
    first = jnp.zeros((n, CMP_HID), F32)
    second = jnp.zeros((n, CMP_HID), F32)
    for j in range(STRIDE_CMP):
        tok = x_ref[0, 0, pl.ds(j, n, stride=STRIDE_CMP), :]
        w_a = w1_ref[j * HEAD_DIM:(j + 1) * HEAD_DIM, :]
        w_b = w1_ref[(STRIDE_CMP + j) * HEAD_DIM:(STRIDE_CMP + j + 1) * HEAD_DIM, :]
        first = first + _dot((tok + pe_ref[j:j + 1, :]).astype(BF16), w_a)
        second = second + _dot((tok + pe_ref[STRIDE_CMP + j:STRIDE_CMP + j + 1, :]).astype(BF16), w_b)
    hid = first + pltpu.roll(second, n - 1, axis=0) + b1_ref[...]
    hid = jax.nn.gelu(hid).astype(BF16)
    if transposed:
        o_ref[0] = (_dot_nt(w2_ref[...], hid) + b2_ref[...]).astype(o_ref.dtype)
    else:
        o_ref[0] = (_dot(hid, w2_ref[...]) + b2_ref[...]).astype(o_ref.dtype)


def compress(x, pe, w1, b1, w2, b2, transposed):
    bsz, n_kv, seq, _ = x.shape
    bh = bsz * n_kv
    n = seq // STRIDE_CMP
    w1k = w1.astype(BF16)
    if transposed:
        pad = V_ROWS - HEAD_DIM
        w2k = jnp.pad(w2.T, ((0, pad), (0, 0))).astype(BF16)
        b2k = jnp.concatenate([b2, jnp.ones((1,), F32), jnp.zeros((pad - 1,), F32)]).reshape(V_ROWS, 1)
        out_block, out_shape = (1, V_ROWS, n), (bh, V_ROWS, n)
    else:
        w2k = w2.astype(BF16)
        b2k = b2.reshape(1, HEAD_DIM)
        out_block, out_shape = (1, n, HEAD_DIM), (bh, n, HEAD_DIM)
    full = lambda a: pl.BlockSpec(a.shape, lambda i: (0,) * a.ndim)
    b1k = b1.reshape(1, CMP_HID)
    return pl.pallas_call(
        functools.partial(_compress_kernel, transposed=transposed),
        grid=(bh,),
        in_specs=[pl.BlockSpec((1, 1, seq, HEAD_DIM), lambda i: (i // n_kv, i % n_kv, 0, 0)),
                  full(pe), full(w1k), full(b1k), full(w2k), full(b2k)],
        out_specs=pl.BlockSpec(out_block, lambda i: (i, 0, 0)),
        out_shape=jax.ShapeDtypeStruct(out_shape, BF16),
        compiler_params=_params("parallel"),
        name="compress_t" if transposed else "compress_n",
    )(x, pe, w1k, b1k, w2k, b2k)


def _heads(x):
    return [x[:, g * Q_BLK:(g + 1) * Q_BLK] for g in range(GQA)]


def _exp2_safe_max(m):
    return jnp.where(m == NEG_INF, 0.0, m)


def _normalise(pv):
    return pv[:HEAD_DIM] / jnp.maximum(pv[HEAD_DIM:HEAD_DIM + 1], TINY)


def _attn_kernel(qt_ref, kc_ref, vct_ref, ks_ref, vst_ref, kw_ref, vwt_ref, gt_ref, ovt_ref, cthr_ref,
                 o_ref, selb_ref, s_ref, *, seq):
    qi = pl.program_id(1)
    n_sel = seq // L_SEL
    lanes = GQA * Q_BLK
    win_keys = WINDOW + Q_BLK

    qt = jnp.concatenate([qt_ref[0, g] for g in range(GQA)], axis=1)
    q0 = qi * Q_BLK
    lane_q = lax.broadcasted_iota(jnp.int32, (1, Q_BLK), 1)
    t_q = q0 + lane_q
    t_all = jnp.concatenate([t_q] * GQA, axis=1)

    s_c = _dot(kc_ref[0], qt)

    def sel_scores(k0):
        return _dot(ks_ref[0, pl.ds(k0, SEL_SUB), :], qt)

    s_ref[...] = sel_scores(0)
    vis = cthr_ref[...] <= q0
    p_heads = []
    for s_g in _heads(s_c):
        s_g = jnp.where(vis, s_g, NEG_INF)
        m_g = _exp2_safe_max(jnp.max(s_g, axis=0, keepdims=True))
        p_heads.append(jnp.exp2(s_g - m_g).astype(BF16))
    p_c = jnp.concatenate(p_heads, axis=1)
    pv_c = _dot(vct_ref[0], p_c)
    r_c = 1.0 / jnp.maximum(pv_c[HEAD_DIM:HEAD_DIM + 1], TINY)
    o_c = pv_c[:HEAD_DIM] * r_c
    imp_all = _dot(ovt_ref[...], p_c) * r_c
    imp = functools.reduce(jnp.add, _heads(imp_all))

    blk = lax.broadcasted_iota(jnp.int32, (n_sel, Q_BLK), 0).astype(F32)
    cur = (t_q // L_SEL).astype(F32)
    forced = (blk == 0.0) | (blk == cur) | (blk == cur - 1.0)
    free = (blk <= cur) & jnp.logical_not(forced)
    cand = jnp.where(free, imp, NEG_INF)
    n_pick = min(TOP_K, n_sel) - N_FORCED

    def rank_exact(val):
        for _ in range(n_pick):
            mx = jnp.max(val, axis=0, keepdims=True)
            first = jnp.min(jnp.where(val == mx, blk, float(n_sel)), axis=0, keepdims=True)
            val = jnp.where(blk == first, NEG_INF, val)
        return val

    val_fast = cand
    for _ in range(n_pick):
        val_fast = jnp.where(val_fast == jnp.max(val_fast, axis=0, keepdims=True), NEG_INF, val_fast)
    n_free = jnp.sum(free.astype(F32), axis=0, keepdims=True)
    n_got = jnp.sum((free & (val_fast == NEG_INF)).astype(F32), axis=0, keepdims=True)
    tied = jnp.max(jnp.where(n_got != jnp.minimum(n_free, float(n_pick)), 1.0, 0.0)) > 0.0

    start = pl.multiple_of(jnp.maximum(q0 - WINDOW, 0), Q_BLK)
    s_w = _dot(kw_ref[0, pl.ds(start, win_keys), :], qt)
    r_minus_c = (lax.broadcasted_iota(jnp.int32, (Q_BLK, Q_BLK), 0)
                 - lax.broadcasted_iota(jnp.int32, (Q_BLK, Q_BLK), 1))
    chunks = []
    for i in range(win_keys // Q_BLK):
        hi = q0 - start - i * Q_BLK
        mask = (r_minus_c <= hi) & (r_minus_c > hi - WINDOW)
        chunks.append(jnp.concatenate(
            [jnp.where(mask, x, NEG_INF) for x in _heads(s_w[i * Q_BLK:(i + 1) * Q_BLK])], axis=1))
    m_w = _exp2_safe_max(functools.reduce(jnp.maximum, [jnp.max(x, axis=0, keepdims=True) for x in chunks]))
    p_w = jnp.concatenate([jnp.exp2(x - m_w).astype(BF16) for x in chunks], axis=0)
    o_w = _normalise(_dot(vwt_ref[0, :, pl.ds(start, win_keys)], p_w))

    val = lax.cond(tied, lambda: rank_exact(cand), lambda: val_fast)
    selb_ref[...] = jnp.where(forced | (free & (val == NEG_INF)), 0.0, NEG_INF)

    row_iota = lax.broadcasted_iota(jnp.int32, (L_SEL, lanes), 0)

    def sel_update(s_t, k0, m_prev, acc_prev, causal):
        blk0 = k0 // L_SEL
        blocks, biases, part_max = [], [], None
        for j in range(SEL_SUB // L_SEL):
            bias = jnp.concatenate([selb_ref[pl.ds(blk0 + j, 1), :]] * GQA, axis=1)
            s_b = s_t[j * L_SEL:(j + 1) * L_SEL]
            if causal:
                s_b = jnp.where(k0 + j * L_SEL + row_iota <= t_all, s_b, NEG_INF)
            s8 = functools.reduce(jnp.maximum, [s_b[r:r + SUBLANES] for r in range(0, L_SEL, SUBLANES)])
            cand = s8 + bias
            part_max = cand if part_max is None else jnp.maximum(part_max, cand)
            blocks.append(s_b)
            biases.append(bias)
        m_new = jnp.maximum(m_prev, jnp.max(part_max, axis=0, keepdims=True))
        m_safe = _exp2_safe_max(m_new)
        alpha = jnp.exp2(m_prev - m_safe)
        p = jnp.concatenate([jnp.exp2(s_b - (m_safe - bias)).astype(BF16)
                             for s_b, bias in zip(blocks, biases)], axis=0)
        pv = _dot(vst_ref[0, :, pl.ds(k0, SEL_SUB)], p)
        return m_new, alpha * acc_prev + pv

    def sel_run(k0, m_prev, acc_prev, n_sub, diagonal):
        s_cur = s_ref[...]
        for i in range(n_sub):
            last = diagonal and i == n_sub - 1
            if not last:
                s_next = sel_scores(pl.multiple_of(k0 + (i + 1) * SEL_SUB, SEL_SUB))
            m_prev, acc_prev = sel_update(s_cur, pl.multiple_of(k0 + i * SEL_SUB, SEL_SUB),
                                          m_prev, acc_prev, causal=last)
            if not last:
                s_cur = s_next
        if not diagonal:
            s_ref[...] = s_cur
        return m_prev, acc_prev

    subs = SEL_TILE // SEL_SUB
    init = (jnp.full((1, lanes), NEG_INF, F32), jnp.zeros((V_ROWS, lanes), F32))
    n_full = q0 // SEL_TILE
    m_s, acc_s = lax.fori_loop(
        0, n_full, lambda kt, c: sel_run(pl.multiple_of(kt * SEL_TILE, SEL_TILE), *c, subs, False), init)
    k_last = pl.multiple_of(n_full * SEL_TILE, SEL_TILE)
    n_below = (q0 - k_last) // SEL_SUB
    acc_s = lax.switch(
        n_below, [functools.partial(lambda n, m, a: sel_run(k_last, m, a, n + 1, True)[1], n) for n in range(subs)],
        m_s, acc_s)
    o_s = _normalise(acc_s)

    outs = []
    for g in range(GQA):
        sl = slice(g * Q_BLK, (g + 1) * Q_BLK)
        gate = jax.nn.sigmoid(gt_ref[0, g])
        outs.append(gate[0:1] * o_c[:, sl] + gate[1:2] * o_s[:, sl] + gate[2:3] * o_w[:, sl])
    pairs = [jnp.concatenate(outs[g:g + 2], axis=0).T for g in range(0, GQA, 2)]
    o_ref[0] = jnp.concatenate(pairs, axis=1)


def nsa_attention(qt, kc, vct, ks, vst, kw, vwt, gt, bsz):
    bh, _, _, seq = qt.shape
    assert seq % SEL_TILE == 0 and seq >= WINDOW + Q_BLK
    n_cmp = seq // STRIDE_CMP
    n_sel = seq // L_SEL
    cmp_start = jnp.arange(n_cmp) * STRIDE_CMP
    sel_start = jnp.arange(n_sel) * L_SEL
    ovt = ((cmp_start[None, :] < sel_start[:, None] + L_SEL)
           & (cmp_start[None, :] + L_CMP > sel_start[:, None])).astype(BF16)
    cthr = (cmp_start + L_CMP - 1)[:, None] - jnp.arange(Q_BLK)[None, :]
    per_bh = lambda shape: pl.BlockSpec((1,) + shape, lambda b, i: (b,) + (0,) * len(shape))
    return pl.pallas_call(
        functools.partial(_attn_kernel, seq=seq),
        grid=(bh, seq // Q_BLK),
        in_specs=[pl.BlockSpec((1, GQA, HEAD_DIM, Q_BLK), lambda b, i: (b, 0, 0, i)),
                  per_bh((n_cmp, HEAD_DIM)), per_bh((V_ROWS, n_cmp)),
                  per_bh((seq, HEAD_DIM)), per_bh((V_ROWS, seq)),
                  per_bh((seq, HEAD_DIM)), per_bh((V_ROWS, seq)),
                  pl.BlockSpec((1, GQA, 3, Q_BLK), lambda b, i: (b, 0, 0, i)),
                  pl.BlockSpec((n_sel, n_cmp), lambda b, i: (0, 0)),
                  pl.BlockSpec((n_cmp, Q_BLK), lambda b, i: (0, 0))],
        out_specs=pl.BlockSpec((1, Q_BLK, GQA * HEAD_DIM), lambda b, i: (b // N_KV, i, b % N_KV)),
        out_shape=jax.ShapeDtypeStruct((bsz, seq, N_KV * GQA * HEAD_DIM), F32),
        scratch_shapes=[pltpu.VMEM((n_sel, Q_BLK), F32), pltpu.VMEM((SEL_SUB, GQA * Q_BLK), F32)],
        compiler_params=_params("parallel", "arbitrary"),
        name="nsa_attention",
    )(qt, kc, vct, ks, vst, kw, vwt, gt, ovt, cthr.astype(jnp.int32))


def _ssm_kernel(u_ref, pt_ref, bq_ref, w_ref, v_ref, a1_ref, a2_ref, o_ref, mt_ref, *, chunks_per_seq):
    width = SSM_CHUNK * SSM_CH
    u = u_ref[0]
    kt = _dot_split(bq_ref[0], pt_ref[0])
    col = lax.broadcasted_iota(jnp.int32, (SSM_CH, width), 1)
    for s in range(SSM_CHUNK):
        shifted = kt if s == 0 else pltpu.roll(kt, SSM_CH * s, axis=1)
        mt_ref[s * SSM_CH:(s + 1) * SSM_CH, :] = jnp.where(col >= SSM_CH * s, shifted, 0.0).astype(BF16)
    y = _dot(u, mt_ref[...])
    x = _dot(u, w_ref[0])
    row = lax.broadcasted_iota(jnp.int32, x.shape, 0) % chunks_per_seq
    steps = int(math.log2(chunks_per_seq))
    for j in range(steps):
        d = 1 << j
        sh = jnp.where(row >= d, pltpu.roll(x, d, axis=0), 0.0)
        x = x + a1_ref[0, j:j + 1, :] * sh + a2_ref[0, j:j + 1, :] * pltpu.roll(sh, SSM_STATE, axis=1)
    prev = jnp.where(row >= 1, pltpu.roll(x, 1, axis=0), 0.0)
    o_ref[0] = y + _dot(prev.astype(BF16), v_ref[0])


def s5_scan(u2, pt, bq, w, v, a1, a2, chunks_per_seq):
    g, n_chunks, width = u2.shape
    per_g = lambda a: pl.BlockSpec((1,) + a.shape[1:], lambda i: (i,) + (0,) * (a.ndim - 1))
    return pl.pallas_call(
        functools.partial(_ssm_kernel, chunks_per_seq=chunks_per_seq),
        grid=(g,),
        in_specs=[per_g(u2), per_g(pt), per_g(bq), per_g(w), per_g(v), per_g(a1), per_g(a2)],
        out_specs=pl.BlockSpec((1, n_chunks, width), lambda i: (i, 0, 0)),
        out_shape=jax.ShapeDtypeStruct((g, n_chunks, width), F32),
        scratch_shapes=[pltpu.VMEM((width, width), BF16)],
        compiler_params=_params("parallel"),
        name="s5_scan",
    )(u2, pt, bq, w, v, a1, a2)


def _s5_tables(log_dt, a_re, a_im, b_re, b_im, c_re, c_im, chunks_per_seq):
    g, n = a_re.shape
    dt = jnp.exp(log_dt)[:, None]
    lam_re, lam_im = dt * a_re, dt * a_im

    def power(k):
        k = k.astype(F32)[None, :, None]
        mag = jnp.exp(k * lam_re[:, None, :])
        return mag * jnp.cos(k * lam_im[:, None, :]), mag * jnp.sin(k * lam_im[:, None, :])

    abar_re, abar_im = (x[:, 0] for x in power(jnp.ones((1,))))
    den = a_re * a_re + a_im * a_im
    nr = abar_re - 1.0
    f_re = (nr * a_re + abar_im * a_im) / den
    f_im = (abar_im * a_re - nr * a_im) / den
    bb_re = f_re[..., None] * b_re - f_im[..., None] * b_im
    bb_im = f_re[..., None] * b_im + f_im[..., None] * b_re
    steps = jnp.arange(SSM_CHUNK)
    width = SSM_CHUNK * SSM_CH
    e_re, e_im = power(steps)
    ca_re = c_re[:, None] * e_re[:, :, None, :] - c_im[:, None] * e_im[:, :, None, :]
    ca_im = c_re[:, None] * e_im[:, :, None, :] + c_im[:, None] * e_re[:, :, None, :]
    pt = jnp.concatenate([ca_re, -ca_im], axis=-1).reshape(g, width, 2 * n).transpose(0, 2, 1)
    bq = jnp.concatenate([bb_re, bb_im], axis=1).transpose(0, 2, 1)
    r_re, r_im = power(SSM_CHUNK - 1 - steps)
    bt_re, bt_im = bb_re.transpose(0, 2, 1)[:, None], bb_im.transpose(0, 2, 1)[:, None]
    w_re = r_re[:, :, None, :] * bt_re - r_im[:, :, None, :] * bt_im
    w_im = r_re[:, :, None, :] * bt_im + r_im[:, :, None, :] * bt_re
    w = jnp.concatenate([w_re, w_im], axis=-1).reshape(g, width, 2 * n).astype(BF16)
    n_re, n_im = power(steps + 1)
    cv_re = c_re[:, None] * n_re[:, :, None, :] - c_im[:, None] * n_im[:, :, None, :]
    cv_im = c_re[:, None] * n_im[:, :, None, :] + c_im[:, None] * n_re[:, :, None, :]
    v = jnp.concatenate([cv_re, -cv_im], axis=-1).reshape(g, width, 2 * n).transpose(0, 2, 1).astype(BF16)
    n_steps = int(math.log2(chunks_per_seq))
    s_re, s_im = power(SSM_CHUNK * (2 ** jnp.arange(max(n_steps, 1))))
    a1 = jnp.concatenate([s_re, s_re], axis=-1)
    a2 = jnp.concatenate([-s_im, s_im], axis=-1)
    return pt, bq, w, v, a1, a2


def _mix_out_kernel(x_ref, oa_ref, ys_ref, u_ref, dsk_ref, wglu_ref, bglu_ref, na_ref, ns_ref,
                    woa_ref, wos_ref, o_ref):
    d_ssm = u_ref.shape[1]
    y = jax.nn.gelu(ys_ref[...] + dsk_ref[...] * u_ref[...])
    z = _dot(y.astype(BF16), wglu_ref[...]) + bglu_ref[...]
    o_ssm = z[:, :d_ssm] * jax.nn.sigmoid(z[:, d_ssm:])
    att_n = _rms(oa_ref[...], na_ref[...]).astype(BF16)
    ssm_n = _rms(o_ssm, ns_ref[...]).astype(BF16)
    o_ref[...] = x_ref[...] + _dot(att_n, woa_ref[...]) + _dot(ssm_n, wos_ref[...])


def mix_out(x2, o_att, y_ssm, u, d_skip, w_glu, b_glu, n_att, n_ssm, w_out):
    t, d = x2.shape
    d_att = o_att.shape[1]
    d_ssm = u.shape[1]
    tm = _row_tile(t)
    rows = lambda w: pl.BlockSpec((tm, w), lambda i: (i, 0))
    full = lambda a: pl.BlockSpec(a.shape, lambda i: (0,) * a.ndim)
    consts = [d_skip.reshape(1, d_ssm), w_glu.astype(BF16), b_glu.reshape(1, 2 * d_ssm),
              n_att.reshape(1, d_att), n_ssm.reshape(1, d_ssm),
              w_out[:d_att].astype(BF16), w_out[d_att:].astype(BF16)]
    return pl.pallas_call(
        _mix_out_kernel,
        grid=(t // tm,),
        in_specs=[rows(d), rows(d_att), rows(d_ssm), rows(d_ssm)] + [full(c) for c in consts],
        out_specs=rows(d),
        out_shape=jax.ShapeDtypeStruct((t, d), F32),
        compiler_params=_params("parallel"),
        name="mix_out",
    )(x2, o_att, y_ssm, u, *consts)


def _ffn_kernel(x_ref, g_ref, wg_ref, wu_ref, wd_ref, fg_ref, o_ref, acc_ref, *, f_chunk, final_norm):
    x = x_ref[...]
    h = _rms(x, g_ref[...]).astype(BF16)
    d_ff = wg_ref.shape[1]
    acc_ref[...] = x
    for c in range(d_ff // f_chunk):
        sl = slice(c * f_chunk, (c + 1) * f_chunk)
        a = jax.nn.silu(_dot(h, wg_ref[:, sl])) * _dot(h, wu_ref[:, sl])
        acc_ref[...] += _dot(a.astype(BF16), wd_ref[sl, :])
    out = acc_ref[...]
    if final_norm:
        out = _rms(out, fg_ref[...])
    o_ref[...] = out


def ffn(x2, g, w_gate, w_up, w_down, final_g, final_norm):
    t, d = x2.shape
    d_ff = w_gate.shape[1]
    tm = _row_tile(t)
    f_chunk = 256 if d_ff % 256 == 0 else d_ff
    full = lambda a: pl.BlockSpec(a.shape, lambda i: (0,) * a.ndim)
    consts = [g.reshape(1, d), w_gate.astype(BF16), w_up.astype(BF16), w_down.astype(BF16),
              final_g.reshape(1, d)]
    return pl.pallas_call(
        functools.partial(_ffn_kernel, f_chunk=f_chunk, final_norm=final_norm),
        grid=(t // tm,),
        in_specs=[pl.BlockSpec((tm, d), lambda i: (i, 0))] + [full(c) for c in consts],
        out_specs=pl.BlockSpec((tm, d), lambda i: (i, 0)),
        out_shape=jax.ShapeDtypeStruct((t, d), F32),
        scratch_shapes=[pltpu.VMEM((tm, d), F32)],
        compiler_params=_params("parallel"),
        name="ffn",
    )(x2, *consts)


def _mixer(x2, bsz, seq, attn_norm_g, w_in, cmp_pe, cmp_w1, cmp_b1, cmp_w2, cmp_b2, log_dt, a_re, a_im,
           b_re, b_im, c_re, c_im, d_skip, w_glu, b_glu, mix_norm_att, mix_norm_ssm, w_out):
    t, d = x2.shape
    d_ssm = d - D_ATT
    bh = bsz * N_KV
    qt, vst, vwt, gt, kc, vc, ks, kw, u = in_proj(x2.reshape(bsz, seq, d), attn_norm_g, w_in)
    k_c = compress(kc, cmp_pe[0], cmp_w1[0], cmp_b1[0], cmp_w2[0], cmp_b2[0], transposed=False)
    v_ct = compress(vc, cmp_pe[1], cmp_w1[1], cmp_b1[1], cmp_w2[1], cmp_b2[1], transposed=True)
    o_att = nsa_attention(qt.reshape(bh, GQA, HEAD_DIM, seq), k_c, v_ct,
                          ks.reshape(bh, seq, HEAD_DIM), vst.reshape(bh, V_ROWS, seq),
                          kw.reshape(bh, seq, HEAD_DIM), vwt.reshape(bh, V_ROWS, seq),
                          gt.reshape(bh, GQA, 3, seq), bsz).reshape(t, D_ATT)

    u = u.reshape(t, d_ssm)
    n_groups = d_ssm // SSM_CH
    chunks_per_seq = seq // SSM_CHUNK
    n_chunks = t // SSM_CHUNK
    width = SSM_CHUNK * SSM_CH
    u2 = u.astype(BF16).reshape(n_chunks, SSM_CHUNK, n_groups, SSM_CH).transpose(2, 0, 1, 3)
    tables = _s5_tables(log_dt, a_re, a_im, b_re, b_im, c_re, c_im, chunks_per_seq)
    y2 = s5_scan(u2.reshape(n_groups, n_chunks, width), *tables, chunks_per_seq)
    y_ssm = y2.reshape(n_groups, n_chunks, SSM_CHUNK, SSM_CH).transpose(1, 2, 0, 3).reshape(t, d_ssm)

    return mix_out(x2, o_att, y_ssm, u, d_skip, w_glu, b_glu, mix_norm_att, mix_norm_ssm, w_out)


def kernel(x, attn_norm_g, w_in, cmp_pe, cmp_w1, cmp_b1, cmp_w2, cmp_b2, log_dt, a_re, a_im, b_re, b_im, c_re, c_im, d_skip, w_glu, b_glu, mix_norm_att, mix_norm_ssm, w_out, ffn_norm_g, w_gate, w_up, w_down, final_norm_g):
    bsz, seq, d = x.shape
    depth = w_in.shape[0]
    x2 = x.reshape(bsz * seq, d)
    for l in range(depth):
        x2 = _mixer(x2, bsz, seq, attn_norm_g[l], w_in[l], cmp_pe[l], cmp_w1[l], cmp_b1[l], cmp_w2[l],
                    cmp_b2[l], log_dt[l], a_re[l], a_im[l], b_re[l], b_im[l], c_re[l], c_im[l], d_skip[l],
                    w_glu[l], b_glu[l], mix_norm_att[l], mix_norm_ssm[l], w_out[l])
        x2 = ffn(x2, ffn_norm_g[l], w_gate[l], w_up[l], w_down[l], final_norm_g, final_norm=(l == depth - 1))
    return x2.reshape(bsz, seq, d)
```

```python
import functools
import math

import jax
import jax.numpy as jnp
from jax import lax
from jax.experimental import pallas as pl
from jax.experimental.pallas import tpu as pltpu

HEAD_DIM = 64
N_KV = 3
GQA = 4
N_HEADS = N_KV * GQA
D_ATT = N_HEADS * HEAD_DIM
SSM_CH = 16
SSM_STATE = 64
L_CMP = 32
STRIDE_CMP = 16
CMP_HID = 256
L_SEL = 64
TOP_K = 16
N_FORCED = 3
WINDOW = 512
Q_BLK = 128
BIG = 1e9
TINY = 1e-30
EPS = 1e-6
LOG2E = 1.4426950408889634

SUBLANES = 8
V_ROWS = HEAD_DIM + 2 * SUBLANES
SSM_CHUNK = 64
SEL_SUB = 512
SEL_LOOP = 8
SEL_DIAG = 4
FRONT_PARTS = 4
SEL_AHEAD = 2
VMEM_LIMIT = 56 * 1024 * 1024

F32 = jnp.float32
BF16 = jnp.bfloat16
NEG_INF = float("-inf")


def _dot(a, b):
    return jnp.dot(a, b, preferred_element_type=F32)


def _dot_nt(a, b):
    return lax.dot_general(a, b, (((1,), (1,)), ((), ())), preferred_element_type=F32)


def _dot_split(a, b):
    a_hi = a.astype(BF16)
    a_lo = (a - a_hi.astype(F32)).astype(BF16)
    b_hi = b.astype(BF16)
    b_lo = (b - b_hi.astype(F32)).astype(BF16)
    return _dot(a_hi, b_hi) + _dot(a_hi, b_lo) + _dot(a_lo, b_hi)


def _params(*sem):
    return pltpu.CompilerParams(dimension_semantics=sem, vmem_limit_bytes=VMEM_LIMIT)


def _row_tile(t):
    for tm in (512, 256, 128, 64, 32, 16, 8):
        if t % tm == 0:
            return tm
    raise ValueError(f"token count {t} must be a multiple of 8")


def _rms(x, g):
    ms = jnp.mean(x * x, axis=-1, keepdims=True)
    return x * lax.rsqrt(ms + EPS) * g


def _in_proj_kernel(x_ref, g_ref, wn_ref, wt_ref, qt_ref, vst_ref, vwt_ref, gt_ref, kc_ref, vc_ref,
                    ks_ref, kw_ref, u_ref):
    p_kv = N_KV * HEAD_DIM
    h = _rms(x_ref[0], g_ref[...]).astype(BF16)
    tm = h.shape[0]
    nat = _dot(h, wn_ref[...])
    tr = _dot_nt(wt_ref[...], h)
    qt_ref[0] = (tr[:D_ATT] * (LOG2E * HEAD_DIM ** -0.5)).astype(BF16)
    ones_rows = (lax.broadcasted_iota(jnp.int32, (V_ROWS - HEAD_DIM, tm), 0) == 0).astype(BF16)
    for hd in range(N_KV):
        rows = slice(hd * HEAD_DIM, (hd + 1) * HEAD_DIM)
        for k, ref in enumerate((vst_ref, vwt_ref)):
            off = D_ATT + k * p_kv
            ref[0, hd, :HEAD_DIM, :] = tr[off + hd * HEAD_DIM:off + (hd + 1) * HEAD_DIM].astype(BF16)
            ref[0, hd, HEAD_DIM:, :] = ones_rows
        kc_ref[0, hd] = nat[:, rows]
        vc_ref[0, hd] = nat[:, p_kv + hd * HEAD_DIM:p_kv + (hd + 1) * HEAD_DIM]
        ks_ref[0, hd] = nat[:, 2 * p_kv + hd * HEAD_DIM:2 * p_kv + (hd + 1) * HEAD_DIM].astype(BF16)
        kw_ref[0, hd] = nat[:, 3 * p_kv + hd * HEAD_DIM:3 * p_kv + (hd + 1) * HEAD_DIM].astype(BF16)
    gt_ref[0] = tr[D_ATT + 2 * p_kv:D_ATT + 2 * p_kv + 3 * N_HEADS]
    u_ref[0] = nat[:, 4 * p_kv:]


def in_proj(x3, g, w_in):
    bsz, seq, d = x3.shape
    p_kv = N_KV * HEAD_DIM
    p_gate = 3 * N_HEADS
    d_ssm = d - D_ATT
    o = [0, D_ATT]
    for width in [p_kv] * 6 + [p_gate, d_ssm]:
        o.append(o[-1] + width)
    q, kc, vc, ksl, vsl, kwn, vwn, gl, u = (w_in[:, o[i]:o[i + 1]] for i in range(9))
    w_nat = jnp.concatenate([kc, vc, ksl, kwn, u], axis=1).astype(BF16)
    n_tr = D_ATT + 2 * p_kv + p_gate
    w_tr = jnp.pad(jnp.concatenate([q, vsl, vwn, gl], axis=1), ((0, 0), (0, -n_tr % SUBLANES))).T.astype(BF16)
    tm = _row_tile(seq)
    full = lambda a: pl.BlockSpec(a.shape, lambda b, i: (0,) * a.ndim)
    feat = lambda rows: pl.BlockSpec((1, rows, tm), lambda b, i: (b, 0, i))
    vals = pl.BlockSpec((1, N_KV, V_ROWS, tm), lambda b, i: (b, 0, 0, i))
    keys = pl.BlockSpec((1, N_KV, tm, HEAD_DIM), lambda b, i: (b, 0, i, 0))
    sds = jax.ShapeDtypeStruct
    g2 = g.reshape(1, d)
    return pl.pallas_call(
        _in_proj_kernel,
        grid=(bsz, seq // tm),
        in_specs=[pl.BlockSpec((1, tm, d), lambda b, i: (b, i, 0)), full(g2), full(w_nat), full(w_tr)],
        out_specs=[feat(D_ATT), vals, vals, feat(p_gate), keys, keys, keys, keys,
                   pl.BlockSpec((1, tm, d_ssm), lambda b, i: (b, i, 0))],
        out_shape=[sds((bsz, D_ATT, seq), BF16), sds((bsz, N_KV, V_ROWS, seq), BF16),
                   sds((bsz, N_KV, V_ROWS, seq), BF16), sds((bsz, p_gate, seq), F32),
                   sds((bsz, N_KV, seq, HEAD_DIM), F32), sds((bsz, N_KV, seq, HEAD_DIM), F32),
                   sds((bsz, N_KV, seq, HEAD_DIM), BF16), sds((bsz, N_KV, seq, HEAD_DIM), BF16),
                   sds((bsz, seq, d_ssm), F32)],
        compiler_params=_params("parallel", "parallel"),
        name="in_proj",
    )(x3, g2, w_nat, w_tr)


def _compress_kernel(x_ref, pe_ref, w1_ref, b1_ref, w2_ref, b2_ref, o_ref, *, transposed):
    seq = x_ref.shape[2]
    n = seq // STRIDE_CMP
    first = jnp.zeros((n, CMP_HID), F32)
    second = jnp.zeros((n, CMP_HID), F32)
    for j in range(STRIDE_CMP):
        tok = x_ref[0, 0, pl.ds(j, n, stride=STRIDE_CMP), :]
        w_a = w1_ref[j * HEAD_DIM:(j + 1) * HEAD_DIM, :]
        w_b = w1_ref[(STRIDE_CMP + j) * HEAD_DIM:(STRIDE_CMP + j + 1) * HEAD_DIM, :]
        first = first + _dot((tok + pe_ref[j:j + 1, :]).astype(BF16), w_a)
        second = second + _dot((tok + pe_ref[STRIDE_CMP + j:STRIDE_CMP + j + 1, :]).astype(BF16), w_b)
    hid = first + pltpu.roll(second, n - 1, axis=0) + b1_ref[...]
    hid = jax.nn.gelu(hid).astype(BF16)
    if transposed:
        o_ref[0] = (_dot_nt(w2_ref[...], hid) + b2_ref[...]).astype(o_ref.dtype)
    else:
        o_ref[0] = (_dot(hid, w2_ref[...]) + b2_ref[...]).astype(o_ref.dtype)


def compress(x, pe, w1, b1, w2, b2, transposed):
    bsz, n_kv, seq, _ = x.shape
    bh = bsz * n_kv
    n = seq // STRIDE_CMP
    w1k = w1.astype(BF16)
    if transposed:
        pad = V_ROWS - HEAD_DIM
        w2k = jnp.pad(w2.T, ((0, pad), (0, 0))).astype(BF16)
        b2k = jnp.concatenate([b2, jnp.ones((1,), F32), jnp.zeros((pad - 1,), F32)]).reshape(V_ROWS, 1)
        out_block, out_shape = (1, V_ROWS, n), (bh, V_ROWS, n)
    else:
        w2k = w2.astype(BF16)
        b2k = b2.reshape(1, HEAD_DIM)
        out_block, out_shape = (1, n, HEAD_DIM), (bh, n, HEAD_DIM)
    full = lambda a: pl.BlockSpec(a.shape, lambda i: (0,) * a.ndim)
    b1k = b1.reshape(1, CMP_HID)
    return pl.pallas_call(
        functools.partial(_compress_kernel, transposed=transposed),
        grid=(bh,),
        in_specs=[pl.BlockSpec((1, 1, seq, HEAD_DIM), lambda i: (i // n_kv, i % n_kv, 0, 0)),
                  full(pe), full(w1k), full(b1k), full(w2k), full(b2k)],
        out_specs=pl.BlockSpec(out_block, lambda i: (i, 0, 0)),
        out_shape=jax.ShapeDtypeStruct(out_shape, BF16),
        compiler_params=_params("parallel"),
        name="compress_t" if transposed else "compress_n",
    )(x, pe, w1k, b1k, w2k, b2k)


def _heads(x):
    return [x[:, g * Q_BLK:(g + 1) * Q_BLK] for g in range(GQA)]


def _exp2_safe_max(m):
    return jnp.where(m == NEG_INF, 0.0, m)


def _normalise(pv):
    return pv[:HEAD_DIM] / jnp.maximum(pv[HEAD_DIM:HEAD_DIM + 1], TINY)


def _attn_kernel(qt_ref, kc_ref, vct_ref, ks_ref, vst_ref, kw_ref, vwt_ref, gt_ref, ovt_ref, cthr_ref,
                 o_ref, selb_ref, s_ref, *, seq):
    qi = pl.program_id(1)
    n_sel = seq // L_SEL
    lanes = GQA * Q_BLK
    win_keys = WINDOW + Q_BLK

    qt = jnp.concatenate([qt_ref[0, g] for g in range(GQA)], axis=1)
    q0 = qi * Q_BLK
    lane_q = lax.broadcasted_iota(jnp.int32, (1, Q_BLK), 1)
    t_q = q0 + lane_q
    t_all = jnp.concatenate([t_q] * GQA, axis=1)

    def sel_scores(k0):
        return _dot(ks_ref[0, pl.ds(k0, SEL_SUB), :], qt)

    def front(n_rows):
        n_blk = n_rows * STRIDE_CMP // L_SEL
        s_c = _dot(kc_ref[0, :n_rows, :], qt)
        for a in range(SEL_AHEAD):
            s_ref[a] = sel_scores(a * SEL_SUB)
        vis = cthr_ref[:n_rows, :] <= q0
        p_heads = []
        for s_g in _heads(s_c):
            s_g = jnp.where(vis, s_g, NEG_INF)
            m_g = _exp2_safe_max(jnp.max(s_g, axis=0, keepdims=True))
            p_heads.append(jnp.exp2(s_g - m_g).astype(BF16))
        p_c = jnp.concatenate(p_heads, axis=1)
        pv_c = _dot(vct_ref[0, :, :n_rows], p_c)
        r_c = 1.0 / jnp.maximum(pv_c[HEAD_DIM:HEAD_DIM + 1], TINY)
        o_c = pv_c[:HEAD_DIM] * r_c
        imp_all = _dot(ovt_ref[:n_blk, :n_rows], p_c) * r_c
        imp = functools.reduce(jnp.add, _heads(imp_all))

        blk = lax.broadcasted_iota(jnp.int32, (n_blk, Q_BLK), 0).astype(F32)
        cur = (t_q // L_SEL).astype(F32)
        forced = (blk == 0.0) | (blk == cur) | (blk == cur - 1.0)
        free = (blk <= cur) & jnp.logical_not(forced)
        cand = jnp.where(free, imp, NEG_INF)
        n_pick = min(TOP_K, n_sel) - N_FORCED

        def rank_exact(val):
            for _ in range(n_pick):
                mx = jnp.max(val, axis=0, keepdims=True)
                first = jnp.min(jnp.where(val == mx, blk, float(n_blk)), axis=0, keepdims=True)
                val = jnp.where(blk == first, NEG_INF, val)
            return val

        val_fast = cand
        for _ in range(n_pick):
            val_fast = jnp.where(val_fast == jnp.max(val_fast, axis=0, keepdims=True), NEG_INF, val_fast)
        n_free = jnp.sum(free.astype(F32), axis=0, keepdims=True)
        n_got = jnp.sum((free & (val_fast == NEG_INF)).astype(F32), axis=0, keepdims=True)
        tied = jnp.max(jnp.where(n_got != jnp.minimum(n_free, float(n_pick)), 1.0, 0.0)) > 0.0

        start = pl.multiple_of(jnp.maximum(q0 - WINDOW, 0), Q_BLK)
        s_w = _dot(kw_ref[0, pl.ds(start, win_keys), :], qt)
        r_minus_c = (lax.broadcasted_iota(jnp.int32, (Q_BLK, Q_BLK), 0)
                     - lax.broadcasted_iota(jnp.int32, (Q_BLK, Q_BLK), 1))
        chunks = []
        for i in range(win_keys // Q_BLK):
            hi = q0 - start - i * Q_BLK
            mask = (r_minus_c <= hi) & (r_minus_c > hi - WINDOW)
            chunks.append(jnp.concatenate(
                [jnp.where(mask, x, NEG_INF) for x in _heads(s_w[i * Q_BLK:(i + 1) * Q_BLK])], axis=1))
        m_w = _exp2_safe_max(functools.reduce(jnp.maximum, [jnp.max(x, axis=0, keepdims=True) for x in chunks]))
        p_w = jnp.concatenate([jnp.exp2(x - m_w).astype(BF16) for x in chunks], axis=0)
        o_w = _normalise(_dot(vwt_ref[0, :, pl.ds(start, win_keys)], p_w))

        val = lax.cond(tied, lambda: rank_exact(cand), lambda: val_fast)
        selb_ref[:n_blk, :] = jnp.where(forced | (free & (val == NEG_INF)), 0.0, NEG_INF)
        return o_c, o_w

    n_cmp = seq // STRIDE_CMP
    parts = FRONT_PARTS if n_cmp % (FRONT_PARTS * Q_BLK) == 0 else 1
    o_c, o_w = lax.switch(q0 // (seq // parts),
                          [functools.partial(front, n_cmp * (k + 1) // parts) for k in range(parts)])

    row_iota = lax.broadcasted_iota(jnp.int32, (L_SEL, lanes), 0)

    def sel_update(s_t, k0, m_prev, acc_prev, causal):
        blk0 = k0 // L_SEL
        blocks, biases, part_max = [], [], None
        for j in range(SEL_SUB // L_SEL):
            bias = jnp.concatenate([selb_ref[pl.ds(blk0 + j, 1), :]] * GQA, axis=1)
            s_b = s_t[j * L_SEL:(j + 1) * L_SEL]
            if causal:
                s_b = jnp.where(k0 + j * L_SEL + row_iota <= t_all, s_b, NEG_INF)
            s8 = functools.reduce(jnp.maximum, [s_b[r:r + SUBLANES] for r in range(0, L_SEL, SUBLANES)])
            cand = s8 + bias
            part_max = cand if part_max is None else jnp.maximum(part_max, cand)
            blocks.append(s_b)
            biases.append(bias)
        m_new = jnp.maximum(m_prev, jnp.max(part_max, axis=0, keepdims=True))
        m_safe = _exp2_safe_max(m_new)
        alpha = jnp.exp2(m_prev - m_safe)
        p = jnp.concatenate([jnp.exp2(s_b - (m_safe - bias)).astype(BF16)
                             for s_b, bias in zip(blocks, biases)], axis=0)
        pv = _dot(vst_ref[0, :, pl.ds(k0, SEL_SUB)], p)
        return m_new, alpha * acc_prev + pv

    def sel_run(k0, m_prev, acc_prev, n_sub, diagonal):
        queue = [s_ref[a] for a in range(min(SEL_AHEAD, n_sub) if diagonal else SEL_AHEAD)]
        for i in range(n_sub):
            ahead = i + SEL_AHEAD
            if not diagonal or ahead < n_sub:
                queue.append(sel_scores(pl.multiple_of(k0 + ahead * SEL_SUB, SEL_SUB)))
            m_prev, acc_prev = sel_update(queue.pop(0), pl.multiple_of(k0 + i * SEL_SUB, SEL_SUB),
                                          m_prev, acc_prev, causal=diagonal and i == n_sub - 1)
        if not diagonal:
            for a in range(SEL_AHEAD):
                s_ref[a] = queue[a]
        return m_prev, acc_prev

    carry = (jnp.full((1, lanes), NEG_INF, F32), jnp.zeros((V_ROWS, lanes), F32))
    n_below = q0 // SEL_SUB
    loop_keys = SEL_LOOP * SEL_SUB
    diag_keys = SEL_DIAG * SEL_SUB
    carry = lax.fori_loop(
        0, n_below // SEL_LOOP,
        lambda kt, c: sel_run(pl.multiple_of(kt * loop_keys, loop_keys), *c, SEL_LOOP, False), carry)
    k_mid = pl.multiple_of((n_below // SEL_LOOP) * loop_keys, loop_keys)
    extra = (n_below % SEL_LOOP) // SEL_DIAG
    carry = lax.cond(extra == 1, lambda m, a: sel_run(k_mid, m, a, SEL_DIAG, False), lambda m, a: (m, a), *carry)
    k_last = pl.multiple_of(k_mid + extra * diag_keys, diag_keys)
    acc_s = lax.switch(
        n_below % SEL_DIAG,
        [functools.partial(lambda n, m, a: sel_run(k_last, m, a, n + 1, True)[1], n) for n in range(SEL_DIAG)],
        *carry)
    o_s = _normalise(acc_s)

    outs = []
    for g in range(GQA):
        sl = slice(g * Q_BLK, (g + 1) * Q_BLK)
        gate = jax.nn.sigmoid(gt_ref[0, g])
        outs.append(gate[0:1] * o_c[:, sl] + gate[1:2] * o_s[:, sl] + gate[2:3] * o_w[:, sl])
    pairs = [jnp.concatenate(outs[g:g + 2], axis=0).T for g in range(0, GQA, 2)]
    o_ref[0] = jnp.concatenate(pairs, axis=1)


def nsa_attention(qt, kc, vct, ks, vst, kw, vwt, gt, bsz):
    bh, _, _, seq = qt.shape
    assert SEL_LOOP == 2 * SEL_DIAG and seq % (SEL_DIAG * SEL_SUB) == 0 and seq >= WINDOW + Q_BLK
    n_cmp = seq // STRIDE_CMP
    n_sel = seq // L_SEL
    cmp_start = jnp.arange(n_cmp) * STRIDE_CMP
    sel_start = jnp.arange(n_sel) * L_SEL
    ovt = ((cmp_start[None, :] < sel_start[:, None] + L_SEL)
           & (cmp_start[None, :] + L_CMP > sel_start[:, None])).astype(BF16)
    cthr = (cmp_start + L_CMP - 1)[:, None] - jnp.arange(Q_BLK)[None, :]
    per_bh = lambda shape: pl.BlockSpec((1,) + shape, lambda b, i: (b,) + (0,) * len(shape))
    return pl.pallas_call(
        functools.partial(_attn_kernel, seq=seq),
        grid=(bh, seq // Q_BLK),
        in_specs=[pl.BlockSpec((1, GQA, HEAD_DIM, Q_BLK), lambda b, i: (b, 0, 0, i)),
                  per_bh((n_cmp, HEAD_DIM)), per_bh((V_ROWS, n_cmp)),
                  per_bh((seq, HEAD_DIM)), per_bh((V_ROWS, seq)),
                  per_bh((seq, HEAD_DIM)), per_bh((V_ROWS, seq)),
                  pl.BlockSpec((1, GQA, 3, Q_BLK), lambda b, i: (b, 0, 0, i)),
                  pl.BlockSpec((n_sel, n_cmp), lambda b, i: (0, 0)),
                  pl.BlockSpec((n_cmp, Q_BLK), lambda b, i: (0, 0))],
        out_specs=pl.BlockSpec((1, Q_BLK, GQA * HEAD_DIM), lambda b, i: (b // N_KV, i, b % N_KV)),
        out_shape=jax.ShapeDtypeStruct((bsz, seq, N_KV * GQA * HEAD_DIM), F32),
        scratch_shapes=[pltpu.VMEM((n_sel, Q_BLK), F32), pltpu.VMEM((SEL_AHEAD, SEL_SUB, GQA * Q_BLK), F32)],
        compiler_params=_params("parallel", "arbitrary"),
        name="nsa_attention",
    )(qt, kc, vct, ks, vst, kw, vwt, gt, ovt, cthr.astype(jnp.int32))


def _ssm_kernel(u_ref, pt_ref, bq_ref, w_ref, v_ref, a1_ref, a2_ref, o_ref, mt_ref, *, chunks_per_seq):
    width = SSM_CHUNK * SSM_CH
    u = u_ref[0]
    kt = _dot_split(bq_ref[0], pt_ref[0])
    col = lax.broadcasted_iota(jnp.int32, (SSM_CH, width), 1)
    for s in range(SSM_CHUNK):
        shifted = kt if s == 0 else pltpu.roll(kt, SSM_CH * s, axis=1)
        mt_ref[s * SSM_CH:(s + 1) * SSM_CH, :] = jnp.where(col >= SSM_CH * s, shifted, 0.0).astype(BF16)
    y = _dot(u, mt_ref[...])
    x = _dot(u, w_ref[0])
    row = lax.broadcasted_iota(jnp.int32, x.shape, 0) % chunks_per_seq
    steps = int(math.log2(chunks_per_seq))
    for j in range(steps):
        d = 1 << j
        sh = jnp.where(row >= d, pltpu.roll(x, d, axis=0), 0.0)
        x = x + a1_ref[0, j:j + 1, :] * sh + a2_ref[0, j:j + 1, :] * pltpu.roll(sh, SSM_STATE, axis=1)
    prev = jnp.where(row >= 1, pltpu.roll(x, 1, axis=0), 0.0)
    o_ref[0] = y + _dot(prev.astype(BF16), v_ref[0])


def s5_scan(u2, pt, bq, w, v, a1, a2, chunks_per_seq):
    g, n_chunks, width = u2.shape
    per_g = lambda a: pl.BlockSpec((1,) + a.shape[1:], lambda i: (i,) + (0,) * (a.ndim - 1))
    return pl.pallas_call(
        functools.partial(_ssm_kernel, chunks_per_seq=chunks_per_seq),
        grid=(g,),
        in_specs=[per_g(u2), per_g(pt), per_g(bq), per_g(w), per_g(v), per_g(a1), per_g(a2)],
        out_specs=pl.BlockSpec((1, n_chunks, width), lambda i: (i, 0, 0)),
        out_shape=jax.ShapeDtypeStruct((g, n_chunks, width), F32),
        scratch_shapes=[pltpu.VMEM((width, width), BF16)],
        compiler_params=_params("parallel"),
        name="s5_scan",
    )(u2, pt, bq, w, v, a1, a2)


def _s5_tables(log_dt, a_re, a_im, b_re, b_im, c_re, c_im, chunks_per_seq):
    g, n = a_re.shape
    dt = jnp.exp(log_dt)[:, None]
    lam_re, lam_im = dt * a_re, dt * a_im

    def power(k):
        k = k.astype(F32)[None, :, None]
        mag = jnp.exp(k * lam_re[:, None, :])
        return mag * jnp.cos(k * lam_im[:, None, :]), mag * jnp.sin(k * lam_im[:, None, :])

    abar_re, abar_im = (x[:, 0] for x in power(jnp.ones((1,))))
    den = a_re * a_re + a_im * a_im
    nr = abar_re - 1.0
    f_re = (nr * a_re + abar_im * a_im) / den
    f_im = (abar_im * a_re - nr * a_im) / den
    bb_re = f_re[..., None] * b_re - f_im[..., None] * b_im
    bb_im = f_re[..., None] * b_im + f_im[..., None] * b_re
    steps = jnp.arange(SSM_CHUNK)
    width = SSM_CHUNK * SSM_CH
    e_re, e_im = power(steps)
    ca_re = c_re[:, None] * e_re[:, :, None, :] - c_im[:, None] * e_im[:, :, None, :]
    ca_im = c_re[:, None] * e_im[:, :, None, :] + c_im[:, None] * e_re[:, :, None, :]
    pt = jnp.concatenate([ca_re, -ca_im], axis=-1).reshape(g, width, 2 * n).transpose(0, 2, 1)
    bq = jnp.concatenate([bb_re, bb_im], axis=1).transpose(0, 2, 1)
    r_re, r_im = power(SSM_CHUNK - 1 - steps)
    bt_re, bt_im = bb_re.transpose(0, 2, 1)[:, None], bb_im.transpose(0, 2, 1)[:, None]
    w_re = r_re[:, :, None, :] * bt_re - r_im[:, :, None, :] * bt_im
    w_im = r_re[:, :, None, :] * bt_im + r_im[:, :, None, :] * bt_re
    w = jnp.concatenate([w_re, w_im], axis=-1).reshape(g, width, 2 * n).astype(BF16)
    n_re, n_im = power(steps + 1)
    cv_re = c_re[:, None] * n_re[:, :, None, :] - c_im[:, None] * n_im[:, :, None, :]
    cv_im = c_re[:, None] * n_im[:, :, None, :] + c_im[:, None] * n_re[:, :, None, :]
    v = jnp.concatenate([cv_re, -cv_im], axis=-1).reshape(g, width, 2 * n).transpose(0, 2, 1).astype(BF16)
    n_steps = int(math.log2(chunks_per_seq))
    s_re, s_im = power(SSM_CHUNK * (2 ** jnp.arange(max(n_steps, 1))))
    a1 = jnp.concatenate([s_re, s_re], axis=-1)
    a2 = jnp.concatenate([-s_im, s_im], axis=-1)
    return pt, bq, w, v, a1, a2


def _mix_out_kernel(x_ref, oa_ref, ys_ref, u_ref, dsk_ref, wglu_ref, bglu_ref, na_ref, ns_ref,
                    woa_ref, wos_ref, o_ref):
    d_ssm = u_ref.shape[1]
    y = jax.nn.gelu(ys_ref[...] + dsk_ref[...] * u_ref[...])
    z = _dot(y.astype(BF16), wglu_ref[...]) + bglu_ref[...]
    o_ssm = z[:, :d_ssm] * jax.nn.sigmoid(z[:, d_ssm:])
    att_n = _rms(oa_ref[...], na_ref[...]).astype(BF16)
    ssm_n = _rms(o_ssm, ns_ref[...]).astype(BF16)
    o_ref[...] = x_ref[...] + _dot(att_n, woa_ref[...]) + _dot(ssm_n, wos_ref[...])


def mix_out(x2, o_att, y_ssm, u, d_skip, w_glu, b_glu, n_att, n_ssm, w_out):
    t, d = x2.shape
    d_att = o_att.shape[1]
    d_ssm = u.shape[1]
    tm = _row_tile(t)
    rows = lambda w: pl.BlockSpec((tm, w), lambda i: (i, 0))
    full = lambda a: pl.BlockSpec(a.shape, lambda i: (0,) * a.ndim)
    consts = [d_skip.reshape(1, d_ssm), w_glu.astype(BF16), b_glu.reshape(1, 2 * d_ssm),
              n_att.reshape(1, d_att), n_ssm.reshape(1, d_ssm),
              w_out[:d_att].astype(BF16), w_out[d_att:].astype(BF16)]
    return pl.pallas_call(
        _mix_out_kernel,
        grid=(t // tm,),
        in_specs=[rows(d), rows(d_att), rows(d_ssm), rows(d_ssm)] + [full(c) for c in consts],
        out_specs=rows(d),
        out_shape=jax.ShapeDtypeStruct((t, d), F32),
        compiler_params=_params("parallel"),
        name="mix_out",
    )(x2, o_att, y_ssm, u, *consts)


def _ffn_kernel(x_ref, g_ref, wg_ref, wu_ref, wd_ref, fg_ref, o_ref, acc_ref, *, f_chunk, final_norm):
    x = x_ref[...]
    h = _rms(x, g_ref[...]).astype(BF16)
    d_ff = wg_ref.shape[1]
    acc_ref[...] = x
    for c in range(d_ff // f_chunk):
        sl = slice(c * f_chunk, (c + 1) * f_chunk)
        a = jax.nn.silu(_dot(h, wg_ref[:, sl])) * _dot(h, wu_ref[:, sl])
        acc_ref[...] += _dot(a.astype(BF16), wd_ref[sl, :])
    out = acc_ref[...]
    if final_norm:
        out = _rms(out, fg_ref[...])
    o_ref[...] = out


def ffn(x2, g, w_gate, w_up, w_down, final_g, final_norm):
    t, d = x2.shape
    d_ff = w_gate.shape[1]
    tm = _row_tile(t)
    f_chunk = 256 if d_ff % 256 == 0 else d_ff
    full = lambda a: pl.BlockSpec(a.shape, lambda i: (0,) * a.ndim)
    consts = [g.reshape(1, d), w_gate.astype(BF16), w_up.astype(BF16), w_down.astype(BF16),
              final_g.reshape(1, d)]
    return pl.pallas_call(
        functools.partial(_ffn_kernel, f_chunk=f_chunk, final_norm=final_norm),
        grid=(t // tm,),
        in_specs=[pl.BlockSpec((tm, d), lambda i: (i, 0))] + [full(c) for c in consts],
        out_specs=pl.BlockSpec((tm, d), lambda i: (i, 0)),
        out_shape=jax.ShapeDtypeStruct((t, d), F32),
        scratch_shapes=[pltpu.VMEM((tm, d), F32)],
        compiler_params=_params("parallel"),
        name="ffn",
    )(x2, *consts)


def _mixer(x2, bsz, seq, attn_norm_g, w_in, cmp_pe, cmp_w1, cmp_b1, cmp_w2, cmp_b2, log_dt, a_re, a_im,
           b_re, b_im, c_re, c_im, d_skip, w_glu, b_glu, mix_norm_att, mix_norm_ssm, w_out):
    t, d = x2.shape
    d_ssm = d - D_ATT
    bh = bsz * N_KV
    qt, vst, vwt, gt, kc, vc, ks, kw, u = in_proj(x2.reshape(bsz, seq, d), attn_norm_g, w_in)
    k_c = compress(kc, cmp_pe[0], cmp_w1[0], cmp_b1[0], cmp_w2[0], cmp_b2[0], transposed=False)
    v_ct = compress(vc, cmp_pe[1], cmp_w1[1], cmp_b1[1], cmp_w2[1], cmp_b2[1], transposed=True)
    o_att = nsa_attention(qt.reshape(bh, GQA, HEAD_DIM, seq), k_c, v_ct,
                          ks.reshape(bh, seq, HEAD_DIM), vst.reshape(bh, V_ROWS, seq),
                          kw.reshape(bh, seq, HEAD_DIM), vwt.reshape(bh, V_ROWS, seq),
                          gt.reshape(bh, GQA, 3, seq), bsz).reshape(t, D_ATT)

    u = u.reshape(t, d_ssm)
    n_groups = d_ssm // SSM_CH
    chunks_per_seq = seq // SSM_CHUNK
    n_chunks = t // SSM_CHUNK
    width = SSM_CHUNK * SSM_CH
    u2 = u.astype(BF16).reshape(n_chunks, SSM_CHUNK, n_groups, SSM_CH).transpose(2, 0, 1, 3)
    tables = _s5_tables(log_dt, a_re, a_im, b_re, b_im, c_re, c_im, chunks_per_seq)
    y2 = s5_scan(u2.reshape(n_groups, n_chunks, width), *tables, chunks_per_seq)
    y_ssm = y2.reshape(n_groups, n_chunks, SSM_CHUNK, SSM_CH).transpose(1, 2, 0, 3).reshape(t, d_ssm)

    return mix_out(x2, o_att, y_ssm, u, d_skip, w_glu, b_glu, mix_norm_att, mix_norm_ssm, w_out)


def kernel(x, attn_norm_g, w_in, cmp_pe, cmp_w1, cmp_b1, cmp_w2, cmp_b2, log_dt, a_re, a_im, b_re, b_im, c_re, c_im, d_skip, w_glu, b_glu, mix_norm_att, mix_norm_ssm, w_out, ffn_norm_g, w_gate, w_up, w_down, final_norm_g):
    bsz, seq, d = x.shape
    depth = w_in.shape[0]
    x2 = x.reshape(bsz * seq, d)
    for l in range(depth):
        x2 = _mixer(x2, bsz, seq, attn_norm_g[l], w_in[l], cmp_pe[l], cmp_w1[l], cmp_b1[l], cmp_w2[l],
                    cmp_b2[l], log_dt[l], a_re[l], a_im[l], b_re[l], b_im[l], c_re[l], c_im[l], d_skip[l],
                    w_glu[l], b_glu[l], mix_norm_att[l], mix_norm_ssm[l], w_out[l])
        x2 = ffn(x2, ffn_norm_g[l], w_gate[l], w_up[l], w_down[l], final_norm_g, final_norm=(l == depth - 1))
    return x2.reshape(bsz, seq, d)
```

```python
import functools
import math

import jax
import jax.numpy as jnp
from jax import lax
from jax.experimental import pallas as pl
from jax.experimental.pallas import tpu as pltpu

HEAD_DIM = 64
N_KV = 3
GQA = 4
N_HEADS = N_KV * GQA
D_ATT = N_HEADS * HEAD_DIM
SSM_CH = 16
SSM_STATE = 64
L_CMP = 32
STRIDE_CMP = 16
CMP_HID = 256
L_SEL = 64
TOP_K = 16
N_FORCED = 3
WINDOW = 512
Q_BLK = 128
BIG = 1e9
TINY = 1e-30
EPS = 1e-6
LOG2E = 1.4426950408889634

SUBLANES = 8
V_ROWS = HEAD_DIM + 2 * SUBLANES
SSM_CHUNK = 64
SEL_SUB = 256
SEL_LOOP = 32
SEL_DIAG = 8
FRONT_PARTS = 4
SEL_AHEAD = 2
VMEM_LIMIT = 56 * 1024 * 1024

F32 = jnp.float32
BF16 = jnp.bfloat16
NEG_INF = float("-inf")


def _dot(a, b):
    return jnp.dot(a, b, preferred_element_type=F32)


def _dot_nt(a, b):
    return lax.dot_general(a, b, (((1,), (1,)), ((), ())), preferred_element_type=F32)


def _dot_split(a, b):
    a_hi = a.astype(BF16)
    a_lo = (a - a_hi.astype(F32)).astype(BF16)
    b_hi = b.astype(BF16)
    b_lo = (b - b_hi.astype(F32)).astype(BF16)
    return _dot(a_hi, b_hi) + _dot(a_hi, b_lo) + _dot(a_lo, b_hi)


def _params(*sem):
    return pltpu.CompilerParams(dimension_semantics=sem, vmem_limit_bytes=VMEM_LIMIT)


def _row_tile(t):
    for tm in (512, 256, 128, 64, 32, 16, 8):
        if t % tm == 0:
            return tm
    raise ValueError(f"token count {t} must be a multiple of 8")


def _rms(x, g):
    ms = jnp.mean(x * x, axis=-1, keepdims=True)
    return x * lax.rsqrt(ms + EPS) * g


def _in_proj_kernel(x_ref, g_ref, wn_ref, wt_ref, qt_ref, vst_ref, vwt_ref, gt_ref, kc_ref, vc_ref,
                    ks_ref, kw_ref, u_ref):
    p_kv = N_KV * HEAD_DIM
    h = _rms(x_ref[0], g_ref[...]).astype(BF16)
    tm = h.shape[0]
    nat = _dot(h, wn_ref[...])
    tr = _dot_nt(wt_ref[...], h)
    qt_ref[0] = (tr[:D_ATT] * (LOG2E * HEAD_DIM ** -0.5)).astype(BF16)
    ones_rows = (lax.broadcasted_iota(jnp.int32, (V_ROWS - HEAD_DIM, tm), 0) == 0).astype(BF16)
    for hd in range(N_KV):
        rows = slice(hd * HEAD_DIM, (hd + 1) * HEAD_DIM)
        for k, ref in enumerate((vst_ref, vwt_ref)):
            off = D_ATT + k * p_kv
            ref[0, hd, :HEAD_DIM, :] = tr[off + hd * HEAD_DIM:off + (hd + 1) * HEAD_DIM].astype(BF16)
            ref[0, hd, HEAD_DIM:, :] = ones_rows
        kc_ref[0, hd] = nat[:, rows]
        vc_ref[0, hd] = nat[:, p_kv + hd * HEAD_DIM:p_kv + (hd + 1) * HEAD_DIM]
        ks_ref[0, hd] = nat[:, 2 * p_kv + hd * HEAD_DIM:2 * p_kv + (hd + 1) * HEAD_DIM].astype(BF16)
        kw_ref[0, hd] = nat[:, 3 * p_kv + hd * HEAD_DIM:3 * p_kv + (hd + 1) * HEAD_DIM].astype(BF16)
    gt_ref[0] = tr[D_ATT + 2 * p_kv:D_ATT + 2 * p_kv + 3 * N_HEADS]
    u_ref[0] = nat[:, 4 * p_kv:]


def in_proj(x3, g, w_in):
    bsz, seq, d = x3.shape
    p_kv = N_KV * HEAD_DIM
    p_gate = 3 * N_HEADS
    d_ssm = d - D_ATT
    o = [0, D_ATT]
    for width in [p_kv] * 6 + [p_gate, d_ssm]:
        o.append(o[-1] + width)
    q, kc, vc, ksl, vsl, kwn, vwn, gl, u = (w_in[:, o[i]:o[i + 1]] for i in range(9))
    w_nat = jnp.concatenate([kc, vc, ksl, kwn, u], axis=1).astype(BF16)
    n_tr = D_ATT + 2 * p_kv + p_gate
    w_tr = jnp.pad(jnp.concatenate([q, vsl, vwn, gl], axis=1), ((0, 0), (0, -n_tr % SUBLANES))).T.astype(BF16)
    tm = _row_tile(seq)
    full = lambda a: pl.BlockSpec(a.shape, lambda b, i: (0,) * a.ndim)
    feat = lambda rows: pl.BlockSpec((1, rows, tm), lambda b, i: (b, 0, i))
    vals = pl.BlockSpec((1, N_KV, V_ROWS, tm), lambda b, i: (b, 0, 0, i))
    keys = pl.BlockSpec((1, N_KV, tm, HEAD_DIM), lambda b, i: (b, 0, i, 0))
    sds = jax.ShapeDtypeStruct
    g2 = g.reshape(1, d)
    return pl.pallas_call(
        _in_proj_kernel,
        grid=(bsz, seq // tm),
        in_specs=[pl.BlockSpec((1, tm, d), lambda b, i: (b, i, 0)), full(g2), full(w_nat), full(w_tr)],
        out_specs=[feat(D_ATT), vals, vals, feat(p_gate), keys, keys, keys, keys,
                   pl.BlockSpec((1, tm, d_ssm), lambda b, i: (b, i, 0))],
        out_shape=[sds((bsz, D_ATT, seq), BF16), sds((bsz, N_KV, V_ROWS, seq), BF16),
                   sds((bsz, N_KV, V_ROWS, seq), BF16), sds((bsz, p_gate, seq), F32),
                   sds((bsz, N_KV, seq, HEAD_DIM), F32), sds((bsz, N_KV, seq, HEAD_DIM), F32),
                   sds((bsz, N_KV, seq, HEAD_DIM), BF16), sds((bsz, N_KV, seq, HEAD_DIM), BF16),
                   sds((bsz, seq, d_ssm), F32)],
        compiler_params=_params("parallel", "parallel"),
        name="in_proj",
    )(x3, g2, w_nat, w_tr)


def _compress_kernel(x_ref, pe_ref, w1_ref, b1_ref, w2_ref, b2_ref, o_ref, *, transposed):
    seq = x_ref.shape[2]
    n = seq // STRIDE_CMP
    first = jnp.zeros((n, CMP_HID), F32)
    second = jnp.zeros((n, CMP_HID), F32)
    for j in range(STRIDE_CMP):
        tok = x_ref[0, 0, pl.ds(j, n, stride=STRIDE_CMP), :]
        w_a = w1_ref[j * HEAD_DIM:(j + 1) * HEAD_DIM, :]
        w_b = w1_ref[(STRIDE_CMP + j) * HEAD_DIM:(STRIDE_CMP + j + 1) * HEAD_DIM, :]
        first = first + _dot((tok + pe_ref[j:j + 1, :]).astype(BF16), w_a)
        second = second + _dot((tok + pe_ref[STRIDE_CMP + j:STRIDE_CMP + j + 1, :]).astype(BF16), w_b)
    hid = first + pltpu.roll(second, n - 1, axis=0) + b1_ref[...]
    hid = jax.nn.gelu(hid).astype(BF16)
    if transposed:
        o_ref[0] = (_dot_nt(w2_ref[...], hid) + b2_ref[...]).astype(o_ref.dtype)
    else:
        o_ref[0] = (_dot(hid, w2_ref[...]) + b2_ref[...]).astype(o_ref.dtype)


def compress(x, pe, w1, b1, w2, b2, transposed):
    bsz, n_kv, seq, _ = x.shape
    bh = bsz * n_kv
    n = seq // STRIDE_CMP
    w1k = w1.astype(BF16)
    if transposed:
        pad = V_ROWS - HEAD_DIM
        w2k = jnp.pad(w2.T, ((0, pad), (0, 0))).astype(BF16)
        b2k = jnp.concatenate([b2, jnp.ones((1,), F32), jnp.zeros((pad - 1,), F32)]).reshape(V_ROWS, 1)
        out_block, out_shape = (1, V_ROWS, n), (bh, V_ROWS, n)
    else:
        w2k = w2.astype(BF16)
        b2k = b2.reshape(1, HEAD_DIM)
        out_block, out_shape = (1, n, HEAD_DIM), (bh, n, HEAD_DIM)
    full = lambda a: pl.BlockSpec(a.shape, lambda i: (0,) * a.ndim)
    b1k = b1.reshape(1, CMP_HID)
    return pl.pallas_call(
        functools.partial(_compress_kernel, transposed=transposed),
        grid=(bh,),
        in_specs=[pl.BlockSpec((1, 1, seq, HEAD_DIM), lambda i: (i // n_kv, i % n_kv, 0, 0)),
                  full(pe), full(w1k), full(b1k), full(w2k), full(b2k)],
        out_specs=pl.BlockSpec(out_block, lambda i: (i, 0, 0)),
        out_shape=jax.ShapeDtypeStruct(out_shape, BF16),
        compiler_params=_params("parallel"),
        name="compress_t" if transposed else "compress_n",
    )(x, pe, w1k, b1k, w2k, b2k)


def _heads(x):
    return [x[:, g * Q_BLK:(g + 1) * Q_BLK] for g in range(GQA)]


def _exp2_safe_max(m):
    return jnp.where(m == NEG_INF, 0.0, m)


def _normalise(pv):
    return pv[:HEAD_DIM] / jnp.maximum(pv[HEAD_DIM:HEAD_DIM + 1], TINY)


def _attn_kernel(qt_ref, kc_ref, vct_ref, ks_ref, vst_ref, kw_ref, vwt_ref, gt_ref, ovt_ref, cthr_ref,
                 o_ref, selb_ref, s_ref, *, seq):
    qi = pl.program_id(1)
    n_sel = seq // L_SEL
    lanes = GQA * Q_BLK
    win_keys = WINDOW + Q_BLK

    qt = jnp.concatenate([qt_ref[0, g] for g in range(GQA)], axis=1)
    q0 = qi * Q_BLK
    lane_q = lax.broadcasted_iota(jnp.int32, (1, Q_BLK), 1)
    t_q = q0 + lane_q
    t_all = jnp.concatenate([t_q] * GQA, axis=1)

    def sel_scores(k0):
        return _dot(ks_ref[0, pl.ds(k0, SEL_SUB), :], qt)

    def front(n_rows, full_band):
        n_blk = n_rows * STRIDE_CMP // L_SEL
        s_c = _dot(kc_ref[0, :n_rows, :], qt)
        for a in range(SEL_AHEAD):
            s_ref[a] = sel_scores(a * SEL_SUB)
        vis = cthr_ref[:n_rows, :] <= q0
        p_heads = []
        for s_g in _heads(s_c):
            s_g = jnp.where(vis, s_g, NEG_INF)
            m_g = _exp2_safe_max(jnp.max(s_g, axis=0, keepdims=True))
            p_heads.append(jnp.exp2(s_g - m_g).astype(BF16))
        p_c = jnp.concatenate(p_heads, axis=1)
        pv_c = _dot(vct_ref[0, :, :n_rows], p_c)
        r_c = 1.0 / jnp.maximum(pv_c[HEAD_DIM:HEAD_DIM + 1], TINY)
        o_c = pv_c[:HEAD_DIM] * r_c
        imp_all = _dot(ovt_ref[:n_blk, :n_rows], p_c) * r_c
        imp = functools.reduce(jnp.add, _heads(imp_all))

        blk = lax.broadcasted_iota(jnp.int32, (n_blk, Q_BLK), 0).astype(F32)
        cur = (t_q // L_SEL).astype(F32)
        forced = (blk == 0.0) | (blk == cur) | (blk == cur - 1.0)
        free = (blk <= cur) & jnp.logical_not(forced)
        cand = jnp.where(free, imp, NEG_INF)
        n_pick = min(TOP_K, n_sel) - N_FORCED

        def rank_exact(val):
            for _ in range(n_pick):
                mx = jnp.max(val, axis=0, keepdims=True)
                first = jnp.min(jnp.where(val == mx, blk, float(n_blk)), axis=0, keepdims=True)
                val = jnp.where(blk == first, NEG_INF, val)
            return val

        val_fast = cand
        for _ in range(n_pick):
            val_fast = jnp.where(val_fast == jnp.max(val_fast, axis=0, keepdims=True), NEG_INF, val_fast)
        n_free = jnp.sum(free.astype(F32), axis=0, keepdims=True)
        n_got = jnp.sum((free & (val_fast == NEG_INF)).astype(F32), axis=0, keepdims=True)
        tied = jnp.max(jnp.where(n_got != jnp.minimum(n_free, float(n_pick)), 1.0, 0.0)) > 0.0

        start = pl.multiple_of(jnp.maximum(q0 - WINDOW, 0), Q_BLK)
        s_w = _dot(kw_ref[0, pl.ds(start, win_keys), :], qt)
        r_minus_c = (lax.broadcasted_iota(jnp.int32, (Q_BLK, Q_BLK), 0)
                     - lax.broadcasted_iota(jnp.int32, (Q_BLK, Q_BLK), 1))
        chunks = []
        n_chunks = win_keys // Q_BLK
        for i in range(n_chunks):
            rows = s_w[i * Q_BLK:(i + 1) * Q_BLK]
            if not full_band or i in (0, n_chunks - 1):
                hi = q0 - start - i * Q_BLK
                mask = (r_minus_c <= hi) & (r_minus_c > hi - WINDOW)
                rows = jnp.concatenate([jnp.where(mask, x, NEG_INF) for x in _heads(rows)], axis=1)
            chunks.append(rows)
        m_w = _exp2_safe_max(functools.reduce(jnp.maximum, [jnp.max(x, axis=0, keepdims=True) for x in chunks]))
        p_w = jnp.concatenate([jnp.exp2(x - m_w).astype(BF16) for x in chunks], axis=0)
        o_w = _normalise(_dot(vwt_ref[0, :, pl.ds(start, win_keys)], p_w))

        val = lax.cond(tied, lambda: rank_exact(cand), lambda: val_fast)
        selb_ref[:n_blk, :] = jnp.where(forced | (free & (val == NEG_INF)), 0.0, NEG_INF)
        return o_c, o_w

    n_cmp = seq // STRIDE_CMP
    parts = FRONT_PARTS if n_cmp % (FRONT_PARTS * Q_BLK) == 0 else 1
    o_c, o_w = lax.switch(
        q0 // (seq // parts),
        [functools.partial(front, n_cmp * (k + 1) // parts, k * (seq // parts) >= WINDOW) for k in range(parts)])

    row_iota = lax.broadcasted_iota(jnp.int32, (L_SEL, lanes), 0)

    def sel_update(s_t, k0, m_prev, acc_prev, causal):
        blk0 = k0 // L_SEL
        blocks, biases, part_max = [], [], None
        for j in range(SEL_SUB // L_SEL):
            bias = jnp.concatenate([selb_ref[pl.ds(blk0 + j, 1), :]] * GQA, axis=1)
            s_b = s_t[j * L_SEL:(j + 1) * L_SEL]
            if causal:
                s_b = jnp.where(k0 + j * L_SEL + row_iota <= t_all, s_b, NEG_INF)
            s8 = functools.reduce(jnp.maximum, [s_b[r:r + SUBLANES] for r in range(0, L_SEL, SUBLANES)])
            cand = s8 + bias
            part_max = cand if part_max is None else jnp.maximum(part_max, cand)
            blocks.append(s_b)
            biases.append(bias)
        m_new = jnp.maximum(m_prev, jnp.max(part_max, axis=0, keepdims=True))
        m_safe = _exp2_safe_max(m_new)
        alpha = jnp.exp2(m_prev - m_safe)
        p = jnp.concatenate([jnp.exp2(s_b - (m_safe - bias)).astype(BF16)
                             for s_b, bias in zip(blocks, biases)], axis=0)
        pv = _dot(vst_ref[0, :, pl.ds(k0, SEL_SUB)], p)
        return m_new, alpha * acc_prev + pv

    def sel_run(k0, m_prev, acc_prev, n_sub, diagonal):
        queue = [s_ref[a] for a in range(min(SEL_AHEAD, n_sub) if diagonal else SEL_AHEAD)]
        for i in range(n_sub):
            ahead = i + SEL_AHEAD
            if not diagonal or ahead < n_sub:
                queue.append(sel_scores(pl.multiple_of(k0 + ahead * SEL_SUB, SEL_SUB)))
            m_prev, acc_prev = sel_update(queue.pop(0), pl.multiple_of(k0 + i * SEL_SUB, SEL_SUB),
                                          m_prev, acc_prev, causal=diagonal and i == n_sub - 1)
        if not diagonal:
            for a in range(SEL_AHEAD):
                s_ref[a] = queue[a]
        return m_prev, acc_prev

    carry = (jnp.full((1, lanes), NEG_INF, F32), jnp.zeros((V_ROWS, lanes), F32))
    n_below = q0 // SEL_SUB
    loop_keys = SEL_LOOP * SEL_SUB
    carry = lax.fori_loop(
        0, n_below // SEL_LOOP,
        lambda kt, c: sel_run(pl.multiple_of(kt * loop_keys, loop_keys), *c, SEL_LOOP, False), carry)
    done = (n_below // SEL_LOOP) * SEL_LOOP
    run = SEL_LOOP // 2
    while run >= SEL_DIAG:
        take = ((n_below - done) // run) == 1
        k_run = pl.multiple_of(done * SEL_SUB, run * SEL_SUB)
        carry = lax.cond(take, functools.partial(lambda k, n, m, a: sel_run(k, m, a, n, False), k_run, run),
                         lambda m, a: (m, a), *carry)
        done = done + jnp.where(take, run, 0)
        run //= 2
    k_last = pl.multiple_of(done * SEL_SUB, SEL_DIAG * SEL_SUB)
    acc_s = lax.switch(
        n_below - done,
        [functools.partial(lambda n, m, a: sel_run(k_last, m, a, n + 1, True)[1], n) for n in range(SEL_DIAG)],
        *carry)
    o_s = _normalise(acc_s)

    outs = []
    for g in range(GQA):
        sl = slice(g * Q_BLK, (g + 1) * Q_BLK)
        gate = jax.nn.sigmoid(gt_ref[0, g])
        outs.append(gate[0:1] * o_c[:, sl] + gate[1:2] * o_s[:, sl] + gate[2:3] * o_w[:, sl])
    pairs = [jnp.concatenate(outs[g:g + 2], axis=0).T for g in range(0, GQA, 2)]
    o_ref[0] = jnp.concatenate(pairs, axis=1)


def nsa_attention(qt, kc, vct, ks, vst, kw, vwt, gt, bsz):
    bh, _, _, seq = qt.shape
    assert SEL_LOOP % SEL_DIAG == 0 and seq % (SEL_DIAG * SEL_SUB) == 0 and seq >= WINDOW + Q_BLK
    n_cmp = seq // STRIDE_CMP
    n_sel = seq // L_SEL
    cmp_start = jnp.arange(n_cmp) * STRIDE_CMP
    sel_start = jnp.arange(n_sel) * L_SEL
    ovt = ((cmp_start[None, :] < sel_start[:, None] + L_SEL)
           & (cmp_start[None, :] + L_CMP > sel_start[:, None])).astype(BF16)
    cthr = (cmp_start + L_CMP - 1)[:, None] - jnp.arange(Q_BLK)[None, :]
    per_bh = lambda shape: pl.BlockSpec((1,) + shape, lambda b, i: (b,) + (0,) * len(shape))
    return pl.pallas_call(
        functools.partial(_attn_kernel, seq=seq),
        grid=(bh, seq // Q_BLK),
        in_specs=[pl.BlockSpec((1, GQA, HEAD_DIM, Q_BLK), lambda b, i: (b, 0, 0, i)),
                  per_bh((n_cmp, HEAD_DIM)), per_bh((V_ROWS, n_cmp)),
                  per_bh((seq, HEAD_DIM)), per_bh((V_ROWS, seq)),
                  per_bh((seq, HEAD_DIM)), per_bh((V_ROWS, seq)),
                  pl.BlockSpec((1, GQA, 3, Q_BLK), lambda b, i: (b, 0, 0, i)),
                  pl.BlockSpec((n_sel, n_cmp), lambda b, i: (0, 0)),
                  pl.BlockSpec((n_cmp, Q_BLK), lambda b, i: (0, 0))],
        out_specs=pl.BlockSpec((1, Q_BLK, GQA * HEAD_DIM), lambda b, i: (b // N_KV, i, b % N_KV)),
        out_shape=jax.ShapeDtypeStruct((bsz, seq, N_KV * GQA * HEAD_DIM), F32),
        scratch_shapes=[pltpu.VMEM((n_sel, Q_BLK), F32), pltpu.VMEM((SEL_AHEAD, SEL_SUB, GQA * Q_BLK), F32)],
        compiler_params=_params("parallel", "arbitrary"),
        name="nsa_attention",
    )(qt, kc, vct, ks, vst, kw, vwt, gt, ovt, cthr.astype(jnp.int32))


def _ssm_kernel(u_ref, pt_ref, bq_ref, w_ref, v_ref, a1_ref, a2_ref, o_ref, mt_ref, *, chunks_per_seq):
    width = SSM_CHUNK * SSM_CH
    u = u_ref[0]
    kt = _dot_split(bq_ref[0], pt_ref[0])
    col = lax.broadcasted_iota(jnp.int32, (SSM_CH, width), 1)
    for s in range(SSM_CHUNK):
        shifted = kt if s == 0 else pltpu.roll(kt, SSM_CH * s, axis=1)
        mt_ref[s * SSM_CH:(s + 1) * SSM_CH, :] = jnp.where(col >= SSM_CH * s, shifted, 0.0).astype(BF16)
    y = _dot(u, mt_ref[...])
    x = _dot(u, w_ref[0])
    row = lax.broadcasted_iota(jnp.int32, x.shape, 0) % chunks_per_seq
    steps = int(math.log2(chunks_per_seq))
    for j in range(steps):
        d = 1 << j
        sh = jnp.where(row >= d, pltpu.roll(x, d, axis=0), 0.0)
        x = x + a1_ref[0, j:j + 1, :] * sh + a2_ref[0, j:j + 1, :] * pltpu.roll(sh, SSM_STATE, axis=1)
    prev = jnp.where(row >= 1, pltpu.roll(x, 1, axis=0), 0.0)
    o_ref[0] = y + _dot(prev.astype(BF16), v_ref[0])


def s5_scan(u2, pt, bq, w, v, a1, a2, chunks_per_seq):
    g, n_chunks, width = u2.shape
    per_g = lambda a: pl.BlockSpec((1,) + a.shape[1:], lambda i: (i,) + (0,) * (a.ndim - 1))
    return pl.pallas_call(
        functools.partial(_ssm_kernel, chunks_per_seq=chunks_per_seq),
        grid=(g,),
        in_specs=[per_g(u2), per_g(pt), per_g(bq), per_g(w), per_g(v), per_g(a1), per_g(a2)],
        out_specs=pl.BlockSpec((1, n_chunks, width), lambda i: (i, 0, 0)),
        out_shape=jax.ShapeDtypeStruct((g, n_chunks, width), F32),
        scratch_shapes=[pltpu.VMEM((width, width), BF16)],
        compiler_params=_params("parallel"),
        name="s5_scan",
    )(u2, pt, bq, w, v, a1, a2)


def _s5_tables(log_dt, a_re, a_im, b_re, b_im, c_re, c_im, chunks_per_seq):
    g, n = a_re.shape
    dt = jnp.exp(log_dt)[:, None]
    lam_re, lam_im = dt * a_re, dt * a_im

    def power(k):
        k = k.astype(F32)[None, :, None]
        mag = jnp.exp(k * lam_re[:, None, :])
        return mag * jnp.cos(k * lam_im[:, None, :]), mag * jnp.sin(k * lam_im[:, None, :])

    abar_re, abar_im = (x[:, 0] for x in power(jnp.ones((1,))))
    den = a_re * a_re + a_im * a_im
    nr = abar_re - 1.0
    f_re = (nr * a_re + abar_im * a_im) / den
    f_im = (abar_im * a_re - nr * a_im) / den
    bb_re = f_re[..., None] * b_re - f_im[..., None] * b_im
    bb_im = f_re[..., None] * b_im + f_im[..., None] * b_re
    steps = jnp.arange(SSM_CHUNK)
    width = SSM_CHUNK * SSM_CH
    e_re, e_im = power(steps)
    ca_re = c_re[:, None] * e_re[:, :, None, :] - c_im[:, None] * e_im[:, :, None, :]
    ca_im = c_re[:, None] * e_im[:, :, None, :] + c_im[:, None] * e_re[:, :, None, :]
    pt = jnp.concatenate([ca_re, -ca_im], axis=-1).reshape(g, width, 2 * n).transpose(0, 2, 1)
    bq = jnp.concatenate([bb_re, bb_im], axis=1).transpose(0, 2, 1)
    r_re, r_im = power(SSM_CHUNK - 1 - steps)
    bt_re, bt_im = bb_re.transpose(0, 2, 1)[:, None], bb_im.transpose(0, 2, 1)[:, None]
    w_re = r_re[:, :, None, :] * bt_re - r_im[:, :, None, :] * bt_im
    w_im = r_re[:, :, None, :] * bt_im + r_im[:, :, None, :] * bt_re
    w = jnp.concatenate([w_re, w_im], axis=-1).reshape(g, width, 2 * n).astype(BF16)
    n_re, n_im = power(steps + 1)
    cv_re = c_re[:, None] * n_re[:, :, None, :] - c_im[:, None] * n_im[:, :, None, :]
    cv_im = c_re[:, None] * n_im[:, :, None, :] + c_im[:, None] * n_re[:, :, None, :]
    v = jnp.concatenate([cv_re, -cv_im], axis=-1).reshape(g, width, 2 * n).transpose(0, 2, 1).astype(BF16)
    n_steps = int(math.log2(chunks_per_seq))
    s_re, s_im = power(SSM_CHUNK * (2 ** jnp.arange(max(n_steps, 1))))
    a1 = jnp.concatenate([s_re, s_re], axis=-1)
    a2 = jnp.concatenate([-s_im, s_im], axis=-1)
    return pt, bq, w, v, a1, a2


def _mix_out_kernel(x_ref, oa_ref, ys_ref, u_ref, dsk_ref, wglu_ref, bglu_ref, na_ref, ns_ref,
                    woa_ref, wos_ref, o_ref):
    d_ssm = u_ref.shape[1]
    y = jax.nn.gelu(ys_ref[...] + dsk_ref[...] * u_ref[...])
    z = _dot(y.astype(BF16), wglu_ref[...]) + bglu_ref[...]
    o_ssm = z[:, :d_ssm] * jax.nn.sigmoid(z[:, d_ssm:])
    att_n = _rms(oa_ref[...], na_ref[...]).astype(BF16)
    ssm_n = _rms(o_ssm, ns_ref[...]).astype(BF16)
    o_ref[...] = x_ref[...] + _dot(att_n, woa_ref[...]) + _dot(ssm_n, wos_ref[...])


def mix_out(x2, o_att, y_ssm, u, d_skip, w_glu, b_glu, n_att, n_ssm, w_out):
    t, d = x2.shape
    d_att = o_att.shape[1]
    d_ssm = u.shape[1]
    tm = _row_tile(t)
    rows = lambda w: pl.BlockSpec((tm, w), lambda i: (i, 0))
    full = lambda a: pl.BlockSpec(a.shape, lambda i: (0,) * a.ndim)
    consts = [d_skip.reshape(1, d_ssm), w_glu.astype(BF16), b_glu.reshape(1, 2 * d_ssm),
              n_att.reshape(1, d_att), n_ssm.reshape(1, d_ssm),
              w_out[:d_att].astype(BF16), w_out[d_att:].astype(BF16)]
    return pl.pallas_call(
        _mix_out_kernel,
        grid=(t // tm,),
        in_specs=[rows(d), rows(d_att), rows(d_ssm), rows(d_ssm)] + [full(c) for c in consts],
        out_specs=rows(d),
        out_shape=jax.ShapeDtypeStruct((t, d), F32),
        compiler_params=_params("parallel"),
        name="mix_out",
    )(x2, o_att, y_ssm, u, *consts)


def _ffn_kernel(x_ref, g_ref, wg_ref, wu_ref, wd_ref, fg_ref, o_ref, acc_ref, *, f_chunk, final_norm):
    x = x_ref[...]
    h = _rms(x, g_ref[...]).astype(BF16)
    d_ff = wg_ref.shape[1]
    acc_ref[...] = x
    for c in range(d_ff // f_chunk):
        sl = slice(c * f_chunk, (c + 1) * f_chunk)
        a = jax.nn.silu(_dot(h, wg_ref[:, sl])) * _dot(h, wu_ref[:, sl])
        acc_ref[...] += _dot(a.astype(BF16), wd_ref[sl, :])
    out = acc_ref[...]
    if final_norm:
        out = _rms(out, fg_ref[...])
    o_ref[...] = out


def ffn(x2, g, w_gate, w_up, w_down, final_g, final_norm):
    t, d = x2.shape
    d_ff = w_gate.shape[1]
    tm = _row_tile(t)
    f_chunk = 256 if d_ff % 256 == 0 else d_ff
    full = lambda a: pl.BlockSpec(a.shape, lambda i: (0,) * a.ndim)
    consts = [g.reshape(1, d), w_gate.astype(BF16), w_up.astype(BF16), w_down.astype(BF16),
              final_g.reshape(1, d)]
    return pl.pallas_call(
        functools.partial(_ffn_kernel, f_chunk=f_chunk, final_norm=final_norm),
        grid=(t // tm,),
        in_specs=[pl.BlockSpec((tm, d), lambda i: (i, 0))] + [full(c) for c in consts],
        out_specs=pl.BlockSpec((tm, d), lambda i: (i, 0)),
        out_shape=jax.ShapeDtypeStruct((t, d), F32),
        scratch_shapes=[pltpu.VMEM((tm, d), F32)],
        compiler_params=_params("parallel"),
        name="ffn",
    )(x2, *consts)


def _mixer(x2, bsz, seq, attn_norm_g, w_in, cmp_pe, cmp_w1, cmp_b1, cmp_w2, cmp_b2, log_dt, a_re, a_im,
           b_re, b_im, c_re, c_im, d_skip, w_glu, b_glu, mix_norm_att, mix_norm_ssm, w_out):
    t, d = x2.shape
    d_ssm = d - D_ATT
    bh = bsz * N_KV
    qt, vst, vwt, gt, kc, vc, ks, kw, u = in_proj(x2.reshape(bsz, seq, d), attn_norm_g, w_in)
    k_c = compress(kc, cmp_pe[0], cmp_w1[0], cmp_b1[0], cmp_w2[0], cmp_b2[0], transposed=False)
    v_ct = compress(vc, cmp_pe[1], cmp_w1[1], cmp_b1[1], cmp_w2[1], cmp_b2[1], transposed=True)
    o_att = nsa_attention(qt.reshape(bh, GQA, HEAD_DIM, seq), k_c, v_ct,
                          ks.reshape(bh, seq, HEAD_DIM), vst.reshape(bh, V_ROWS, seq),
                          kw.reshape(bh, seq, HEAD_DIM), vwt.reshape(bh, V_ROWS, seq),
                          gt.reshape(bh, GQA, 3, seq), bsz).reshape(t, D_ATT)

    u = u.reshape(t, d_ssm)
    n_groups = d_ssm // SSM_CH
    chunks_per_seq = seq // SSM_CHUNK
    n_chunks = t // SSM_CHUNK
    width = SSM_CHUNK * SSM_CH
    u2 = u.astype(BF16).reshape(n_chunks, SSM_CHUNK, n_groups, SSM_CH).transpose(2, 0, 1, 3)
    tables = _s5_tables(log_dt, a_re, a_im, b_re, b_im, c_re, c_im, chunks_per_seq)
    y2 = s5_scan(u2.reshape(n_groups, n_chunks, width), *tables, chunks_per_seq)
    y_ssm = y2.reshape(n_groups, n_chunks, SSM_CHUNK, SSM_CH).transpose(1, 2, 0, 3).reshape(t, d_ssm)

    return mix_out(x2, o_att, y_ssm, u, d_skip, w_glu, b_glu, mix_norm_att, mix_norm_ssm, w_out)


def kernel(x, attn_norm_g, w_in, cmp_pe, cmp_w1, cmp_b1, cmp_w2, cmp_b2, log_dt, a_re, a_im, b_re, b_im, c_re, c_im, d_skip, w_glu, b_glu, mix_norm_att, mix_norm_ssm, w_out, ffn_norm_g, w_gate, w_up, w_down, final_norm_g):
    bsz, seq, d = x.shape
    depth = w_in.shape[0]
    x2 = x.reshape(bsz * seq, d)
    for l in range(depth):
        x2 = _mixer(x2, bsz, seq, attn_norm_g[l], w_in[l], cmp_pe[l], cmp_w1[l], cmp_b1[l], cmp_w2[l],
                    cmp_b2[l], log_dt[l], a_re[l], a_im[l], b_re[l], b_im[l], c_re[l], c_im[l], d_skip[l],
                    w_glu[l], b_glu[l], mix_norm_att[l], mix_norm_ssm[l], w_out[l])
        x2 = ffn(x2, ffn_norm_g[l], w_gate[l], w_up[l], w_down[l], final_norm_g, final_norm=(l == depth - 1))
    return x2.reshape(bsz, seq, d)
```

```python
import functools
import math

import jax
import jax.numpy as jnp
from jax import lax
from jax.experimental import pallas as pl
from jax.experimental.pallas import tpu as pltpu

HEAD_DIM = 64
N_KV = 3
GQA = 4
N_HEADS = N_KV * GQA
D_ATT = N_HEADS * HEAD_DIM
SSM_CH = 16
SSM_STATE = 64
L_CMP = 32
STRIDE_CMP = 16
CMP_HID = 256
L_SEL = 64
TOP_K = 16
N_FORCED = 3
WINDOW = 512
Q_BLK = 128
BIG = 1e9
TINY = 1e-30
EPS = 1e-6
LOG2E = 1.4426950408889634

SUBLANES = 8
V_ROWS = HEAD_DIM + 2 * SUBLANES
SSM_CHUNK = 64
SEL_SUB = 256
SEL_LOOP = 32
SEL_DIAG = 8
FRONT_PARTS = 4
SEL_AHEAD = 3
VMEM_LIMIT = 56 * 1024 * 1024

F32 = jnp.float32
BF16 = jnp.bfloat16
NEG_INF = float("-inf")


def _dot(a, b):
    return jnp.dot(a, b, preferred_element_type=F32)


def _dot_nt(a, b):
    return lax.dot_general(a, b, (((1,), (1,)), ((), ())), preferred_element_type=F32)


def _dot_split(a, b):
    a_hi = a.astype(BF16)
    a_lo = (a - a_hi.astype(F32)).astype(BF16)
    b_hi = b.astype(BF16)
    b_lo = (b - b_hi.astype(F32)).astype(BF16)
    return _dot(a_hi, b_hi) + _dot(a_hi, b_lo) + _dot(a_lo, b_hi)


def _params(*sem):
    return pltpu.CompilerParams(dimension_semantics=sem, vmem_limit_bytes=VMEM_LIMIT)


def _row_tile(t):
    for tm in (512, 256, 128, 64, 32, 16, 8):
        if t % tm == 0:
            return tm
    raise ValueError(f"token count {t} must be a multiple of 8")


def _rms(x, g):
    ms = jnp.mean(x * x, axis=-1, keepdims=True)
    return x * lax.rsqrt(ms + EPS) * g


def _in_proj_kernel(x_ref, g_ref, wn_ref, wt_ref, qt_ref, vst_ref, vwt_ref, gt_ref, kc_ref, vc_ref,
                    ks_ref, kw_ref, u_ref):
    p_kv = N_KV * HEAD_DIM
    h = _rms(x_ref[0], g_ref[...]).astype(BF16)
    tm = h.shape[0]
    nat = _dot(h, wn_ref[...])
    tr = _dot_nt(wt_ref[...], h)
    qt_ref[0] = (tr[:D_ATT] * (LOG2E * HEAD_DIM ** -0.5)).astype(BF16)
    ones_rows = (lax.broadcasted_iota(jnp.int32, (V_ROWS - HEAD_DIM, tm), 0) == 0).astype(BF16)
    for hd in range(N_KV):
        rows = slice(hd * HEAD_DIM, (hd + 1) * HEAD_DIM)
        for k, ref in enumerate((vst_ref, vwt_ref)):
            off = D_ATT + k * p_kv
            ref[0, hd, :HEAD_DIM, :] = tr[off + hd * HEAD_DIM:off + (hd + 1) * HEAD_DIM].astype(BF16)
            ref[0, hd, HEAD_DIM:, :] = ones_rows
        kc_ref[0, hd] = nat[:, rows]
        vc_ref[0, hd] = nat[:, p_kv + hd * HEAD_DIM:p_kv + (hd + 1) * HEAD_DIM]
        ks_ref[0, hd] = nat[:, 2 * p_kv + hd * HEAD_DIM:2 * p_kv + (hd + 1) * HEAD_DIM].astype(BF16)
        kw_ref[0, hd] = nat[:, 3 * p_kv + hd * HEAD_DIM:3 * p_kv + (hd + 1) * HEAD_DIM].astype(BF16)
    gt_ref[0] = tr[D_ATT + 2 * p_kv:D_ATT + 2 * p_kv + 3 * N_HEADS]
    u_ref[0] = nat[:, 4 * p_kv:]


def in_proj(x3, g, w_in):
    bsz, seq, d = x3.shape
    p_kv = N_KV * HEAD_DIM
    p_gate = 3 * N_HEADS
    d_ssm = d - D_ATT
    o = [0, D_ATT]
    for width in [p_kv] * 6 + [p_gate, d_ssm]:
        o.append(o[-1] + width)
    q, kc, vc, ksl, vsl, kwn, vwn, gl, u = (w_in[:, o[i]:o[i + 1]] for i in range(9))
    w_nat = jnp.concatenate([kc, vc, ksl, kwn, u], axis=1).astype(BF16)
    n_tr = D_ATT + 2 * p_kv + p_gate
    w_tr = jnp.pad(jnp.concatenate([q, vsl, vwn, gl], axis=1), ((0, 0), (0, -n_tr % SUBLANES))).T.astype(BF16)
    tm = _row_tile(seq)
    full = lambda a: pl.BlockSpec(a.shape, lambda b, i: (0,) * a.ndim)
    feat = lambda rows: pl.BlockSpec((1, rows, tm), lambda b, i: (b, 0, i))
    vals = pl.BlockSpec((1, N_KV, V_ROWS, tm), lambda b, i: (b, 0, 0, i))
    keys = pl.BlockSpec((1, N_KV, tm, HEAD_DIM), lambda b, i: (b, 0, i, 0))
    sds = jax.ShapeDtypeStruct
    g2 = g.reshape(1, d)
    return pl.pallas_call(
        _in_proj_kernel,
        grid=(bsz, seq // tm),
        in_specs=[pl.BlockSpec((1, tm, d), lambda b, i: (b, i, 0)), full(g2), full(w_nat), full(w_tr)],
        out_specs=[feat(D_ATT), vals, vals, feat(p_gate), keys, keys, keys, keys,
                   pl.BlockSpec((1, tm, d_ssm), lambda b, i: (b, i, 0))],
        out_shape=[sds((bsz, D_ATT, seq), BF16), sds((bsz, N_KV, V_ROWS, seq), BF16),
                   sds((bsz, N_KV, V_ROWS, seq), BF16), sds((bsz, p_gate, seq), F32),
                   sds((bsz, N_KV, seq, HEAD_DIM), F32), sds((bsz, N_KV, seq, HEAD_DIM), F32),
                   sds((bsz, N_KV, seq, HEAD_DIM), BF16), sds((bsz, N_KV, seq, HEAD_DIM), BF16),
                   sds((bsz, seq, d_ssm), F32)],
        compiler_params=_params("parallel", "parallel"),
        name="in_proj",
    )(x3, g2, w_nat, w_tr)


def _compress_kernel(x_ref, pe_ref, w1_ref, b1_ref, w2_ref, b2_ref, o_ref, *, transposed):
    seq = x_ref.shape[2]
    n = seq // STRIDE_CMP
    first = jnp.zeros((n, CMP_HID), F32)
    second = jnp.zeros((n, CMP_HID), F32)
    for j in range(STRIDE_CMP):
        tok = x_ref[0, 0, pl.ds(j, n, stride=STRIDE_CMP), :]
        w_a = w1_ref[j * HEAD_DIM:(j + 1) * HEAD_DIM, :]
        w_b = w1_ref[(STRIDE_CMP + j) * HEAD_DIM:(STRIDE_CMP + j + 1) * HEAD_DIM, :]
        first = first + _dot((tok + pe_ref[j:j + 1, :]).astype(BF16), w_a)
        second = second + _dot((tok + pe_ref[STRIDE_CMP + j:STRIDE_CMP + j + 1, :]).astype(BF16), w_b)
    hid = first + pltpu.roll(second, n - 1, axis=0) + b1_ref[...]
    hid = jax.nn.gelu(hid).astype(BF16)
    if transposed:
        o_ref[0] = (_dot_nt(w2_ref[...], hid) + b2_ref[...]).astype(o_ref.dtype)
    else:
        o_ref[0] = (_dot(hid, w2_ref[...]) + b2_ref[...]).astype(o_ref.dtype)


def compress(x, pe, w1, b1, w2, b2, transposed):
    bsz, n_kv, seq, _ = x.shape
    bh = bsz * n_kv
    n = seq // STRIDE_CMP
    w1k = w1.astype(BF16)
    if transposed:
        pad = V_ROWS - HEAD_DIM
        w2k = jnp.pad(w2.T, ((0, pad), (0, 0))).astype(BF16)
        b2k = jnp.concatenate([b2, jnp.ones((1,), F32), jnp.zeros((pad - 1,), F32)]).reshape(V_ROWS, 1)
        out_block, out_shape = (1, V_ROWS, n), (bh, V_ROWS, n)
    else:
        w2k = w2.astype(BF16)
        b2k = b2.reshape(1, HEAD_DIM)
        out_block, out_shape = (1, n, HEAD_DIM), (bh, n, HEAD_DIM)
    full = lambda a: pl.BlockSpec(a.shape, lambda i: (0,) * a.ndim)
    b1k = b1.reshape(1, CMP_HID)
    return pl.pallas_call(
        functools.partial(_compress_kernel, transposed=transposed),
        grid=(bh,),
        in_specs=[pl.BlockSpec((1, 1, seq, HEAD_DIM), lambda i: (i // n_kv, i % n_kv, 0, 0)),
                  full(pe), full(w1k), full(b1k), full(w2k), full(b2k)],
        out_specs=pl.BlockSpec(out_block, lambda i: (i, 0, 0)),
        out_shape=jax.ShapeDtypeStruct(out_shape, BF16),
        compiler_params=_params("parallel"),
        name="compress_t" if transposed else "compress_n",
    )(x, pe, w1k, b1k, w2k, b2k)


def _heads(x):
    return [x[:, g * Q_BLK:(g + 1) * Q_BLK] for g in range(GQA)]


def _exp2_safe_max(m):
    return jnp.where(m == NEG_INF, 0.0, m)


def _aligned(x, m):
    return x if isinstance(x, int) else pl.multiple_of(x, m)


def _normalise(pv):
    return pv[:HEAD_DIM] / jnp.maximum(pv[HEAD_DIM:HEAD_DIM + 1], TINY)


def _attn_kernel(qt_ref, kc_ref, vct_ref, ks_ref, vst_ref, kw_ref, vwt_ref, gt_ref, ovt_ref, cthr_ref,
                 o_ref, selb_ref, s_ref, *, seq):
    qi = pl.program_id(1)
    n_sel = seq // L_SEL
    lanes = GQA * Q_BLK
    win_keys = WINDOW + Q_BLK

    qt = jnp.concatenate([qt_ref[0, g] for g in range(GQA)], axis=1)
    q0 = qi * Q_BLK
    lane_q = lax.broadcasted_iota(jnp.int32, (1, Q_BLK), 1)
    t_q = q0 + lane_q
    t_all = jnp.concatenate([t_q] * GQA, axis=1)

    def sel_scores(k0):
        return _dot(ks_ref[0, pl.ds(k0, SEL_SUB), :], qt)

    def front(n_rows, full_band, n_static):
        n_blk = n_rows * STRIDE_CMP // L_SEL
        s_c = _dot(kc_ref[0, :n_rows, :], qt)
        for a in range(SEL_AHEAD):
            s_ref[a] = sel_scores(a * SEL_SUB)
        vis = cthr_ref[:n_rows, :] <= q0
        p_heads = []
        for s_g in _heads(s_c):
            s_g = jnp.where(vis, s_g, NEG_INF)
            m_g = _exp2_safe_max(jnp.max(s_g, axis=0, keepdims=True))
            p_heads.append(jnp.exp2(s_g - m_g).astype(BF16))
        p_c = jnp.concatenate(p_heads, axis=1)
        pv_c = _dot(vct_ref[0, :, :n_rows], p_c)
        r_c = 1.0 / jnp.maximum(pv_c[HEAD_DIM:HEAD_DIM + 1], TINY)
        o_c = pv_c[:HEAD_DIM] * r_c
        imp_all = _dot(ovt_ref[:n_blk, :n_rows], p_c) * r_c
        imp = functools.reduce(jnp.add, _heads(imp_all))

        blk = lax.broadcasted_iota(jnp.int32, (n_blk, Q_BLK), 0).astype(F32)
        cur = (t_q // L_SEL).astype(F32)
        forced = (blk == 0.0) | (blk == cur) | (blk == cur - 1.0)
        free = (blk <= cur) & jnp.logical_not(forced)
        cand = jnp.where(free, imp, NEG_INF)
        n_pick = min(TOP_K, n_sel) - N_FORCED

        def rank_exact(val):
            for _ in range(n_pick):
                mx = jnp.max(val, axis=0, keepdims=True)
                first = jnp.min(jnp.where(val == mx, blk, float(n_blk)), axis=0, keepdims=True)
                val = jnp.where(blk == first, NEG_INF, val)
            return val

        val_fast = cand
        for _ in range(n_pick):
            val_fast = jnp.where(val_fast == jnp.max(val_fast, axis=0, keepdims=True), NEG_INF, val_fast)
        n_free = jnp.sum(free.astype(F32), axis=0, keepdims=True)
        n_got = jnp.sum((free & (val_fast == NEG_INF)).astype(F32), axis=0, keepdims=True)
        tied = jnp.max(jnp.where(n_got != jnp.minimum(n_free, float(n_pick)), 1.0, 0.0)) > 0.0

        start = pl.multiple_of(jnp.maximum(q0 - WINDOW, 0), Q_BLK)
        s_w = _dot(kw_ref[0, pl.ds(start, win_keys), :], qt)
        r_minus_c = (lax.broadcasted_iota(jnp.int32, (Q_BLK, Q_BLK), 0)
                     - lax.broadcasted_iota(jnp.int32, (Q_BLK, Q_BLK), 1))
        chunks = []
        n_chunks = win_keys // Q_BLK
        for i in range(n_chunks):
            rows = s_w[i * Q_BLK:(i + 1) * Q_BLK]
            if not full_band or i in (0, n_chunks - 1):
                hi = q0 - start - i * Q_BLK
                mask = (r_minus_c <= hi) & (r_minus_c > hi - WINDOW)
                rows = jnp.concatenate([jnp.where(mask, x, NEG_INF) for x in _heads(rows)], axis=1)
            chunks.append(rows)
        m_w = _exp2_safe_max(functools.reduce(jnp.maximum, [jnp.max(x, axis=0, keepdims=True) for x in chunks]))
        p_w = jnp.concatenate([jnp.exp2(x - m_w).astype(BF16) for x in chunks], axis=0)
        o_w = _normalise(_dot(vwt_ref[0, :, pl.ds(start, win_keys)], p_w))

        val = lax.cond(tied, lambda: rank_exact(cand), lambda: val_fast)
        selb_ref[:n_blk, :] = jnp.where(forced | (free & (val == NEG_INF)), 0.0, NEG_INF)
        carry = (jnp.full((1, lanes), NEG_INF, F32), jnp.zeros((V_ROWS, lanes), F32))
        if n_static:
            carry = sel_run(0, *carry, n_static, False)
        return (o_c, o_w) + tuple(carry)

    row_iota = lax.broadcasted_iota(jnp.int32, (L_SEL, lanes), 0)

    def sel_update(s_t, k0, m_prev, acc_prev, causal):
        blk0 = k0 // L_SEL
        blocks, biases, part_max = [], [], None
        for j in range(SEL_SUB // L_SEL):
            bias = jnp.concatenate([selb_ref[pl.ds(blk0 + j, 1), :]] * GQA, axis=1)
            s_b = s_t[j * L_SEL:(j + 1) * L_SEL]
            if causal:
                s_b = jnp.where(k0 + j * L_SEL + row_iota <= t_all, s_b, NEG_INF)
            s8 = functools.reduce(jnp.maximum, [s_b[r:r + SUBLANES] for r in range(0, L_SEL, SUBLANES)])
            cand = s8 + bias
            part_max = cand if part_max is None else jnp.maximum(part_max, cand)
            blocks.append(s_b)
            biases.append(bias)
        m_new = jnp.maximum(m_prev, jnp.max(part_max, axis=0, keepdims=True))
        m_safe = _exp2_safe_max(m_new)
        alpha = jnp.exp2(m_prev - m_safe)
        p = jnp.concatenate([jnp.exp2(s_b - (m_safe - bias)).astype(BF16)
                             for s_b, bias in zip(blocks, biases)], axis=0)
        pv = _dot(vst_ref[0, :, pl.ds(k0, SEL_SUB)], p)
        return m_new, alpha * acc_prev + pv

    def sel_run(k0, m_prev, acc_prev, n_sub, diagonal):
        queue = [s_ref[a] for a in range(min(SEL_AHEAD, n_sub) if diagonal else SEL_AHEAD)]
        for i in range(n_sub):
            ahead = i + SEL_AHEAD
            if not diagonal or ahead < n_sub:
                queue.append(sel_scores(_aligned(k0 + ahead * SEL_SUB, SEL_SUB)))
            m_prev, acc_prev = sel_update(queue.pop(0), _aligned(k0 + i * SEL_SUB, SEL_SUB),
                                          m_prev, acc_prev, causal=diagonal and i == n_sub - 1)
        if not diagonal:
            for a in range(SEL_AHEAD):
                s_ref[a] = queue[a]
        return m_prev, acc_prev

    n_cmp = seq // STRIDE_CMP
    parts = FRONT_PARTS if n_cmp % (FRONT_PARTS * Q_BLK) == 0 else 1
    part_subs = seq // parts // SEL_SUB
    part = q0 // (seq // parts)
    o_c, o_w, *carry = lax.switch(
        part, [functools.partial(front, n_cmp * (k + 1) // parts, k * (seq // parts) >= WINDOW, k * part_subs)
               for k in range(parts)])

    n_below = q0 // SEL_SUB
    done = part * part_subs
    if part_subs > SEL_LOOP:
        trips = (n_below - done) // SEL_LOOP
        carry = lax.fori_loop(
            0, trips, lambda kt, c: sel_run(_aligned((done + kt * SEL_LOOP) * SEL_SUB, SEL_SUB), *c, SEL_LOOP, False),
            tuple(carry))
        done = done + trips * SEL_LOOP
    run = min(SEL_LOOP, part_subs) // 2
    while run >= SEL_DIAG:
        take = ((n_below - done) // run) == 1
        k_run = _aligned(done * SEL_SUB, SEL_SUB)
        carry = lax.cond(take, functools.partial(lambda k, n, m, a: sel_run(k, m, a, n, False), k_run, run),
                         lambda m, a: (m, a), *carry)
        done = done + jnp.where(take, run, 0)
        run //= 2
    k_last = _aligned(done * SEL_SUB, SEL_SUB)
    acc_s = lax.switch(
        n_below - done,
        [functools.partial(lambda n, m, a: sel_run(k_last, m, a, n + 1, True)[1], n) for n in range(SEL_DIAG)],
        *carry)
    o_s = _normalise(acc_s)

    outs = []
    for g in range(GQA):
        sl = slice(g * Q_BLK, (g + 1) * Q_BLK)
        gate = jax.nn.sigmoid(gt_ref[0, g])
        outs.append(gate[0:1] * o_c[:, sl] + gate[1:2] * o_s[:, sl] + gate[2:3] * o_w[:, sl])
    pairs = [jnp.concatenate(outs[g:g + 2], axis=0).T for g in range(0, GQA, 2)]
    o_ref[0] = jnp.concatenate(pairs, axis=1)


def nsa_attention(qt, kc, vct, ks, vst, kw, vwt, gt, bsz):
    bh, _, _, seq = qt.shape
    assert SEL_LOOP % SEL_DIAG == 0 and seq % (SEL_DIAG * SEL_SUB) == 0 and seq >= WINDOW + Q_BLK
    n_cmp = seq // STRIDE_CMP
    n_sel = seq // L_SEL
    cmp_start = jnp.arange(n_cmp) * STRIDE_CMP
    sel_start = jnp.arange(n_sel) * L_SEL
    ovt = ((cmp_start[None, :] < sel_start[:, None] + L_SEL)
           & (cmp_start[None, :] + L_CMP > sel_start[:, None])).astype(BF16)
    cthr = (cmp_start + L_CMP - 1)[:, None] - jnp.arange(Q_BLK)[None, :]
    per_bh = lambda shape: pl.BlockSpec((1,) + shape, lambda b, i: (b,) + (0,) * len(shape))
    return pl.pallas_call(
        functools.partial(_attn_kernel, seq=seq),
        grid=(bh, seq // Q_BLK),
        in_specs=[pl.BlockSpec((1, GQA, HEAD_DIM, Q_BLK), lambda b, i: (b, 0, 0, i)),
                  per_bh((n_cmp, HEAD_DIM)), per_bh((V_ROWS, n_cmp)),
                  per_bh((seq, HEAD_DIM)), per_bh((V_ROWS, seq)),
                  per_bh((seq, HEAD_DIM)), per_bh((V_ROWS, seq)),
                  pl.BlockSpec((1, GQA, 3, Q_BLK), lambda b, i: (b, 0, 0, i)),
                  pl.BlockSpec((n_sel, n_cmp), lambda b, i: (0, 0)),
                  pl.BlockSpec((n_cmp, Q_BLK), lambda b, i: (0, 0))],
        out_specs=pl.BlockSpec((1, Q_BLK, GQA * HEAD_DIM), lambda b, i: (b // N_KV, i, b % N_KV)),
        out_shape=jax.ShapeDtypeStruct((bsz, seq, N_KV * GQA * HEAD_DIM), F32),
        scratch_shapes=[pltpu.VMEM((n_sel, Q_BLK), F32), pltpu.VMEM((SEL_AHEAD, SEL_SUB, GQA * Q_BLK), F32)],
        compiler_params=_params("parallel", "arbitrary"),
        name="nsa_attention",
    )(qt, kc, vct, ks, vst, kw, vwt, gt, ovt, cthr.astype(jnp.int32))


def _ssm_kernel(u_ref, pt_ref, bq_ref, w_ref, v_ref, a1_ref, a2_ref, o_ref, mt_ref, *, chunks_per_seq):
    width = SSM_CHUNK * SSM_CH
    u = u_ref[0]
    kt = _dot_split(bq_ref[0], pt_ref[0])
    col = lax.broadcasted_iota(jnp.int32, (SSM_CH, width), 1)
    for s in range(SSM_CHUNK):
        shifted = kt if s == 0 else pltpu.roll(kt, SSM_CH * s, axis=1)
        mt_ref[s * SSM_CH:(s + 1) * SSM_CH, :] = jnp.where(col >= SSM_CH * s, shifted, 0.0).astype(BF16)
    y = _dot(u, mt_ref[...])
    x = _dot(u, w_ref[0])
    row = lax.broadcasted_iota(jnp.int32, x.shape, 0) % chunks_per_seq
    steps = int(math.log2(chunks_per_seq))
    for j in range(steps):
        d = 1 << j
        sh = jnp.where(row >= d, pltpu.roll(x, d, axis=0), 0.0)
        x = x + a1_ref[0, j:j + 1, :] * sh + a2_ref[0, j:j + 1, :] * pltpu.roll(sh, SSM_STATE, axis=1)
    prev = jnp.where(row >= 1, pltpu.roll(x, 1, axis=0), 0.0)
    o_ref[0] = y + _dot(prev.astype(BF16), v_ref[0])


def s5_scan(u2, pt, bq, w, v, a1, a2, chunks_per_seq):
    g, n_chunks, width = u2.shape
    per_g = lambda a: pl.BlockSpec((1,) + a.shape[1:], lambda i: (i,) + (0,) * (a.ndim - 1))
    return pl.pallas_call(
        functools.partial(_ssm_kernel, chunks_per_seq=chunks_per_seq),
        grid=(g,),
        in_specs=[per_g(u2), per_g(pt), per_g(bq), per_g(w), per_g(v), per_g(a1), per_g(a2)],
        out_specs=pl.BlockSpec((1, n_chunks, width), lambda i: (i, 0, 0)),
        out_shape=jax.ShapeDtypeStruct((g, n_chunks, width), F32),
        scratch_shapes=[pltpu.VMEM((width, width), BF16)],
        compiler_params=_params("parallel"),
        name="s5_scan",
    )(u2, pt, bq, w, v, a1, a2)


def _s5_tables(log_dt, a_re, a_im, b_re, b_im, c_re, c_im, chunks_per_seq):
    g, n = a_re.shape
    dt = jnp.exp(log_dt)[:, None]
    lam_re, lam_im = dt * a_re, dt * a_im

    def power(k):
        k = k.astype(F32)[None, :, None]
        mag = jnp.exp(k * lam_re[:, None, :])
        return mag * jnp.cos(k * lam_im[:, None, :]), mag * jnp.sin(k * lam_im[:, None, :])

    abar_re, abar_im = (x[:, 0] for x in power(jnp.ones((1,))))
    den = a_re * a_re + a_im * a_im
    nr = abar_re - 1.0
    f_re = (nr * a_re + abar_im * a_im) / den
    f_im = (abar_im * a_re - nr * a_im) / den
    bb_re = f_re[..., None] * b_re - f_im[..., None] * b_im
    bb_im = f_re[..., None] * b_im + f_im[..., None] * b_re
    steps = jnp.arange(SSM_CHUNK)
    width = SSM_CHUNK * SSM_CH
    e_re, e_im = power(steps)
    ca_re = c_re[:, None] * e_re[:, :, None, :] - c_im[:, None] * e_im[:, :, None, :]
    ca_im = c_re[:, None] * e_im[:, :, None, :] + c_im[:, None] * e_re[:, :, None, :]
    pt = jnp.concatenate([ca_re, -ca_im], axis=-1).reshape(g, width, 2 * n).transpose(0, 2, 1)
    bq = jnp.concatenate([bb_re, bb_im], axis=1).transpose(0, 2, 1)
    r_re, r_im = power(SSM_CHUNK - 1 - steps)
    bt_re, bt_im = bb_re.transpose(0, 2, 1)[:, None], bb_im.transpose(0, 2, 1)[:, None]
    w_re = r_re[:, :, None, :] * bt_re - r_im[:, :, None, :] * bt_im
    w_im = r_re[:, :, None, :] * bt_im + r_im[:, :, None, :] * bt_re
    w = jnp.concatenate([w_re, w_im], axis=-1).reshape(g, width, 2 * n).astype(BF16)
    n_re, n_im = power(steps + 1)
    cv_re = c_re[:, None] * n_re[:, :, None, :] - c_im[:, None] * n_im[:, :, None, :]
    cv_im = c_re[:, None] * n_im[:, :, None, :] + c_im[:, None] * n_re[:, :, None, :]
    v = jnp.concatenate([cv_re, -cv_im], axis=-1).reshape(g, width, 2 * n).transpose(0, 2, 1).astype(BF16)
    n_steps = int(math.log2(chunks_per_seq))
    s_re, s_im = power(SSM_CHUNK * (2 ** jnp.arange(max(n_steps, 1))))
    a1 = jnp.concatenate([s_re, s_re], axis=-1)
    a2 = jnp.concatenate([-s_im, s_im], axis=-1)
    return pt, bq, w, v, a1, a2


def _mix_out_kernel(x_ref, oa_ref, ys_ref, u_ref, dsk_ref, wglu_ref, bglu_ref, na_ref, ns_ref,
                    woa_ref, wos_ref, o_ref):
    d_ssm = u_ref.shape[1]
    y = jax.nn.gelu(ys_ref[...] + dsk_ref[...] * u_ref[...])
    z = _dot(y.astype(BF16), wglu_ref[...]) + bglu_ref[...]
    o_ssm = z[:, :d_ssm] * jax.nn.sigmoid(z[:, d_ssm:])
    att_n = _rms(oa_ref[...], na_ref[...]).astype(BF16)
    ssm_n = _rms(o_ssm, ns_ref[...]).astype(BF16)
    o_ref[...] = x_ref[...] + _dot(att_n, woa_ref[...]) + _dot(ssm_n, wos_ref[...])


def mix_out(x2, o_att, y_ssm, u, d_skip, w_glu, b_glu, n_att, n_ssm, w_out):
    t, d = x2.shape
    d_att = o_att.shape[1]
    d_ssm = u.shape[1]
    tm = _row_tile(t)
    rows = lambda w: pl.BlockSpec((tm, w), lambda i: (i, 0))
    full = lambda a: pl.BlockSpec(a.shape, lambda i: (0,) * a.ndim)
    consts = [d_skip.reshape(1, d_ssm), w_glu.astype(BF16), b_glu.reshape(1, 2 * d_ssm),
              n_att.reshape(1, d_att), n_ssm.reshape(1, d_ssm),
              w_out[:d_att].astype(BF16), w_out[d_att:].astype(BF16)]
    return pl.pallas_call(
        _mix_out_kernel,
        grid=(t // tm,),
        in_specs=[rows(d), rows(d_att), rows(d_ssm), rows(d_ssm)] + [full(c) for c in consts],
        out_specs=rows(d),
        out_shape=jax.ShapeDtypeStruct((t, d), F32),
        compiler_params=_params("parallel"),
        name="mix_out",
    )(x2, o_att, y_ssm, u, *consts)


def _ffn_kernel(x_ref, g_ref, wg_ref, wu_ref, wd_ref, fg_ref, o_ref, acc_ref, *, f_chunk, final_norm):
    x = x_ref[...]
    h = _rms(x, g_ref[...]).astype(BF16)
    d_ff = wg_ref.shape[1]
    acc_ref[...] = x
    for c in range(d_ff // f_chunk):
        sl = slice(c * f_chunk, (c + 1) * f_chunk)
        a = jax.nn.silu(_dot(h, wg_ref[:, sl])) * _dot(h, wu_ref[:, sl])
        acc_ref[...] += _dot(a.astype(BF16), wd_ref[sl, :])
    out = acc_ref[...]
    if final_norm:
        out = _rms(out, fg_ref[...])
    o_ref[...] = out


def ffn(x2, g, w_gate, w_up, w_down, final_g, final_norm):
    t, d = x2.shape
    d_ff = w_gate.shape[1]
    tm = _row_tile(t)
    f_chunk = 256 if d_ff % 256 == 0 else d_ff
    full = lambda a: pl.BlockSpec(a.shape, lambda i: (0,) * a.ndim)
    consts = [g.reshape(1, d), w_gate.astype(BF16), w_up.astype(BF16), w_down.astype(BF16),
              final_g.reshape(1, d)]
    return pl.pallas_call(
        functools.partial(_ffn_kernel, f_chunk=f_chunk, final_norm=final_norm),
        grid=(t // tm,),
        in_specs=[pl.BlockSpec((tm, d), lambda i: (i, 0))] + [full(c) for c in consts],
        out_specs=pl.BlockSpec((tm, d), lambda i: (i, 0)),
        out_shape=jax.ShapeDtypeStruct((t, d), F32),
        scratch_shapes=[pltpu.VMEM((tm, d), F32)],
        compiler_params=_params("parallel"),
        name="ffn",
    )(x2, *consts)


def _mixer(x2, bsz, seq, attn_norm_g, w_in, cmp_pe, cmp_w1, cmp_b1, cmp_w2, cmp_b2, log_dt, a_re, a_im,
           b_re, b_im, c_re, c_im, d_skip, w_glu, b_glu, mix_norm_att, mix_norm_ssm, w_out):
    t, d = x2.shape
    d_ssm = d - D_ATT
    bh = bsz * N_KV
    qt, vst, vwt, gt, kc, vc, ks, kw, u = in_proj(x2.reshape(bsz, seq, d), attn_norm_g, w_in)
    k_c = compress(kc, cmp_pe[0], cmp_w1[0], cmp_b1[0], cmp_w2[0], cmp_b2[0], transposed=False)
    v_ct = compress(vc, cmp_pe[1], cmp_w1[1], cmp_b1[1], cmp_w2[1], cmp_b2[1], transposed=True)
    o_att = nsa_attention(qt.reshape(bh, GQA, HEAD_DIM, seq), k_c, v_ct,
                          ks.reshape(bh, seq, HEAD_DIM), vst.reshape(bh, V_ROWS, seq),
                          kw.reshape(bh, seq, HEAD_DIM), vwt.reshape(bh, V_ROWS, seq),
                          gt.reshape(bh, GQA, 3, seq), bsz).reshape(t, D_ATT)

    u = u.reshape(t, d_ssm)
    n_groups = d_ssm // SSM_CH
    chunks_per_seq = seq // SSM_CHUNK
    n_chunks = t // SSM_CHUNK
    width = SSM_CHUNK * SSM_CH
    u2 = u.astype(BF16).reshape(n_chunks, SSM_CHUNK, n_groups, SSM_CH).transpose(2, 0, 1, 3)
    tables = _s5_tables(log_dt, a_re, a_im, b_re, b_im, c_re, c_im, chunks_per_seq)
    y2 = s5_scan(u2.reshape(n_groups, n_chunks, width), *tables, chunks_per_seq)
    y_ssm = y2.reshape(n_groups, n_chunks, SSM_CHUNK, SSM_CH).transpose(1, 2, 0, 3).reshape(t, d_ssm)

    return mix_out(x2, o_att, y_ssm, u, d_skip, w_glu, b_glu, mix_norm_att, mix_norm_ssm, w_out)


def kernel(x, attn_norm_g, w_in, cmp_pe, cmp_w1, cmp_b1, cmp_w2, cmp_b2, log_dt, a_re, a_im, b_re, b_im, c_re, c_im, d_skip, w_glu, b_glu, mix_norm_att, mix_norm_ssm, w_out, ffn_norm_g, w_gate, w_up, w_down, final_norm_g):
    bsz, seq, d = x.shape
    depth = w_in.shape[0]
    x2 = x.reshape(bsz * seq, d)
    for l in range(depth):
        x2 = _mixer(x2, bsz, seq, attn_norm_g[l], w_in[l], cmp_pe[l], cmp_w1[l], cmp_b1[l], cmp_w2[l],
                    cmp_b2[l], log_dt[l], a_re[l], a_im[l], b_re[l], b_im[l], c_re[l], c_im[l], d_skip[l],
                    w_glu[l], b_glu[l], mix_norm_att[l], mix_norm_ssm[l], w_out[l])
        x2 = ffn(x2, ffn_norm_g[l], w_gate[l], w_up[l], w_down[l], final_norm_g, final_norm=(l == depth - 1))
    return x2.reshape(bsz, seq, d)
```

```python
import functools
import math

import jax
import jax.numpy as jnp
from jax import lax
from jax.experimental import pallas as pl
from jax.experimental.pallas import tpu as pltpu

HEAD_DIM = 64
N_KV = 3
GQA = 4
N_HEADS = N_KV * GQA
D_ATT = N_HEADS * HEAD_DIM
SSM_CH = 16
SSM_STATE = 64
L_CMP = 32
STRIDE_CMP = 16
CMP_HID = 256
L_SEL = 64
TOP_K = 16
N_FORCED = 3
WINDOW = 512
Q_BLK = 128
BIG = 1e9
TINY = 1e-30
EPS = 1e-6
LOG2E = 1.4426950408889634

SUBLANES = 8
V_ROWS = HEAD_DIM + 2 * SUBLANES
SSM_CHUNK = 64
SEL_SUB = 256
SEL_LOOP = 32
SEL_DIAG = 8
FRONT_PARTS = 8
SEL_AHEAD = 3
VMEM_LIMIT = 56 * 1024 * 1024

F32 = jnp.float32
BF16 = jnp.bfloat16
NEG_INF = float("-inf")


def _dot(a, b):
    return jnp.dot(a, b, preferred_element_type=F32)


def _dot_nt(a, b):
    return lax.dot_general(a, b, (((1,), (1,)), ((), ())), preferred_element_type=F32)


def _dot_split(a, b):
    a_hi = a.astype(BF16)
    a_lo = (a - a_hi.astype(F32)).astype(BF16)
    b_hi = b.astype(BF16)
    b_lo = (b - b_hi.astype(F32)).astype(BF16)
    return _dot(a_hi, b_hi) + _dot(a_hi, b_lo) + _dot(a_lo, b_hi)


def _params(*sem):
    return pltpu.CompilerParams(dimension_semantics=sem, vmem_limit_bytes=VMEM_LIMIT)


def _row_tile(t):
    for tm in (512, 256, 128, 64, 32, 16, 8):
        if t % tm == 0:
            return tm
    raise ValueError(f"token count {t} must be a multiple of 8")


def _rms(x, g):
    ms = jnp.mean(x * x, axis=-1, keepdims=True)
    return x * lax.rsqrt(ms + EPS) * g


def _in_proj_kernel(x_ref, g_ref, wn_ref, wt_ref, qt_ref, vst_ref, vwt_ref, gt_ref, kc_ref, vc_ref,
                    ks_ref, kw_ref, u_ref):
    p_kv = N_KV * HEAD_DIM
    h = _rms(x_ref[0], g_ref[...]).astype(BF16)
    tm = h.shape[0]
    nat = _dot(h, wn_ref[...])
    tr = _dot_nt(wt_ref[...], h)
    qt_ref[0] = (tr[:D_ATT] * (LOG2E * HEAD_DIM ** -0.5)).astype(BF16)
    ones_rows = (lax.broadcasted_iota(jnp.int32, (V_ROWS - HEAD_DIM, tm), 0) == 0).astype(BF16)
    for hd in range(N_KV):
        rows = slice(hd * HEAD_DIM, (hd + 1) * HEAD_DIM)
        for k, ref in enumerate((vst_ref, vwt_ref)):
            off = D_ATT + k * p_kv
            ref[0, hd, :HEAD_DIM, :] = tr[off + hd * HEAD_DIM:off + (hd + 1) * HEAD_DIM].astype(BF16)
            ref[0, hd, HEAD_DIM:, :] = ones_rows
        kc_ref[0, hd] = nat[:, rows]
        vc_ref[0, hd] = nat[:, p_kv + hd * HEAD_DIM:p_kv + (hd + 1) * HEAD_DIM]
        ks_ref[0, hd] = nat[:, 2 * p_kv + hd * HEAD_DIM:2 * p_kv + (hd + 1) * HEAD_DIM].astype(BF16)
        kw_ref[0, hd] = nat[:, 3 * p_kv + hd * HEAD_DIM:3 * p_kv + (hd + 1) * HEAD_DIM].astype(BF16)
    gt_ref[0] = tr[D_ATT + 2 * p_kv:D_ATT + 2 * p_kv + 3 * N_HEADS]
    u_ref[0] = nat[:, 4 * p_kv:]


def in_proj(x3, g, w_in):
    bsz, seq, d = x3.shape
    p_kv = N_KV * HEAD_DIM
    p_gate = 3 * N_HEADS
    d_ssm = d - D_ATT
    o = [0, D_ATT]
    for width in [p_kv] * 6 + [p_gate, d_ssm]:
        o.append(o[-1] + width)
    q, kc, vc, ksl, vsl, kwn, vwn, gl, u = (w_in[:, o[i]:o[i + 1]] for i in range(9))
    w_nat = jnp.concatenate([kc, vc, ksl, kwn, u], axis=1).astype(BF16)
    n_tr = D_ATT + 2 * p_kv + p_gate
    w_tr = jnp.pad(jnp.concatenate([q, vsl, vwn, gl], axis=1), ((0, 0), (0, -n_tr % SUBLANES))).T.astype(BF16)
    tm = _row_tile(seq)
    full = lambda a: pl.BlockSpec(a.shape, lambda b, i: (0,) * a.ndim)
    feat = lambda rows: pl.BlockSpec((1, rows, tm), lambda b, i: (b, 0, i))
    vals = pl.BlockSpec((1, N_KV, V_ROWS, tm), lambda b, i: (b, 0, 0, i))
    keys = pl.BlockSpec((1, N_KV, tm, HEAD_DIM), lambda b, i: (b, 0, i, 0))
    sds = jax.ShapeDtypeStruct
    g2 = g.reshape(1, d)
    return pl.pallas_call(
        _in_proj_kernel,
        grid=(bsz, seq // tm),
        in_specs=[pl.BlockSpec((1, tm, d), lambda b, i: (b, i, 0)), full(g2), full(w_nat), full(w_tr)],
        out_specs=[feat(D_ATT), vals, vals, feat(p_gate), keys, keys, keys, keys,
                   pl.BlockSpec((1, tm, d_ssm), lambda b, i: (b, i, 0))],
        out_shape=[sds((bsz, D_ATT, seq), BF16), sds((bsz, N_KV, V_ROWS, seq), BF16),
                   sds((bsz, N_KV, V_ROWS, seq), BF16), sds((bsz, p_gate, seq), F32),
                   sds((bsz, N_KV, seq, HEAD_DIM), F32), sds((bsz, N_KV, seq, HEAD_DIM), F32),
                   sds((bsz, N_KV, seq, HEAD_DIM), BF16), sds((bsz, N_KV, seq, HEAD_DIM), BF16),
                   sds((bsz, seq, d_ssm), F32)],
        compiler_params=_params("parallel", "parallel"),
        name="in_proj",
    )(x3, g2, w_nat, w_tr)


def _compress_kernel(x_ref, pe_ref, w1_ref, b1_ref, w2_ref, b2_ref, o_ref, *, transposed):
    seq = x_ref.shape[2]
    n = seq // STRIDE_CMP
    first = jnp.zeros((n, CMP_HID), F32)
    second = jnp.zeros((n, CMP_HID), F32)
    for j in range(STRIDE_CMP):
        tok = x_ref[0, 0, pl.ds(j, n, stride=STRIDE_CMP), :]
        w_a = w1_ref[j * HEAD_DIM:(j + 1) * HEAD_DIM, :]
        w_b = w1_ref[(STRIDE_CMP + j) * HEAD_DIM:(STRIDE_CMP + j + 1) * HEAD_DIM, :]
        first = first + _dot((tok + pe_ref[j:j + 1, :]).astype(BF16), w_a)
        second = second + _dot((tok + pe_ref[STRIDE_CMP + j:STRIDE_CMP + j + 1, :]).astype(BF16), w_b)
    hid = first + pltpu.roll(second, n - 1, axis=0) + b1_ref[...]
    hid = jax.nn.gelu(hid).astype(BF16)
    if transposed:
        o_ref[0] = (_dot_nt(w2_ref[...], hid) + b2_ref[...]).astype(o_ref.dtype)
    else:
        o_ref[0] = (_dot(hid, w2_ref[...]) + b2_ref[...]).astype(o_ref.dtype)


def compress(x, pe, w1, b1, w2, b2, transposed):
    bsz, n_kv, seq, _ = x.shape
    bh = bsz * n_kv
    n = seq // STRIDE_CMP
    w1k = w1.astype(BF16)
    if transposed:
        pad = V_ROWS - HEAD_DIM
        w2k = jnp.pad(w2.T, ((0, pad), (0, 0))).astype(BF16)
        b2k = jnp.concatenate([b2, jnp.ones((1,), F32), jnp.zeros((pad - 1,), F32)]).reshape(V_ROWS, 1)
        out_block, out_shape = (1, V_ROWS, n), (bh, V_ROWS, n)
    else:
        w2k = w2.astype(BF16)
        b2k = b2.reshape(1, HEAD_DIM)
        out_block, out_shape = (1, n, HEAD_DIM), (bh, n, HEAD_DIM)
    full = lambda a: pl.BlockSpec(a.shape, lambda i: (0,) * a.ndim)
    b1k = b1.reshape(1, CMP_HID)
    return pl.pallas_call(
        functools.partial(_compress_kernel, transposed=transposed),
        grid=(bh,),
        in_specs=[pl.BlockSpec((1, 1, seq, HEAD_DIM), lambda i: (i // n_kv, i % n_kv, 0, 0)),
                  full(pe), full(w1k), full(b1k), full(w2k), full(b2k)],
        out_specs=pl.BlockSpec(out_block, lambda i: (i, 0, 0)),
        out_shape=jax.ShapeDtypeStruct(out_shape, BF16),
        compiler_params=_params("parallel"),
        name="compress_t" if transposed else "compress_n",
    )(x, pe, w1k, b1k, w2k, b2k)


def _heads(x):
    return [x[:, g * Q_BLK:(g + 1) * Q_BLK] for g in range(GQA)]


def _exp2_safe_max(m):
    return jnp.where(m == NEG_INF, 0.0, m)


def _aligned(x, m):
    return x if isinstance(x, int) else pl.multiple_of(x, m)


def _normalise(pv):
    return pv[:HEAD_DIM] / jnp.maximum(pv[HEAD_DIM:HEAD_DIM + 1], TINY)


def _attn_kernel(qt_ref, kc_ref, vct_ref, ks_ref, vst_ref, kw_ref, vwt_ref, gt_ref, ovt_ref, cthr_ref,
                 o_ref, selb_ref, s_ref, *, seq):
    qi = pl.program_id(1)
    n_sel = seq // L_SEL
    lanes = GQA * Q_BLK
    win_keys = WINDOW + Q_BLK

    qt = jnp.concatenate([qt_ref[0, g] for g in range(GQA)], axis=1)
    q0 = qi * Q_BLK
    lane_q = lax.broadcasted_iota(jnp.int32, (1, Q_BLK), 1)
    t_q = q0 + lane_q
    t_all = jnp.concatenate([t_q] * GQA, axis=1)

    def sel_scores(k0):
        return _dot(ks_ref[0, pl.ds(k0, SEL_SUB), :], qt)

    def front(n_rows, full_band, n_static):
        n_blk = n_rows * STRIDE_CMP // L_SEL
        s_c = _dot(kc_ref[0, :n_rows, :], qt)
        for a in range(SEL_AHEAD):
            s_ref[a] = sel_scores(a * SEL_SUB)
        vis = cthr_ref[:n_rows, :] <= q0
        p_heads = []
        for s_g in _heads(s_c):
            s_g = jnp.where(vis, s_g, NEG_INF)
            m_g = _exp2_safe_max(jnp.max(s_g, axis=0, keepdims=True))
            p_heads.append(jnp.exp2(s_g - m_g).astype(BF16))
        p_c = jnp.concatenate(p_heads, axis=1)
        pv_c = _dot(vct_ref[0, :, :n_rows], p_c)
        r_c = 1.0 / jnp.maximum(pv_c[HEAD_DIM:HEAD_DIM + 1], TINY)
        o_c = pv_c[:HEAD_DIM] * r_c
        imp_all = _dot(ovt_ref[:n_blk, :n_rows], p_c) * r_c
        imp = functools.reduce(jnp.add, _heads(imp_all))

        blk = lax.broadcasted_iota(jnp.int32, (n_blk, Q_BLK), 0).astype(F32)
        cur = (t_q // L_SEL).astype(F32)
        forced = (blk == 0.0) | (blk == cur) | (blk == cur - 1.0)
        free = (blk <= cur) & jnp.logical_not(forced)
        cand = jnp.where(free, imp, NEG_INF)
        n_pick = min(TOP_K, n_sel) - N_FORCED

        def rank_exact(val):
            for _ in range(n_pick):
                mx = jnp.max(val, axis=0, keepdims=True)
                first = jnp.min(jnp.where(val == mx, blk, float(n_blk)), axis=0, keepdims=True)
                val = jnp.where(blk == first, NEG_INF, val)
            return val

        val_fast = cand
        for _ in range(n_pick):
            val_fast = jnp.where(val_fast == jnp.max(val_fast, axis=0, keepdims=True), NEG_INF, val_fast)
        n_free = jnp.sum(free.astype(F32), axis=0, keepdims=True)
        n_got = jnp.sum((free & (val_fast == NEG_INF)).astype(F32), axis=0, keepdims=True)
        tied = jnp.max(jnp.where(n_got != jnp.minimum(n_free, float(n_pick)), 1.0, 0.0)) > 0.0

        start = pl.multiple_of(jnp.maximum(q0 - WINDOW, 0), Q_BLK)
        s_w = _dot(kw_ref[0, pl.ds(start, win_keys), :], qt)
        r_minus_c = (lax.broadcasted_iota(jnp.int32, (Q_BLK, Q_BLK), 0)
                     - lax.broadcasted_iota(jnp.int32, (Q_BLK, Q_BLK), 1))
        chunks = []
        n_chunks = win_keys // Q_BLK
        for i in range(n_chunks):
            rows = s_w[i * Q_BLK:(i + 1) * Q_BLK]
            if not full_band or i in (0, n_chunks - 1):
                hi = q0 - start - i * Q_BLK
                mask = (r_minus_c <= hi) & (r_minus_c > hi - WINDOW)
                rows = jnp.concatenate([jnp.where(mask, x, NEG_INF) for x in _heads(rows)], axis=1)
            chunks.append(rows)
        m_w = _exp2_safe_max(functools.reduce(jnp.maximum, [jnp.max(x, axis=0, keepdims=True) for x in chunks]))
        p_w = jnp.concatenate([jnp.exp2(x - m_w).astype(BF16) for x in chunks], axis=0)
        o_w = _normalise(_dot(vwt_ref[0, :, pl.ds(start, win_keys)], p_w))

        val = lax.cond(tied, lambda: rank_exact(cand), lambda: val_fast)
        selb_ref[:n_blk, :] = jnp.where(forced | (free & (val == NEG_INF)), 0.0, NEG_INF)
        carry = (jnp.full((1, lanes), NEG_INF, F32), jnp.zeros((V_ROWS, lanes), F32))
        if n_static:
            carry = sel_run(0, *carry, n_static, False)
        return (o_c, o_w) + tuple(carry)

    row_iota = lax.broadcasted_iota(jnp.int32, (L_SEL, lanes), 0)

    def sel_update(s_t, k0, m_prev, acc_prev, causal):
        blk0 = k0 // L_SEL
        blocks, biases, part_max = [], [], None
        for j in range(SEL_SUB // L_SEL):
            bias = jnp.concatenate([selb_ref[pl.ds(blk0 + j, 1), :]] * GQA, axis=1)
            s_b = s_t[j * L_SEL:(j + 1) * L_SEL]
            if causal:
                s_b = jnp.where(k0 + j * L_SEL + row_iota <= t_all, s_b, NEG_INF)
            s8 = functools.reduce(jnp.maximum, [s_b[r:r + SUBLANES] for r in range(0, L_SEL, SUBLANES)])
            cand = s8 + bias
            part_max = cand if part_max is None else jnp.maximum(part_max, cand)
            blocks.append(s_b)
            biases.append(bias)
        m_new = jnp.maximum(m_prev, jnp.max(part_max, axis=0, keepdims=True))
        m_safe = _exp2_safe_max(m_new)
        alpha = jnp.exp2(m_prev - m_safe)
        p = jnp.concatenate([jnp.exp2(s_b - (m_safe - bias)).astype(BF16)
                             for s_b, bias in zip(blocks, biases)], axis=0)
        pv = _dot(vst_ref[0, :, pl.ds(k0, SEL_SUB)], p)
        return m_new, alpha * acc_prev + pv

    def sel_run(k0, m_prev, acc_prev, n_sub, diagonal):
        queue = [s_ref[a] for a in range(min(SEL_AHEAD, n_sub) if diagonal else SEL_AHEAD)]
        for i in range(n_sub):
            ahead = i + SEL_AHEAD
            if not diagonal or ahead < n_sub:
                queue.append(sel_scores(_aligned(k0 + ahead * SEL_SUB, SEL_SUB)))
            m_prev, acc_prev = sel_update(queue.pop(0), _aligned(k0 + i * SEL_SUB, SEL_SUB),
                                          m_prev, acc_prev, causal=diagonal and i == n_sub - 1)
        if not diagonal:
            for a in range(SEL_AHEAD):
                s_ref[a] = queue[a]
        return m_prev, acc_prev

    n_cmp = seq // STRIDE_CMP
    parts = FRONT_PARTS if n_cmp % (FRONT_PARTS * Q_BLK) == 0 else 1
    part_subs = seq // parts // SEL_SUB
    part = q0 // (seq // parts)
    o_c, o_w, *carry = lax.switch(
        part, [functools.partial(front, n_cmp * (k + 1) // parts, k * (seq // parts) >= WINDOW, k * part_subs)
               for k in range(parts)])

    n_below = q0 // SEL_SUB
    done = part * part_subs
    if part_subs > SEL_LOOP:
        trips = (n_below - done) // SEL_LOOP
        carry = lax.fori_loop(
            0, trips, lambda kt, c: sel_run(_aligned((done + kt * SEL_LOOP) * SEL_SUB, SEL_SUB), *c, SEL_LOOP, False),
            tuple(carry))
        done = done + trips * SEL_LOOP
    run = min(SEL_LOOP, part_subs) // 2
    while run >= SEL_DIAG:
        take = ((n_below - done) // run) == 1
        k_run = _aligned(done * SEL_SUB, SEL_SUB)
        carry = lax.cond(take, functools.partial(lambda k, n, m, a: sel_run(k, m, a, n, False), k_run, run),
                         lambda m, a: (m, a), *carry)
        done = done + jnp.where(take, run, 0)
        run //= 2
    k_last = _aligned(done * SEL_SUB, SEL_SUB)
    acc_s = lax.switch(
        n_below - done,
        [functools.partial(lambda n, m, a: sel_run(k_last, m, a, n + 1, True)[1], n) for n in range(SEL_DIAG)],
        *carry)
    o_s = _normalise(acc_s)

    outs = []
    for g in range(GQA):
        sl = slice(g * Q_BLK, (g + 1) * Q_BLK)
        gate = jax.nn.sigmoid(gt_ref[0, g])
        outs.append(gate[0:1] * o_c[:, sl] + gate[1:2] * o_s[:, sl] + gate[2:3] * o_w[:, sl])
    pairs = [jnp.concatenate(outs[g:g + 2], axis=0).T for g in range(0, GQA, 2)]
    o_ref[0] = jnp.concatenate(pairs, axis=1)


def nsa_attention(qt, kc, vct, ks, vst, kw, vwt, gt, bsz):
    bh, _, _, seq = qt.shape
    assert SEL_LOOP % SEL_DIAG == 0 and seq % (SEL_DIAG * SEL_SUB) == 0 and seq >= WINDOW + Q_BLK
    n_cmp = seq // STRIDE_CMP
    n_sel = seq // L_SEL
    cmp_start = jnp.arange(n_cmp) * STRIDE_CMP
    sel_start = jnp.arange(n_sel) * L_SEL
    ovt = ((cmp_start[None, :] < sel_start[:, None] + L_SEL)
           & (cmp_start[None, :] + L_CMP > sel_start[:, None])).astype(BF16)
    cthr = (cmp_start + L_CMP - 1)[:, None] - jnp.arange(Q_BLK)[None, :]
    per_bh = lambda shape: pl.BlockSpec((1,) + shape, lambda b, i: (b,) + (0,) * len(shape))
    return pl.pallas_call(
        functools.partial(_attn_kernel, seq=seq),
        grid=(bh, seq // Q_BLK),
        in_specs=[pl.BlockSpec((1, GQA, HEAD_DIM, Q_BLK), lambda b, i: (b, 0, 0, i)),
                  per_bh((n_cmp, HEAD_DIM)), per_bh((V_ROWS, n_cmp)),
                  per_bh((seq, HEAD_DIM)), per_bh((V_ROWS, seq)),
                  per_bh((seq, HEAD_DIM)), per_bh((V_ROWS, seq)),
                  pl.BlockSpec((1, GQA, 3, Q_BLK), lambda b, i: (b, 0, 0, i)),
                  pl.BlockSpec((n_sel, n_cmp), lambda b, i: (0, 0)),
                  pl.BlockSpec((n_cmp, Q_BLK), lambda b, i: (0, 0))],
        out_specs=pl.BlockSpec((1, Q_BLK, GQA * HEAD_DIM), lambda b, i: (b // N_KV, i, b % N_KV)),
        out_shape=jax.ShapeDtypeStruct((bsz, seq, N_KV * GQA * HEAD_DIM), F32),
        scratch_shapes=[pltpu.VMEM((n_sel, Q_BLK), F32), pltpu.VMEM((SEL_AHEAD, SEL_SUB, GQA * Q_BLK), F32)],
        compiler_params=_params("parallel", "arbitrary"),
        name="nsa_attention",
    )(qt, kc, vct, ks, vst, kw, vwt, gt, ovt, cthr.astype(jnp.int32))


def _ssm_kernel(u_ref, pt_ref, bq_ref, w_ref, v_ref, a1_ref, a2_ref, o_ref, mt_ref, *, chunks_per_seq):
    width = SSM_CHUNK * SSM_CH
    u = u_ref[0]
    kt = _dot_split(bq_ref[0], pt_ref[0])
    col = lax.broadcasted_iota(jnp.int32, (SSM_CH, width), 1)
    for s in range(SSM_CHUNK):
        shifted = kt if s == 0 else pltpu.roll(kt, SSM_CH * s, axis=1)
        mt_ref[s * SSM_CH:(s + 1) * SSM_CH, :] = jnp.where(col >= SSM_CH * s, shifted, 0.0).astype(BF16)
    y = _dot(u, mt_ref[...])
    x = _dot(u, w_ref[0])
    row = lax.broadcasted_iota(jnp.int32, x.shape, 0) % chunks_per_seq
    steps = int(math.log2(chunks_per_seq))
    for j in range(steps):
        d = 1 << j
        sh = jnp.where(row >= d, pltpu.roll(x, d, axis=0), 0.0)
        x = x + a1_ref[0, j:j + 1, :] * sh + a2_ref[0, j:j + 1, :] * pltpu.roll(sh, SSM_STATE, axis=1)
    prev = jnp.where(row >= 1, pltpu.roll(x, 1, axis=0), 0.0)
    o_ref[0] = y + _dot(prev.astype(BF16), v_ref[0])


def s5_scan(u2, pt, bq, w, v, a1, a2, chunks_per_seq):
    g, n_chunks, width = u2.shape
    per_g = lambda a: pl.BlockSpec((1,) + a.shape[1:], lambda i: (i,) + (0,) * (a.ndim - 1))
    return pl.pallas_call(
        functools.partial(_ssm_kernel, chunks_per_seq=chunks_per_seq),
        grid=(g,),
        in_specs=[per_g(u2), per_g(pt), per_g(bq), per_g(w), per_g(v), per_g(a1), per_g(a2)],
        out_specs=pl.BlockSpec((1, n_chunks, width), lambda i: (i, 0, 0)),
        out_shape=jax.ShapeDtypeStruct((g, n_chunks, width), F32),
        scratch_shapes=[pltpu.VMEM((width, width), BF16)],
        compiler_params=_params("parallel"),
        name="s5_scan",
    )(u2, pt, bq, w, v, a1, a2)


def _s5_tables(log_dt, a_re, a_im, b_re, b_im, c_re, c_im, chunks_per_seq):
    g, n = a_re.shape
    dt = jnp.exp(log_dt)[:, None]
    lam_re, lam_im = dt * a_re, dt * a_im

    def power(k):
        k = k.astype(F32)[None, :, None]
        mag = jnp.exp(k * lam_re[:, None, :])
        return mag * jnp.cos(k * lam_im[:, None, :]), mag * jnp.sin(k * lam_im[:, None, :])

    abar_re, abar_im = (x[:, 0] for x in power(jnp.ones((1,))))
    den = a_re * a_re + a_im * a_im
    nr = abar_re - 1.0
    f_re = (nr * a_re + abar_im * a_im) / den
    f_im = (abar_im * a_re - nr * a_im) / den
    bb_re = f_re[..., None] * b_re - f_im[..., None] * b_im
    bb_im = f_re[..., None] * b_im + f_im[..., None] * b_re
    steps = jnp.arange(SSM_CHUNK)
    width = SSM_CHUNK * SSM_CH
    e_re, e_im = power(steps)
    ca_re = c_re[:, None] * e_re[:, :, None, :] - c_im[:, None] * e_im[:, :, None, :]
    ca_im = c_re[:, None] * e_im[:, :, None, :] + c_im[:, None] * e_re[:, :, None, :]
    pt = jnp.concatenate([ca_re, -ca_im], axis=-1).reshape(g, width, 2 * n).transpose(0, 2, 1)
    bq = jnp.concatenate([bb_re, bb_im], axis=1).transpose(0, 2, 1)
    r_re, r_im = power(SSM_CHUNK - 1 - steps)
    bt_re, bt_im = bb_re.transpose(0, 2, 1)[:, None], bb_im.transpose(0, 2, 1)[:, None]
    w_re = r_re[:, :, None, :] * bt_re - r_im[:, :, None, :] * bt_im
    w_im = r_re[:, :, None, :] * bt_im + r_im[:, :, None, :] * bt_re
    w = jnp.concatenate([w_re, w_im], axis=-1).reshape(g, width, 2 * n).astype(BF16)
    n_re, n_im = power(steps + 1)
    cv_re = c_re[:, None] * n_re[:, :, None, :] - c_im[:, None] * n_im[:, :, None, :]
    cv_im = c_re[:, None] * n_im[:, :, None, :] + c_im[:, None] * n_re[:, :, None, :]
    v = jnp.concatenate([cv_re, -cv_im], axis=-1).reshape(g, width, 2 * n).transpose(0, 2, 1).astype(BF16)
    n_steps = int(math.log2(chunks_per_seq))
    s_re, s_im = power(SSM_CHUNK * (2 ** jnp.arange(max(n_steps, 1))))
    a1 = jnp.concatenate([s_re, s_re], axis=-1)
    a2 = jnp.concatenate([-s_im, s_im], axis=-1)
    return pt, bq, w, v, a1, a2


def _mix_out_kernel(x_ref, oa_ref, ys_ref, u_ref, dsk_ref, wglu_ref, bglu_ref, na_ref, ns_ref,
                    woa_ref, wos_ref, o_ref):
    d_ssm = u_ref.shape[1]
    y = jax.nn.gelu(ys_ref[...] + dsk_ref[...] * u_ref[...])
    z = _dot(y.astype(BF16), wglu_ref[...]) + bglu_ref[...]
    o_ssm = z[:, :d_ssm] * jax.nn.sigmoid(z[:, d_ssm:])
    att_n = _rms(oa_ref[...], na_ref[...]).astype(BF16)
    ssm_n = _rms(o_ssm, ns_ref[...]).astype(BF16)
    o_ref[...] = x_ref[...] + _dot(att_n, woa_ref[...]) + _dot(ssm_n, wos_ref[...])


def mix_out(x2, o_att, y_ssm, u, d_skip, w_glu, b_glu, n_att, n_ssm, w_out):
    t, d = x2.shape
    d_att = o_att.shape[1]
    d_ssm = u.shape[1]
    tm = _row_tile(t)
    rows = lambda w: pl.BlockSpec((tm, w), lambda i: (i, 0))
    full = lambda a: pl.BlockSpec(a.shape, lambda i: (0,) * a.ndim)
    consts = [d_skip.reshape(1, d_ssm), w_glu.astype(BF16), b_glu.reshape(1, 2 * d_ssm),
              n_att.reshape(1, d_att), n_ssm.reshape(1, d_ssm),
              w_out[:d_att].astype(BF16), w_out[d_att:].astype(BF16)]
    return pl.pallas_call(
        _mix_out_kernel,
        grid=(t // tm,),
        in_specs=[rows(d), rows(d_att), rows(d_ssm), rows(d_ssm)] + [full(c) for c in consts],
        out_specs=rows(d),
        out_shape=jax.ShapeDtypeStruct((t, d), F32),
        compiler_params=_params("parallel"),
        name="mix_out",
    )(x2, o_att, y_ssm, u, *consts)


def _ffn_kernel(x_ref, g_ref, wg_ref, wu_ref, wd_ref, fg_ref, o_ref, acc_ref, *, f_chunk, final_norm):
    x = x_ref[...]
    h = _rms(x, g_ref[...]).astype(BF16)
    d_ff = wg_ref.shape[1]
    acc_ref[...] = x
    for c in range(d_ff // f_chunk):
        sl = slice(c * f_chunk, (c + 1) * f_chunk)
        a = jax.nn.silu(_dot(h, wg_ref[:, sl])) * _dot(h, wu_ref[:, sl])
        acc_ref[...] += _dot(a.astype(BF16), wd_ref[sl, :])
    out = acc_ref[...]
    if final_norm:
        out = _rms(out, fg_ref[...])
    o_ref[...] = out


def ffn(x2, g, w_gate, w_up, w_down, final_g, final_norm):
    t, d = x2.shape
    d_ff = w_gate.shape[1]
    tm = _row_tile(t)
    f_chunk = 256 if d_ff % 256 == 0 else d_ff
    full = lambda a: pl.BlockSpec(a.shape, lambda i: (0,) * a.ndim)
    consts = [g.reshape(1, d), w_gate.astype(BF16), w_up.astype(BF16), w_down.astype(BF16),
              final_g.reshape(1, d)]
    return pl.pallas_call(
        functools.partial(_ffn_kernel, f_chunk=f_chunk, final_norm=final_norm),
        grid=(t // tm,),
        in_specs=[pl.BlockSpec((tm, d), lambda i: (i, 0))] + [full(c) for c in consts],
        out_specs=pl.BlockSpec((tm, d), lambda i: (i, 0)),
        out_shape=jax.ShapeDtypeStruct((t, d), F32),
        scratch_shapes=[pltpu.VMEM((tm, d), F32)],
        compiler_params=_params("parallel"),
        name="ffn",
    )(x2, *consts)


def _mixer(x2, bsz, seq, attn_norm_g, w_in, cmp_pe, cmp_w1, cmp_b1, cmp_w2, cmp_b2, log_dt, a_re, a_im,
           b_re, b_im, c_re, c_im, d_skip, w_glu, b_glu, mix_norm_att, mix_norm_ssm, w_out):
    t, d = x2.shape
    d_ssm = d - D_ATT
    bh = bsz * N_KV
    qt, vst, vwt, gt, kc, vc, ks, kw, u = in_proj(x2.reshape(bsz, seq, d), attn_norm_g, w_in)
    k_c = compress(kc, cmp_pe[0], cmp_w1[0], cmp_b1[0], cmp_w2[0], cmp_b2[0], transposed=False)
    v_ct = compress(vc, cmp_pe[1], cmp_w1[1], cmp_b1[1], cmp_w2[1], cmp_b2[1], transposed=True)
    o_att = nsa_attention(qt.reshape(bh, GQA, HEAD_DIM, seq), k_c, v_ct,
                          ks.reshape(bh, seq, HEAD_DIM), vst.reshape(bh, V_ROWS, seq),
                          kw.reshape(bh, seq, HEAD_DIM), vwt.reshape(bh, V_ROWS, seq),
                          gt.reshape(bh, GQA, 3, seq), bsz).reshape(t, D_ATT)

    u = u.reshape(t, d_ssm)
    n_groups = d_ssm // SSM_CH
    chunks_per_seq = seq // SSM_CHUNK
    n_chunks = t // SSM_CHUNK
    width = SSM_CHUNK * SSM_CH
    u2 = u.astype(BF16).reshape(n_chunks, SSM_CHUNK, n_groups, SSM_CH).transpose(2, 0, 1, 3)
    tables = _s5_tables(log_dt, a_re, a_im, b_re, b_im, c_re, c_im, chunks_per_seq)
    y2 = s5_scan(u2.reshape(n_groups, n_chunks, width), *tables, chunks_per_seq)
    y_ssm = y2.reshape(n_groups, n_chunks, SSM_CHUNK, SSM_CH).transpose(1, 2, 0, 3).reshape(t, d_ssm)

    return mix_out(x2, o_att, y_ssm, u, d_skip, w_glu, b_glu, mix_norm_att, mix_norm_ssm, w_out)


def kernel(x, attn_norm_g, w_in, cmp_pe, cmp_w1, cmp_b1, cmp_w2, cmp_b2, log_dt, a_re, a_im, b_re, b_im, c_re, c_im, d_skip, w_glu, b_glu, mix_norm_att, mix_norm_ssm, w_out, ffn_norm_g, w_gate, w_up, w_down, final_norm_g):
    bsz, seq, d = x.shape
    depth = w_in.shape[0]
    x2 = x.reshape(bsz * seq, d)
    for l in range(depth):
        x2 = _mixer(x2, bsz, seq, attn_norm_g[l], w_in[l], cmp_pe[l], cmp_w1[l], cmp_b1[l], cmp_w2[l],
                    cmp_b2[l], log_dt[l], a_re[l], a_im[l], b_re[l], b_im[l], c_re[l], c_im[l], d_skip[l],
                    w_glu[l], b_glu[l], mix_norm_att[l], mix_norm_ssm[l], w_out[l])
        x2 = ffn(x2, ffn_norm_g[l], w_gate[l], w_up[l], w_down[l], final_norm_g, final_norm=(l == depth - 1))
    return x2.reshape(bsz, seq, d)
```

```python
import functools
import math

import jax
import jax.numpy as jnp
from jax import lax
from jax.experimental import pallas as pl
from jax.experimental.pallas import tpu as pltpu

HEAD_DIM = 64
N_KV = 3
GQA = 4
N_HEADS = N_KV * GQA
D_ATT = N_HEADS * HEAD_DIM
SSM_CH = 16
SSM_STATE = 64
L_CMP = 32
STRIDE_CMP = 16
CMP_HID = 256
L_SEL = 64
TOP_K = 16
N_FORCED = 3
WINDOW = 512
Q_BLK = 128
BIG = 1e9
TINY = 1e-30
EPS = 1e-6
LOG2E = 1.4426950408889634

SUBLANES = 8
LANES = 128
V_ROWS = HEAD_DIM + 2 * SUBLANES
SSM_CHUNK = 64
PACK_CHUNKS = 64
SEL_SUB = 256
SEL_LOOP = 32
SEL_DIAG = 8
FRONT_PARTS = 4
SEL_AHEAD = 3
VMEM_LIMIT = 56 * 1024 * 1024

F32 = jnp.float32
BF16 = jnp.bfloat16
NEG_INF = float("-inf")


def _dot(a, b):
    return jnp.dot(a, b, preferred_element_type=F32)


def _dot_nt(a, b):
    return lax.dot_general(a, b, (((1,), (1,)), ((), ())), preferred_element_type=F32)


def _dot_split(a, b):
    a_hi = a.astype(BF16)
    a_lo = (a - a_hi.astype(F32)).astype(BF16)
    b_hi = b.astype(BF16)
    b_lo = (b - b_hi.astype(F32)).astype(BF16)
    return _dot(a_hi, b_hi) + _dot(a_hi, b_lo) + _dot(a_lo, b_hi)


def _params(*sem):
    return pltpu.CompilerParams(dimension_semantics=sem, vmem_limit_bytes=VMEM_LIMIT)


def _row_tile(t):
    for tm in (512, 256, 128, 64, 32, 16, 8):
        if t % tm == 0:
            return tm
    raise ValueError(f"token count {t} must be a multiple of 8")


def _rms(x, g):
    ms = jnp.mean(x * x, axis=-1, keepdims=True)
    return x * lax.rsqrt(ms + EPS) * g


def _in_proj_kernel(x_ref, g_ref, wn_ref, wt_ref, qt_ref, vst_ref, vwt_ref, gt_ref, kc_ref, vc_ref,
                    ks_ref, kw_ref, u_ref):
    p_kv = N_KV * HEAD_DIM
    h = _rms(x_ref[0], g_ref[...]).astype(BF16)
    tm = h.shape[0]
    nat = _dot(h, wn_ref[...])
    tr = _dot_nt(wt_ref[...], h)
    qt_ref[0] = (tr[:D_ATT] * (LOG2E * HEAD_DIM ** -0.5)).astype(BF16)
    ones_rows = (lax.broadcasted_iota(jnp.int32, (V_ROWS - HEAD_DIM, tm), 0) == 0).astype(BF16)
    for hd in range(N_KV):
        rows = slice(hd * HEAD_DIM, (hd + 1) * HEAD_DIM)
        for k, ref in enumerate((vst_ref, vwt_ref)):
            off = D_ATT + k * p_kv
            ref[0, hd, :HEAD_DIM, :] = tr[off + hd * HEAD_DIM:off + (hd + 1) * HEAD_DIM].astype(BF16)
            ref[0, hd, HEAD_DIM:, :] = ones_rows
        kc_ref[0, hd] = nat[:, rows]
        vc_ref[0, hd] = nat[:, p_kv + hd * HEAD_DIM:p_kv + (hd + 1) * HEAD_DIM]
        ks_ref[0, hd] = nat[:, 2 * p_kv + hd * HEAD_DIM:2 * p_kv + (hd + 1) * HEAD_DIM].astype(BF16)
        kw_ref[0, hd] = nat[:, 3 * p_kv + hd * HEAD_DIM:3 * p_kv + (hd + 1) * HEAD_DIM].astype(BF16)
    gt_ref[0] = tr[D_ATT + 2 * p_kv:D_ATT + 2 * p_kv + 3 * N_HEADS]
    u_ref[0] = nat[:, 4 * p_kv:]


def in_proj(x3, g, w_in):
    bsz, seq, d = x3.shape
    p_kv = N_KV * HEAD_DIM
    p_gate = 3 * N_HEADS
    d_ssm = d - D_ATT
    o = [0, D_ATT]
    for width in [p_kv] * 6 + [p_gate, d_ssm]:
        o.append(o[-1] + width)
    q, kc, vc, ksl, vsl, kwn, vwn, gl, u = (w_in[:, o[i]:o[i + 1]] for i in range(9))
    w_nat = jnp.concatenate([kc, vc, ksl, kwn, u], axis=1).astype(BF16)
    n_tr = D_ATT + 2 * p_kv + p_gate
    w_tr = jnp.pad(jnp.concatenate([q, vsl, vwn, gl], axis=1), ((0, 0), (0, -n_tr % SUBLANES))).T.astype(BF16)
    tm = _row_tile(seq)
    full = lambda a: pl.BlockSpec(a.shape, lambda b, i: (0,) * a.ndim)
    feat = lambda rows: pl.BlockSpec((1, rows, tm), lambda b, i: (b, 0, i))
    vals = pl.BlockSpec((1, N_KV, V_ROWS, tm), lambda b, i: (b, 0, 0, i))
    keys = pl.BlockSpec((1, N_KV, tm, HEAD_DIM), lambda b, i: (b, 0, i, 0))
    sds = jax.ShapeDtypeStruct
    g2 = g.reshape(1, d)
    return pl.pallas_call(
        _in_proj_kernel,
        grid=(bsz, seq // tm),
        in_specs=[pl.BlockSpec((1, tm, d), lambda b, i: (b, i, 0)), full(g2), full(w_nat), full(w_tr)],
        out_specs=[feat(D_ATT), vals, vals, feat(p_gate), keys, keys, keys, keys,
                   pl.BlockSpec((1, tm, d_ssm), lambda b, i: (b, i, 0))],
        out_shape=[sds((bsz, D_ATT, seq), BF16), sds((bsz, N_KV, V_ROWS, seq), BF16),
                   sds((bsz, N_KV, V_ROWS, seq), BF16), sds((bsz, p_gate, seq), F32),
                   sds((bsz, N_KV, seq, HEAD_DIM), F32), sds((bsz, N_KV, seq, HEAD_DIM), F32),
                   sds((bsz, N_KV, seq, HEAD_DIM), BF16), sds((bsz, N_KV, seq, HEAD_DIM), BF16),
                   sds((bsz, seq, d_ssm), F32)],
        compiler_params=_params("parallel", "parallel"),
        name="in_proj",
    )(x3, g2, w_nat, w_tr)


def _compress_kernel(x_ref, pe_ref, w1_ref, b1_ref, w2_ref, b2_ref, o_ref, *, transposed):
    seq = x_ref.shape[2]
    n = seq // STRIDE_CMP
    first = jnp.zeros((n, CMP_HID), F32)
    second = jnp.zeros((n, CMP_HID), F32)
    for j in range(STRIDE_CMP):
        tok = x_ref[0, 0, pl.ds(j, n, stride=STRIDE_CMP), :]
        w_a = w1_ref[j * HEAD_DIM:(j + 1) * HEAD_DIM, :]
        w_b = w1_ref[(STRIDE_CMP + j) * HEAD_DIM:(STRIDE_CMP + j + 1) * HEAD_DIM, :]
        first = first + _dot((tok + pe_ref[j:j + 1, :]).astype(BF16), w_a)
        second = second + _dot((tok + pe_ref[STRIDE_CMP + j:STRIDE_CMP + j + 1, :]).astype(BF16), w_b)
    hid = first + pltpu.roll(second, n - 1, axis=0) + b1_ref[...]
    hid = jax.nn.gelu(hid).astype(BF16)
    if transposed:
        o_ref[0] = (_dot_nt(w2_ref[...], hid) + b2_ref[...]).astype(o_ref.dtype)
    else:
        o_ref[0] = (_dot(hid, w2_ref[...]) + b2_ref[...]).astype(o_ref.dtype)


def compress(x, pe, w1, b1, w2, b2, transposed):
    bsz, n_kv, seq, _ = x.shape
    bh = bsz * n_kv
    n = seq // STRIDE_CMP
    w1k = w1.astype(BF16)
    if transposed:
        pad = V_ROWS - HEAD_DIM
        w2k = jnp.pad(w2.T, ((0, pad), (0, 0))).astype(BF16)
        b2k = jnp.concatenate([b2, jnp.ones((1,), F32), jnp.zeros((pad - 1,), F32)]).reshape(V_ROWS, 1)
        out_block, out_shape = (1, V_ROWS, n), (bh, V_ROWS, n)
    else:
        w2k = w2.astype(BF16)
        b2k = b2.reshape(1, HEAD_DIM)
        out_block, out_shape = (1, n, HEAD_DIM), (bh, n, HEAD_DIM)
    full = lambda a: pl.BlockSpec(a.shape, lambda i: (0,) * a.ndim)
    b1k = b1.reshape(1, CMP_HID)
    return pl.pallas_call(
        functools.partial(_compress_kernel, transposed=transposed),
        grid=(bh,),
        in_specs=[pl.BlockSpec((1, 1, seq, HEAD_DIM), lambda i: (i // n_kv, i % n_kv, 0, 0)),
                  full(pe), full(w1k), full(b1k), full(w2k), full(b2k)],
        out_specs=pl.BlockSpec(out_block, lambda i: (i, 0, 0)),
        out_shape=jax.ShapeDtypeStruct(out_shape, BF16),
        compiler_params=_params("parallel"),
        name="compress_t" if transposed else "compress_n",
    )(x, pe, w1k, b1k, w2k, b2k)


def _heads(x):
    return [x[:, g * Q_BLK:(g + 1) * Q_BLK] for g in range(GQA)]


def _exp2_safe_max(m):
    return jnp.where(m == NEG_INF, 0.0, m)


def _aligned(x, m):
    return x if isinstance(x, int) else pl.multiple_of(x, m)


def _normalise(pv):
    return pv[:HEAD_DIM] / jnp.maximum(pv[HEAD_DIM:HEAD_DIM + 1], TINY)


def _attn_kernel(qt_ref, kc_ref, vct_ref, ks_ref, vst_ref, kw_ref, vwt_ref, gt_ref, ovt_ref, cthr_ref,
                 o_ref, selb_ref, s_ref, *, seq):
    qi = pl.program_id(1)
    n_sel = seq // L_SEL
    lanes = GQA * Q_BLK
    win_keys = WINDOW + Q_BLK

    qt = jnp.concatenate([qt_ref[0, g] for g in range(GQA)], axis=1)
    q0 = qi * Q_BLK
    lane_q = lax.broadcasted_iota(jnp.int32, (1, Q_BLK), 1)
    t_q = q0 + lane_q
    t_all = jnp.concatenate([t_q] * GQA, axis=1)

    def sel_scores(k0):
        return _dot(ks_ref[0, pl.ds(k0, SEL_SUB), :], qt)

    def front(n_rows, full_band, n_static):
        n_blk = n_rows * STRIDE_CMP // L_SEL
        s_c = _dot(kc_ref[0, :n_rows, :], qt)
        for a in range(SEL_AHEAD):
            s_ref[a] = sel_scores(a * SEL_SUB)
        vis = cthr_ref[:n_rows, :] <= q0
        p_heads = []
        for s_g in _heads(s_c):
            s_g = jnp.where(vis, s_g, NEG_INF)
            m_g = _exp2_safe_max(jnp.max(s_g, axis=0, keepdims=True))
            p_heads.append(jnp.exp2(s_g - m_g).astype(BF16))
        p_c = jnp.concatenate(p_heads, axis=1)
        pv_c = _dot(vct_ref[0, :, :n_rows], p_c)
        r_c = 1.0 / jnp.maximum(pv_c[HEAD_DIM:HEAD_DIM + 1], TINY)
        o_c = pv_c[:HEAD_DIM] * r_c
        imp_all = _dot(ovt_ref[:n_blk, :n_rows], p_c) * r_c
        imp = functools.reduce(jnp.add, _heads(imp_all))

        blk = lax.broadcasted_iota(jnp.int32, (n_blk, Q_BLK), 0).astype(F32)
        cur = (t_q // L_SEL).astype(F32)
        forced = (blk == 0.0) | (blk == cur) | (blk == cur - 1.0)
        free = (blk <= cur) & jnp.logical_not(forced)
        cand = jnp.where(free, imp, NEG_INF)
        n_pick = min(TOP_K, n_sel) - N_FORCED

        def rank_exact(val):
            for _ in range(n_pick):
                mx = jnp.max(val, axis=0, keepdims=True)
                first = jnp.min(jnp.where(val == mx, blk, float(n_blk)), axis=0, keepdims=True)
                val = jnp.where(blk == first, NEG_INF, val)
            return val

        val_fast = cand
        for _ in range(n_pick):
            val_fast = jnp.where(val_fast == jnp.max(val_fast, axis=0, keepdims=True), NEG_INF, val_fast)
        n_free = jnp.sum(free.astype(F32), axis=0, keepdims=True)
        n_got = jnp.sum((free & (val_fast == NEG_INF)).astype(F32), axis=0, keepdims=True)
        tied = jnp.max(jnp.where(n_got != jnp.minimum(n_free, float(n_pick)), 1.0, 0.0)) > 0.0

        start = pl.multiple_of(jnp.maximum(q0 - WINDOW, 0), Q_BLK)
        s_w = _dot(kw_ref[0, pl.ds(start, win_keys), :], qt)
        r_minus_c = (lax.broadcasted_iota(jnp.int32, (Q_BLK, Q_BLK), 0)
                     - lax.broadcasted_iota(jnp.int32, (Q_BLK, Q_BLK), 1))
        chunks = []
        n_chunks = win_keys // Q_BLK
        for i in range(n_chunks):
            rows = s_w[i * Q_BLK:(i + 1) * Q_BLK]
            if not full_band or i in (0, n_chunks - 1):
                hi = q0 - start - i * Q_BLK
                mask = (r_minus_c <= hi) & (r_minus_c > hi - WINDOW)
                rows = jnp.concatenate([jnp.where(mask, x, NEG_INF) for x in _heads(rows)], axis=1)
            chunks.append(rows)
        m_w = _exp2_safe_max(functools.reduce(jnp.maximum, [jnp.max(x, axis=0, keepdims=True) for x in chunks]))
        p_w = jnp.concatenate([jnp.exp2(x - m_w).astype(BF16) for x in chunks], axis=0)
        o_w = _normalise(_dot(vwt_ref[0, :, pl.ds(start, win_keys)], p_w))

        val = lax.cond(tied, lambda: rank_exact(cand), lambda: val_fast)
        selb_ref[:n_blk, :] = jnp.where(forced | (free & (val == NEG_INF)), 0.0, NEG_INF)
        carry = (jnp.full((1, lanes), NEG_INF, F32), jnp.zeros((V_ROWS, lanes), F32))
        if n_static:
            carry = sel_run(0, *carry, n_static, False)
        return (o_c, o_w) + tuple(carry)

    row_iota = lax.broadcasted_iota(jnp.int32, (L_SEL, lanes), 0)

    def sel_update(s_t, k0, m_prev, acc_prev, causal):
        blk0 = k0 // L_SEL
        blocks, biases, part_max = [], [], None
        for j in range(SEL_SUB // L_SEL):
            bias = jnp.concatenate([selb_ref[pl.ds(blk0 + j, 1), :]] * GQA, axis=1)
            s_b = s_t[j * L_SEL:(j + 1) * L_SEL]
            if causal:
                s_b = jnp.where(k0 + j * L_SEL + row_iota <= t_all, s_b, NEG_INF)
            s8 = functools.reduce(jnp.maximum, [s_b[r:r + SUBLANES] for r in range(0, L_SEL, SUBLANES)])
            cand = s8 + bias
            part_max = cand if part_max is None else jnp.maximum(part_max, cand)
            blocks.append(s_b)
            biases.append(bias)
        m_new = jnp.maximum(m_prev, jnp.max(part_max, axis=0, keepdims=True))
        m_safe = _exp2_safe_max(m_new)
        alpha = jnp.exp2(m_prev - m_safe)
        p = jnp.concatenate([jnp.exp2(s_b - (m_safe - bias)).astype(BF16)
                             for s_b, bias in zip(blocks, biases)], axis=0)
        pv = _dot(vst_ref[0, :, pl.ds(k0, SEL_SUB)], p)
        return m_new, alpha * acc_prev + pv

    def sel_run(k0, m_prev, acc_prev, n_sub, diagonal):
        queue = [s_ref[a] for a in range(min(SEL_AHEAD, n_sub) if diagonal else SEL_AHEAD)]
        for i in range(n_sub):
            ahead = i + SEL_AHEAD
            if not diagonal or ahead < n_sub:
                queue.append(sel_scores(_aligned(k0 + ahead * SEL_SUB, SEL_SUB)))
            m_prev, acc_prev = sel_update(queue.pop(0), _aligned(k0 + i * SEL_SUB, SEL_SUB),
                                          m_prev, acc_prev, causal=diagonal and i == n_sub - 1)
        if not diagonal:
            for a in range(SEL_AHEAD):
                s_ref[a] = queue[a]
        return m_prev, acc_prev

    n_cmp = seq // STRIDE_CMP
    parts = FRONT_PARTS if n_cmp % (FRONT_PARTS * Q_BLK) == 0 else 1
    part_subs = seq // parts // SEL_SUB
    part = q0 // (seq // parts)
    o_c, o_w, *carry = lax.switch(
        part, [functools.partial(front, n_cmp * (k + 1) // parts, k * (seq // parts) >= WINDOW, k * part_subs)
               for k in range(parts)])

    n_below = q0 // SEL_SUB
    done = part * part_subs
    if part_subs > SEL_LOOP:
        trips = (n_below - done) // SEL_LOOP
        carry = lax.fori_loop(
            0, trips, lambda kt, c: sel_run(_aligned((done + kt * SEL_LOOP) * SEL_SUB, SEL_SUB), *c, SEL_LOOP, False),
            tuple(carry))
        done = done + trips * SEL_LOOP
    run = min(SEL_LOOP, part_subs) // 2
    while run >= SEL_DIAG:
        take = ((n_below - done) // run) == 1
        k_run = _aligned(done * SEL_SUB, SEL_SUB)
        carry = lax.cond(take, functools.partial(lambda k, n, m, a: sel_run(k, m, a, n, False), k_run, run),
                         lambda m, a: (m, a), *carry)
        done = done + jnp.where(take, run, 0)
        run //= 2
    k_last = _aligned(done * SEL_SUB, SEL_SUB)

    def finish(n_diag, m_s, acc_s):
        o_s = _normalise(sel_run(k_last, m_s, acc_s, n_diag, True)[1])
        outs = []
        for g in range(GQA):
            sl = slice(g * Q_BLK, (g + 1) * Q_BLK)
            gate = jax.nn.sigmoid(gt_ref[0, g])
            outs.append(gate[0:1] * o_c[:, sl] + gate[1:2] * o_s[:, sl] + gate[2:3] * o_w[:, sl])
        pairs = [jnp.concatenate(outs[g:g + 2], axis=0).T for g in range(0, GQA, 2)]
        o_ref[0] = jnp.concatenate(pairs, axis=1)

    lax.switch(n_below - done, [functools.partial(finish, n + 1) for n in range(SEL_DIAG)], *carry)


def nsa_attention(qt, kc, vct, ks, vst, kw, vwt, gt, bsz):
    bh, _, _, seq = qt.shape
    assert SEL_LOOP % SEL_DIAG == 0 and seq % (SEL_DIAG * SEL_SUB) == 0 and seq >= WINDOW + Q_BLK
    n_cmp = seq // STRIDE_CMP
    n_sel = seq // L_SEL
    cmp_start = jnp.arange(n_cmp) * STRIDE_CMP
    sel_start = jnp.arange(n_sel) * L_SEL
    ovt = ((cmp_start[None, :] < sel_start[:, None] + L_SEL)
           & (cmp_start[None, :] + L_CMP > sel_start[:, None])).astype(BF16)
    cthr = (cmp_start + L_CMP - 1)[:, None] - jnp.arange(Q_BLK)[None, :]
    per_bh = lambda shape: pl.BlockSpec((1,) + shape, lambda b, i: (b,) + (0,) * len(shape))
    return pl.pallas_call(
        functools.partial(_attn_kernel, seq=seq),
        grid=(bh, seq // Q_BLK),
        in_specs=[pl.BlockSpec((1, GQA, HEAD_DIM, Q_BLK), lambda b, i: (b, 0, 0, i)),
                  per_bh((n_cmp, HEAD_DIM)), per_bh((V_ROWS, n_cmp)),
                  per_bh((seq, HEAD_DIM)), per_bh((V_ROWS, seq)),
                  per_bh((seq, HEAD_DIM)), per_bh((V_ROWS, seq)),
                  pl.BlockSpec((1, GQA, 3, Q_BLK), lambda b, i: (b, 0, 0, i)),
                  pl.BlockSpec((n_sel, n_cmp), lambda b, i: (0, 0)),
                  pl.BlockSpec((n_cmp, Q_BLK), lambda b, i: (0, 0))],
        out_specs=pl.BlockSpec((1, Q_BLK, GQA * HEAD_DIM), lambda b, i: (b // N_KV, i, b % N_KV)),
        out_shape=jax.ShapeDtypeStruct((bsz, seq, N_KV * GQA * HEAD_DIM), F32),
        scratch_shapes=[pltpu.VMEM((n_sel, Q_BLK), F32), pltpu.VMEM((SEL_AHEAD, SEL_SUB, GQA * Q_BLK), F32)],
        compiler_params=_params("parallel", "arbitrary"),
        name="nsa_attention",
    )(qt, kc, vct, ks, vst, kw, vwt, gt, ovt, cthr.astype(jnp.int32))


def _ssm_kernel(u_ref, pt_ref, bq_ref, w_ref, v_ref, a1_ref, a2_ref, o_ref, mt_ref, *, chunks_per_seq):
    width = SSM_CHUNK * SSM_CH
    u = u_ref[0]
    kt = _dot_split(bq_ref[0], pt_ref[0])
    col = lax.broadcasted_iota(jnp.int32, (SSM_CH, width), 1)
    for s in range(SSM_CHUNK):
        shifted = kt if s == 0 else pltpu.roll(kt, SSM_CH * s, axis=1)
        mt_ref[s * SSM_CH:(s + 1) * SSM_CH, :] = jnp.where(col >= SSM_CH * s, shifted, 0.0).astype(BF16)
    y = _dot(u, mt_ref[...])
    x = _dot(u, w_ref[0])
    row = lax.broadcasted_iota(jnp.int32, x.shape, 0) % chunks_per_seq
    steps = int(math.log2(chunks_per_seq))
    for j in range(steps):
        d = 1 << j
        sh = jnp.where(row >= d, pltpu.roll(x, d, axis=0), 0.0)
        x = x + a1_ref[0, j:j + 1, :] * sh + a2_ref[0, j:j + 1, :] * pltpu.roll(sh, SSM_STATE, axis=1)
    prev = jnp.where(row >= 1, pltpu.roll(x, 1, axis=0), 0.0)
    o_ref[0] = y + _dot(prev.astype(BF16), v_ref[0])


def s5_scan(u2, pt, bq, w, v, a1, a2, chunks_per_seq):
    g, n_chunks, width = u2.shape
    per_g = lambda a: pl.BlockSpec((1,) + a.shape[1:], lambda i: (i,) + (0,) * (a.ndim - 1))
    return pl.pallas_call(
        functools.partial(_ssm_kernel, chunks_per_seq=chunks_per_seq),
        grid=(g,),
        in_specs=[per_g(u2), per_g(pt), per_g(bq), per_g(w), per_g(v), per_g(a1), per_g(a2)],
        out_specs=pl.BlockSpec((1, n_chunks, width), lambda i: (i, 0, 0)),
        out_shape=jax.ShapeDtypeStruct((g, n_chunks, width), F32),
        scratch_shapes=[pltpu.VMEM((width, width), BF16)],
        compiler_params=_params("parallel"),
        name="s5_scan",
    )(u2, pt, bq, w, v, a1, a2)


def _s5_tables(log_dt, a_re, a_im, b_re, b_im, c_re, c_im, chunks_per_seq):
    g, n = a_re.shape
    dt = jnp.exp(log_dt)[:, None]
    lam_re, lam_im = dt * a_re, dt * a_im

    def power(k):
        k = k.astype(F32)[None, :, None]
        mag = jnp.exp(k * lam_re[:, None, :])
        return mag * jnp.cos(k * lam_im[:, None, :]), mag * jnp.sin(k * lam_im[:, None, :])

    abar_re, abar_im = (x[:, 0] for x in power(jnp.ones((1,))))
    den = a_re * a_re + a_im * a_im
    nr = abar_re - 1.0
    f_re = (nr * a_re + abar_im * a_im) / den
    f_im = (abar_im * a_re - nr * a_im) / den
    bb_re = f_re[..., None] * b_re - f_im[..., None] * b_im
    bb_im = f_re[..., None] * b_im + f_im[..., None] * b_re
    steps = jnp.arange(SSM_CHUNK)
    width = SSM_CHUNK * SSM_CH
    e_re, e_im = power(steps)
    ca_re = c_re[:, None] * e_re[:, :, None, :] - c_im[:, None] * e_im[:, :, None, :]
    ca_im = c_re[:, None] * e_im[:, :, None, :] + c_im[:, None] * e_re[:, :, None, :]
    pt = jnp.concatenate([ca_re, -ca_im], axis=-1).reshape(g, width, 2 * n).transpose(0, 2, 1)
    bq = jnp.concatenate([bb_re, bb_im], axis=1).transpose(0, 2, 1)
    r_re, r_im = power(SSM_CHUNK - 1 - steps)
    bt_re, bt_im = bb_re.transpose(0, 2, 1)[:, None], bb_im.transpose(0, 2, 1)[:, None]
    w_re = r_re[:, :, None, :] * bt_re - r_im[:, :, None, :] * bt_im
    w_im = r_re[:, :, None, :] * bt_im + r_im[:, :, None, :] * bt_re
    w = jnp.concatenate([w_re, w_im], axis=-1).reshape(g, width, 2 * n).astype(BF16)
    n_re, n_im = power(steps + 1)
    cv_re = c_re[:, None] * n_re[:, :, None, :] - c_im[:, None] * n_im[:, :, None, :]
    cv_im = c_re[:, None] * n_im[:, :, None, :] + c_im[:, None] * n_re[:, :, None, :]
    v = jnp.concatenate([cv_re, -cv_im], axis=-1).reshape(g, width, 2 * n).transpose(0, 2, 1).astype(BF16)
    n_steps = int(math.log2(chunks_per_seq))
    s_re, s_im = power(SSM_CHUNK * (2 ** jnp.arange(max(n_steps, 1))))
    a1 = jnp.concatenate([s_re, s_re], axis=-1)
    a2 = jnp.concatenate([-s_im, s_im], axis=-1)
    return pt, bq, w, v, a1, a2


def _slot_masks(rows, width):
    slot = lax.broadcasted_iota(jnp.int32, (rows, width), 1) // SSM_CH
    return [slot == k for k in range(width // SSM_CH)]


def _s5_pack_kernel(*refs):
    u_refs, o_ref = refs[:-1], refs[-1]
    d = LANES * len(u_refs)
    n_groups = d // SSM_CH
    cb = o_ref.shape[1]
    steps = LANES // SSM_CH
    masks = _slot_masks(cb, d)
    for m in range(SSM_CHUNK // steps):
        rot = []
        for j in range(steps):
            rows = [r[pl.ds(m * steps + j, cb, stride=SSM_CHUNK), :] for r in u_refs]
            rot.append(pltpu.roll(jnp.concatenate(rows, axis=1), SSM_CH * j, axis=1))
        for g in range(n_groups):
            mix = jnp.zeros((cb, d), F32)
            for j in range(steps):
                mix = jnp.where(masks[(g + j) % n_groups], rot[j], mix)
            out = mix if g == 0 else pltpu.roll(mix, d - SSM_CH * g, axis=1)
            o_ref[g, :, m * LANES:(m + 1) * LANES] = out[:, :LANES].astype(o_ref.dtype)


def s5_pack(u):
    t, d = u.shape
    n_groups = d // SSM_CH
    n_chunks = t // SSM_CHUNK
    cb = min(PACK_CHUNKS, n_chunks)
    width = SSM_CHUNK * SSM_CH
    return pl.pallas_call(
        _s5_pack_kernel,
        grid=(n_chunks // cb,),
        in_specs=[pl.BlockSpec((cb * SSM_CHUNK, LANES), functools.partial(lambda b, i: (i, b), b))
                  for b in range(d // LANES)],
        out_specs=pl.BlockSpec((n_groups, cb, width), lambda i: (0, i, 0)),
        out_shape=jax.ShapeDtypeStruct((n_groups, n_chunks, width), BF16),
        compiler_params=_params("parallel"),
        name="s5_pack",
    )(*[u] * (d // LANES))


def _s5_unpack_kernel(y_ref, *o_refs):
    n_groups, cb, _ = y_ref.shape
    d = LANES * len(o_refs)
    steps = LANES // SSM_CH
    masks = _slot_masks(cb, d)
    pad = jnp.zeros((cb, d - LANES), F32)
    for m in range(SSM_CHUNK // steps):
        rot = []
        for g in range(n_groups):
            wide = jnp.concatenate([y_ref[g, :, m * LANES:(m + 1) * LANES], pad], axis=1)
            rot.append(wide if g == 0 else pltpu.roll(wide, SSM_CH * g, axis=1))
        for j in range(steps):
            mix = jnp.zeros((cb, d), F32)
            for g in range(n_groups):
                mix = jnp.where(masks[(g + j) % n_groups], rot[g], mix)
            out = mix if j == 0 else pltpu.roll(mix, d - SSM_CH * j, axis=1)
            for b, o_ref in enumerate(o_refs):
                o_ref[pl.ds(m * steps + j, cb, stride=SSM_CHUNK), :] = out[:, b * LANES:(b + 1) * LANES]


def s5_unpack(y2):
    n_groups, n_chunks, width = y2.shape
    d = n_groups * SSM_CH
    cb = min(PACK_CHUNKS, n_chunks)
    return pl.pallas_call(
        _s5_unpack_kernel,
        grid=(n_chunks // cb,),
        in_specs=[pl.BlockSpec((n_groups, cb, width), lambda i: (0, i, 0))],
        out_specs=[pl.BlockSpec((cb * SSM_CHUNK, LANES), lambda i: (i, 0))] * (d // LANES),
        out_shape=[jax.ShapeDtypeStruct((n_chunks * SSM_CHUNK, LANES), F32)] * (d // LANES),
        compiler_params=_params("parallel"),
        name="s5_unpack",
    )(y2)


def _mix_out_kernel(x_ref, oa_ref, u_ref, dsk_ref, wglu_ref, bglu_ref, na_ref, ns_ref, woa_ref, wos_ref,
                    *refs):
    ys_refs, o_ref = refs[:-1], refs[-1]
    d_ssm = u_ref.shape[1]
    y_scan = jnp.concatenate([r[...] for r in ys_refs], axis=1)
    y = jax.nn.gelu(y_scan + dsk_ref[...] * u_ref[...])
    z = _dot(y.astype(BF16), wglu_ref[...]) + bglu_ref[...]
    o_ssm = z[:, :d_ssm] * jax.nn.sigmoid(z[:, d_ssm:])
    att_n = _rms(oa_ref[...], na_ref[...]).astype(BF16)
    ssm_n = _rms(o_ssm, ns_ref[...]).astype(BF16)
    o_ref[...] = x_ref[...] + _dot(att_n, woa_ref[...]) + _dot(ssm_n, wos_ref[...])


def mix_out(x2, o_att, y_ssm, u, d_skip, w_glu, b_glu, n_att, n_ssm, w_out):
    t, d = x2.shape
    d_att = o_att.shape[1]
    d_ssm = u.shape[1]
    tm = _row_tile(t)
    rows = lambda w: pl.BlockSpec((tm, w), lambda i: (i, 0))
    full = lambda a: pl.BlockSpec(a.shape, lambda i: (0,) * a.ndim)
    consts = [d_skip.reshape(1, d_ssm), w_glu.astype(BF16), b_glu.reshape(1, 2 * d_ssm),
              n_att.reshape(1, d_att), n_ssm.reshape(1, d_ssm),
              w_out[:d_att].astype(BF16), w_out[d_att:].astype(BF16)]
    return pl.pallas_call(
        _mix_out_kernel,
        grid=(t // tm,),
        in_specs=[rows(d), rows(d_att), rows(d_ssm)] + [full(c) for c in consts] + [rows(LANES)] * len(y_ssm),
        out_specs=rows(d),
        out_shape=jax.ShapeDtypeStruct((t, d), F32),
        compiler_params=_params("parallel"),
        name="mix_out",
    )(x2, o_att, u, *consts, *y_ssm)


def _ffn_kernel(x_ref, g_ref, wg_ref, wu_ref, wd_ref, fg_ref, o_ref, acc_ref, *, f_chunk, final_norm):
    x = x_ref[...]
    h = _rms(x, g_ref[...]).astype(BF16)
    d_ff = wg_ref.shape[1]
    acc_ref[...] = x
    for c in range(d_ff // f_chunk):
        sl = slice(c * f_chunk, (c + 1) * f_chunk)
        a = jax.nn.silu(_dot(h, wg_ref[:, sl])) * _dot(h, wu_ref[:, sl])
        acc_ref[...] += _dot(a.astype(BF16), wd_ref[sl, :])
    out = acc_ref[...]
    if final_norm:
        out = _rms(out, fg_ref[...])
    o_ref[...] = out


def ffn(x2, g, w_gate, w_up, w_down, final_g, final_norm):
    t, d = x2.shape
    d_ff = w_gate.shape[1]
    tm = _row_tile(t)
    f_chunk = 256 if d_ff % 256 == 0 else d_ff
    full = lambda a: pl.BlockSpec(a.shape, lambda i: (0,) * a.ndim)
    consts = [g.reshape(1, d), w_gate.astype(BF16), w_up.astype(BF16), w_down.astype(BF16),
              final_g.reshape(1, d)]
    return pl.pallas_call(
        functools.partial(_ffn_kernel, f_chunk=f_chunk, final_norm=final_norm),
        grid=(t // tm,),
        in_specs=[pl.BlockSpec((tm, d), lambda i: (i, 0))] + [full(c) for c in consts],
        out_specs=pl.BlockSpec((tm, d), lambda i: (i, 0)),
        out_shape=jax.ShapeDtypeStruct((t, d), F32),
        scratch_shapes=[pltpu.VMEM((tm, d), F32)],
        compiler_params=_params("parallel"),
        name="ffn",
    )(x2, *consts)


def _mixer(x2, bsz, seq, attn_norm_g, w_in, cmp_pe, cmp_w1, cmp_b1, cmp_w2, cmp_b2, log_dt, a_re, a_im,
           b_re, b_im, c_re, c_im, d_skip, w_glu, b_glu, mix_norm_att, mix_norm_ssm, w_out):
    t, d = x2.shape
    d_ssm = d - D_ATT
    bh = bsz * N_KV
    qt, vst, vwt, gt, kc, vc, ks, kw, u = in_proj(x2.reshape(bsz, seq, d), attn_norm_g, w_in)
    k_c = compress(kc, cmp_pe[0], cmp_w1[0], cmp_b1[0], cmp_w2[0], cmp_b2[0], transposed=False)
    v_ct = compress(vc, cmp_pe[1], cmp_w1[1], cmp_b1[1], cmp_w2[1], cmp_b2[1], transposed=True)
    o_att = nsa_attention(qt.reshape(bh, GQA, HEAD_DIM, seq), k_c, v_ct,
                          ks.reshape(bh, seq, HEAD_DIM), vst.reshape(bh, V_ROWS, seq),
                          kw.reshape(bh, seq, HEAD_DIM), vwt.reshape(bh, V_ROWS, seq),
                          gt.reshape(bh, GQA, 3, seq), bsz).reshape(t, D_ATT)

    u = u.reshape(t, d_ssm)
    chunks_per_seq = seq // SSM_CHUNK
    tables = _s5_tables(log_dt, a_re, a_im, b_re, b_im, c_re, c_im, chunks_per_seq)
    y_ssm = s5_unpack(s5_scan(s5_pack(u), *tables, chunks_per_seq))

    return mix_out(x2, o_att, y_ssm, u, d_skip, w_glu, b_glu, mix_norm_att, mix_norm_ssm, w_out)


def kernel(x, attn_norm_g, w_in, cmp_pe, cmp_w1, cmp_b1, cmp_w2, cmp_b2, log_dt, a_re, a_im, b_re, b_im, c_re, c_im, d_skip, w_glu, b_glu, mix_norm_att, mix_norm_ssm, w_out, ffn_norm_g, w_gate, w_up, w_down, final_norm_g):
    bsz, seq, d = x.shape
    depth = w_in.shape[0]
    x2 = x.reshape(bsz * seq, d)
    for l in range(depth):
        x2 = _mixer(x2, bsz, seq, attn_norm_g[l], w_in[l], cmp_pe[l], cmp_w1[l], cmp_b1[l], cmp_w2[l],
                    cmp_b2[l], log_dt[l], a_re[l], a_im[l], b_re[l], b_im[l], c_re[l], c_im[l], d_skip[l],
                    w_glu[l], b_glu[l], mix_norm_att[l], mix_norm_ssm[l], w_out[l])
        x2 = ffn(x2, ffn_norm_g[l], w_gate[l], w_up[l], w_down[l], final_norm_g, final_norm=(l == depth - 1))
    return x2.reshape(bsz, seq, d)
```

```python
import functools
import math

import jax
import jax.numpy as jnp
from jax import lax
from jax.experimental import pallas as pl
from jax.experimental.pallas import tpu as pltpu

HEAD_DIM = 64
N_KV = 3
GQA = 4
N_HEADS = N_KV * GQA
D_ATT = N_HEADS * HEAD_DIM
SSM_CH = 16
SSM_STATE = 64
L_CMP = 32
STRIDE_CMP = 16
CMP_HID = 256
L_SEL = 64
TOP_K = 16
N_FORCED = 3
WINDOW = 512
Q_BLK = 128
TINY = 1e-30
EPS = 1e-6
LOG2E = 1.4426950408889634

SUBLANES = 8
LANES = 128
V_ROWS = HEAD_DIM + 2 * SUBLANES
SSM_CHUNK = 64
PACK_CHUNKS = 64
SEL_SUB = 256
SEL_LOOP = 32
SEL_DIAG = 8
FRONT_PARTS = 4
SEL_AHEAD = 3
VMEM_LIMIT = 56 * 1024 * 1024

F32 = jnp.float32
BF16 = jnp.bfloat16
NEG_INF = float("-inf")


def _dot(a, b):
    return jnp.dot(a, b, preferred_element_type=F32)


def _dot_nt(a, b):
    return lax.dot_general(a, b, (((1,), (1,)), ((), ())), preferred_element_type=F32)


def _dot_split(a, b):
    a_hi = a.astype(BF16)
    a_lo = (a - a_hi.astype(F32)).astype(BF16)
    b_hi = b.astype(BF16)
    b_lo = (b - b_hi.astype(F32)).astype(BF16)
    return _dot(a_hi, b_hi) + _dot(a_hi, b_lo) + _dot(a_lo, b_hi)


def _params(*sem):
    return pltpu.CompilerParams(dimension_semantics=sem, vmem_limit_bytes=VMEM_LIMIT)


def _row_tile(t):
    for tm in (512, 256, 128, 64, 32, 16, 8):
        if t % tm == 0:
            return tm
    raise ValueError(f"token count {t} must be a multiple of 8")


def _rms(x, g):
    ms = jnp.mean(x * x, axis=-1, keepdims=True)
    return x * lax.rsqrt(ms + EPS) * g


def _in_proj_kernel(x_ref, g_ref, wn_ref, wt_ref, qt_ref, vst_ref, vwt_ref, gt_ref, kc_ref, vc_ref,
                    ks_ref, kw_ref, u_ref):
    p_kv = N_KV * HEAD_DIM
    h = _rms(x_ref[0], g_ref[...]).astype(BF16)
    tm = h.shape[0]
    nat = _dot(h, wn_ref[...])
    tr = _dot_nt(wt_ref[...], h)
    qt_ref[0] = (tr[:D_ATT] * (LOG2E * HEAD_DIM ** -0.5)).astype(BF16)
    ones_rows = (lax.broadcasted_iota(jnp.int32, (V_ROWS - HEAD_DIM, tm), 0) == 0).astype(BF16)
    for hd in range(N_KV):
        rows = slice(hd * HEAD_DIM, (hd + 1) * HEAD_DIM)
        for k, ref in enumerate((vst_ref, vwt_ref)):
            off = D_ATT + k * p_kv
            ref[0, hd, :HEAD_DIM, :] = tr[off + hd * HEAD_DIM:off + (hd + 1) * HEAD_DIM].astype(BF16)
            ref[0, hd, HEAD_DIM:, :] = ones_rows
        kc_ref[0, hd] = nat[:, rows]
        vc_ref[0, hd] = nat[:, p_kv + hd * HEAD_DIM:p_kv + (hd + 1) * HEAD_DIM]
        ks_ref[0, hd] = nat[:, 2 * p_kv + hd * HEAD_DIM:2 * p_kv + (hd + 1) * HEAD_DIM].astype(BF16)
        kw_ref[0, hd] = nat[:, 3 * p_kv + hd * HEAD_DIM:3 * p_kv + (hd + 1) * HEAD_DIM].astype(BF16)
    gt_ref[0] = tr[D_ATT + 2 * p_kv:D_ATT + 2 * p_kv + 3 * N_HEADS]
    u_ref[0] = nat[:, 4 * p_kv:]


def in_proj(x3, g, w_in):
    bsz, seq, d = x3.shape
    p_kv = N_KV * HEAD_DIM
    p_gate = 3 * N_HEADS
    d_ssm = d - D_ATT
    o = [0, D_ATT]
    for width in [p_kv] * 6 + [p_gate, d_ssm]:
        o.append(o[-1] + width)
    q, kc, vc, ksl, vsl, kwn, vwn, gl, u = (w_in[:, o[i]:o[i + 1]] for i in range(9))
    w_nat = jnp.concatenate([kc, vc, ksl, kwn, u], axis=1).astype(BF16)
    n_tr = D_ATT + 2 * p_kv + p_gate
    w_tr = jnp.pad(jnp.concatenate([q, vsl, vwn, gl], axis=1), ((0, 0), (0, -n_tr % SUBLANES))).T.astype(BF16)
    tm = _row_tile(seq)
    full = lambda a: pl.BlockSpec(a.shape, lambda b, i: (0,) * a.ndim)
    feat = lambda rows: pl.BlockSpec((1, rows, tm), lambda b, i: (b, 0, i))
    vals = pl.BlockSpec((1, N_KV, V_ROWS, tm), lambda b, i: (b, 0, 0, i))
    keys = pl.BlockSpec((1, N_KV, tm, HEAD_DIM), lambda b, i: (b, 0, i, 0))
    sds = jax.ShapeDtypeStruct
    g2 = g.reshape(1, d)
    return pl.pallas_call(
        _in_proj_kernel,
        grid=(bsz, seq // tm),
        in_specs=[pl.BlockSpec((1, tm, d), lambda b, i: (b, i, 0)), full(g2), full(w_nat), full(w_tr)],
        out_specs=[feat(D_ATT), vals, vals, feat(p_gate), keys, keys, keys, keys,
                   pl.BlockSpec((1, tm, d_ssm), lambda b, i: (b, i, 0))],
        out_shape=[sds((bsz, D_ATT, seq), BF16), sds((bsz, N_KV, V_ROWS, seq), BF16),
                   sds((bsz, N_KV, V_ROWS, seq), BF16), sds((bsz, p_gate, seq), F32),
                   sds((bsz, N_KV, seq, HEAD_DIM), F32), sds((bsz, N_KV, seq, HEAD_DIM), F32),
                   sds((bsz, N_KV, seq, HEAD_DIM), BF16), sds((bsz, N_KV, seq, HEAD_DIM), BF16),
                   sds((bsz, seq, d_ssm), F32)],
        compiler_params=_params("parallel", "parallel"),
        name="in_proj",
    )(x3, g2, w_nat, w_tr)


def _compress_kernel(x_ref, pe_ref, w1_ref, b1_ref, w2_ref, b2_ref, o_ref, *, transposed):
    seq = x_ref.shape[2]
    n = seq // STRIDE_CMP
    first = jnp.zeros((n, CMP_HID), F32)
    second = jnp.zeros((n, CMP_HID), F32)
    for j in range(STRIDE_CMP):
        tok = x_ref[0, 0, pl.ds(j, n, stride=STRIDE_CMP), :]
        w_a = w1_ref[j * HEAD_DIM:(j + 1) * HEAD_DIM, :]
        w_b = w1_ref[(STRIDE_CMP + j) * HEAD_DIM:(STRIDE_CMP + j + 1) * HEAD_DIM, :]
        first = first + _dot((tok + pe_ref[j:j + 1, :]).astype(BF16), w_a)
        second = second + _dot((tok + pe_ref[STRIDE_CMP + j:STRIDE_CMP + j + 1, :]).astype(BF16), w_b)
    hid = first + pltpu.roll(second, n - 1, axis=0) + b1_ref[...]
    hid = jax.nn.gelu(hid).astype(BF16)
    if transposed:
        o_ref[0] = (_dot_nt(w2_ref[...], hid) + b2_ref[...]).astype(o_ref.dtype)
    else:
        o_ref[0] = (_dot(hid, w2_ref[...]) + b2_ref[...]).astype(o_ref.dtype)


def compress(x, pe, w1, b1, w2, b2, transposed):
    bsz, n_kv, seq, _ = x.shape
    bh = bsz * n_kv
    n = seq // STRIDE_CMP
    w1k = w1.astype(BF16)
    if transposed:
        pad = V_ROWS - HEAD_DIM
        w2k = jnp.pad(w2.T, ((0, pad), (0, 0))).astype(BF16)
        b2k = jnp.concatenate([b2, jnp.ones((1,), F32), jnp.zeros((pad - 1,), F32)]).reshape(V_ROWS, 1)
        out_block, out_shape = (1, V_ROWS, n), (bh, V_ROWS, n)
    else:
        w2k = w2.astype(BF16)
        b2k = b2.reshape(1, HEAD_DIM)
        out_block, out_shape = (1, n, HEAD_DIM), (bh, n, HEAD_DIM)
    full = lambda a: pl.BlockSpec(a.shape, lambda i: (0,) * a.ndim)
    b1k = b1.reshape(1, CMP_HID)
    return pl.pallas_call(
        functools.partial(_compress_kernel, transposed=transposed),
        grid=(bh,),
        in_specs=[pl.BlockSpec((1, 1, seq, HEAD_DIM), lambda i: (i // n_kv, i % n_kv, 0, 0)),
                  full(pe), full(w1k), full(b1k), full(w2k), full(b2k)],
        out_specs=pl.BlockSpec(out_block, lambda i: (i, 0, 0)),
        out_shape=jax.ShapeDtypeStruct(out_shape, BF16),
        compiler_params=_params("parallel"),
        name="compress_t" if transposed else "compress_n",
    )(x, pe, w1k, b1k, w2k, b2k)


def _heads(x):
    return [x[:, g * Q_BLK:(g + 1) * Q_BLK] for g in range(GQA)]


def _exp2_safe_max(m):
    return jnp.where(m == NEG_INF, 0.0, m)


def _aligned(x, m):
    return x if isinstance(x, int) else pl.multiple_of(x, m)


def _normalise(pv):
    return pv[:HEAD_DIM] / jnp.maximum(pv[HEAD_DIM:HEAD_DIM + 1], TINY)


def _attn_kernel(qt_ref, kc_ref, vct_ref, ks_ref, vst_ref, kw_ref, vwt_ref, gt_ref, ovt_ref, cthr_ref,
                 o_ref, selb_ref, s_ref, *, seq):
    qi = pl.program_id(1)
    n_sel = seq // L_SEL
    lanes = GQA * Q_BLK
    win_keys = WINDOW + Q_BLK

    qt = jnp.concatenate([qt_ref[0, g] for g in range(GQA)], axis=1)
    q0 = qi * Q_BLK
    lane_q = lax.broadcasted_iota(jnp.int32, (1, Q_BLK), 1)
    t_q = q0 + lane_q
    t_all = jnp.concatenate([t_q] * GQA, axis=1)

    def sel_scores(k0):
        return _dot(ks_ref[0, pl.ds(k0, SEL_SUB), :], qt)

    def front(n_rows, full_band, n_static):
        n_blk = n_rows * STRIDE_CMP // L_SEL
        s_c = _dot(kc_ref[0, :n_rows, :], qt)
        for a in range(SEL_AHEAD):
            s_ref[a] = sel_scores(a * SEL_SUB)
        vis = cthr_ref[:n_rows, :] <= q0
        p_heads = []
        for s_g in _heads(s_c):
            s_g = jnp.where(vis, s_g, NEG_INF)
            m_g = _exp2_safe_max(jnp.max(s_g, axis=0, keepdims=True))
            p_heads.append(jnp.exp2(s_g - m_g).astype(BF16))
        p_c = jnp.concatenate(p_heads, axis=1)
        pv_c = _dot(vct_ref[0, :, :n_rows], p_c)
        r_c = 1.0 / jnp.maximum(pv_c[HEAD_DIM:HEAD_DIM + 1], TINY)
        o_c = pv_c[:HEAD_DIM] * r_c
        imp_all = _dot(ovt_ref[:n_blk, :n_rows], p_c) * r_c
        imp = functools.reduce(jnp.add, _heads(imp_all))

        blk = lax.broadcasted_iota(jnp.int32, (n_blk, Q_BLK), 0).astype(F32)
        cur = (t_q // L_SEL).astype(F32)
        forced = (blk == 0.0) | (blk == cur) | (blk == cur - 1.0)
        free = (blk <= cur) & jnp.logical_not(forced)
        cand = jnp.where(free, imp, NEG_INF)
        n_pick = min(TOP_K, n_sel) - N_FORCED

        def rank_exact(val):
            for _ in range(n_pick):
                mx = jnp.max(val, axis=0, keepdims=True)
                first = jnp.min(jnp.where(val == mx, blk, float(n_blk)), axis=0, keepdims=True)
                val = jnp.where(blk == first, NEG_INF, val)
            return val

        val_fast = cand
        for _ in range(n_pick):
            val_fast = jnp.where(val_fast == jnp.max(val_fast, axis=0, keepdims=True), NEG_INF, val_fast)
        n_free = jnp.sum(free.astype(F32), axis=0, keepdims=True)
        n_got = jnp.sum((free & (val_fast == NEG_INF)).astype(F32), axis=0, keepdims=True)
        tied = jnp.max(jnp.where(n_got != jnp.minimum(n_free, float(n_pick)), 1.0, 0.0)) > 0.0

        start = pl.multiple_of(jnp.maximum(q0 - WINDOW, 0), Q_BLK)
        s_w = _dot(kw_ref[0, pl.ds(start, win_keys), :], qt)
        r_minus_c = (lax.broadcasted_iota(jnp.int32, (Q_BLK, Q_BLK), 0)
                     - lax.broadcasted_iota(jnp.int32, (Q_BLK, Q_BLK), 1))
        chunks = []
        n_chunks = win_keys // Q_BLK
        for i in range(n_chunks):
            rows = s_w[i * Q_BLK:(i + 1) * Q_BLK]
            if not full_band or i in (0, n_chunks - 1):
                hi = q0 - start - i * Q_BLK
                mask = (r_minus_c <= hi) & (r_minus_c > hi - WINDOW)
                rows = jnp.concatenate([jnp.where(mask, x, NEG_INF) for x in _heads(rows)], axis=1)
            chunks.append(rows)
        m_w = _exp2_safe_max(functools.reduce(jnp.maximum, [jnp.max(x, axis=0, keepdims=True) for x in chunks]))
        p_w = jnp.concatenate([jnp.exp2(x - m_w).astype(BF16) for x in chunks], axis=0)
        o_w = _normalise(_dot(vwt_ref[0, :, pl.ds(start, win_keys)], p_w))
        val = lax.cond(tied, lambda: rank_exact(cand), lambda: val_fast)
        selb_ref[:n_blk, :] = jnp.where(forced | (free & (val == NEG_INF)), 0.0, NEG_INF)
        carry = (jnp.full((1, lanes), NEG_INF, F32), jnp.zeros((V_ROWS, lanes), F32))
        if n_static:
            carry = sel_run(0, *carry, n_static, False)
        return (o_c, o_w) + tuple(carry)

    row_iota = lax.broadcasted_iota(jnp.int32, (L_SEL, lanes), 0)

    def sel_update(s_t, k0, m_prev, acc_prev, causal):
        blk0 = k0 // L_SEL
        blocks, biases, part_max = [], [], None
        for j in range(SEL_SUB // L_SEL):
            bias = jnp.concatenate([selb_ref[pl.ds(blk0 + j, 1), :]] * GQA, axis=1)
            s_b = s_t[j * L_SEL:(j + 1) * L_SEL]
            if causal:
                s_b = jnp.where(k0 + j * L_SEL + row_iota <= t_all, s_b, NEG_INF)
            s8 = functools.reduce(jnp.maximum, [s_b[r:r + SUBLANES] for r in range(0, L_SEL, SUBLANES)])
            cand = s8 + bias
            part_max = cand if part_max is None else jnp.maximum(part_max, cand)
            blocks.append(s_b)
            biases.append(bias)
        m_new = jnp.maximum(m_prev, jnp.max(part_max, axis=0, keepdims=True))
        m_safe = _exp2_safe_max(m_new)
        alpha = jnp.exp2(m_prev - m_safe)
        p = jnp.concatenate([jnp.exp2(s_b - (m_safe - bias)).astype(BF16)
                             for s_b, bias in zip(blocks, biases)], axis=0)
        pv = _dot(vst_ref[0, :, pl.ds(k0, SEL_SUB)], p)
        return m_new, alpha * acc_prev + pv

    def sel_run(k0, m_prev, acc_prev, n_sub, diagonal):
        queue = [s_ref[a] for a in range(min(SEL_AHEAD, n_sub) if diagonal else SEL_AHEAD)]
        for i in range(n_sub):
            ahead = i + SEL_AHEAD
            if not diagonal or ahead < n_sub:
                queue.append(sel_scores(_aligned(k0 + ahead * SEL_SUB, SEL_SUB)))
            m_prev, acc_prev = sel_update(queue.pop(0), _aligned(k0 + i * SEL_SUB, SEL_SUB),
                                          m_prev, acc_prev, causal=diagonal and i == n_sub - 1)
        if not diagonal:
            for a in range(SEL_AHEAD):
                s_ref[a] = queue[a]
        return m_prev, acc_prev

    n_cmp = seq // STRIDE_CMP
    parts = FRONT_PARTS if n_cmp % (FRONT_PARTS * Q_BLK) == 0 else 1
    part_subs = seq // parts // SEL_SUB
    part = q0 // (seq // parts)
    o_c, o_w, *carry = lax.switch(
        part, [functools.partial(front, n_cmp * (k + 1) // parts, k * (seq // parts) >= WINDOW, k * part_subs)
               for k in range(parts)])

    n_below = q0 // SEL_SUB
    done = part * part_subs
    if part_subs > SEL_LOOP:
        trips = (n_below - done) // SEL_LOOP
        carry = lax.fori_loop(
            0, trips, lambda kt, c: sel_run(_aligned((done + kt * SEL_LOOP) * SEL_SUB, SEL_SUB), *c, SEL_LOOP, False),
            tuple(carry))
        done = done + trips * SEL_LOOP
    run = min(SEL_LOOP, part_subs) // 2
    while run >= SEL_DIAG:
        take = ((n_below - done) // run) == 1
        k_run = _aligned(done * SEL_SUB, SEL_SUB)
        carry = lax.cond(take, functools.partial(lambda k, n, m, a: sel_run(k, m, a, n, False), k_run, run),
                         lambda m, a: (m, a), *carry)
        done = done + jnp.where(take, run, 0)
        run //= 2
    k_last = _aligned(done * SEL_SUB, SEL_SUB)

    def finish(n_diag, m_s, acc_s):
        o_s = _normalise(sel_run(k_last, m_s, acc_s, n_diag, True)[1])
        outs = []
        for g in range(GQA):
            sl = slice(g * Q_BLK, (g + 1) * Q_BLK)
            gate = jax.nn.sigmoid(gt_ref[0, g])
            outs.append(gate[0:1] * o_c[:, sl] + gate[1:2] * o_s[:, sl] + gate[2:3] * o_w[:, sl])
        pairs = [jnp.concatenate(outs[g:g + 2], axis=0).T for g in range(0, GQA, 2)]
        o_ref[0] = jnp.concatenate(pairs, axis=1)

    lax.switch(n_below - done, [functools.partial(finish, n + 1) for n in range(SEL_DIAG)], *carry)


def nsa_attention(qt, kc, vct, ks, vst, kw, vwt, gt, bsz):
    bh, _, _, seq = qt.shape
    assert SEL_LOOP % SEL_DIAG == 0 and seq % (SEL_DIAG * SEL_SUB) == 0 and seq >= WINDOW + Q_BLK
    n_cmp = seq // STRIDE_CMP
    n_sel = seq // L_SEL
    cmp_start = jnp.arange(n_cmp) * STRIDE_CMP
    sel_start = jnp.arange(n_sel) * L_SEL
    ovt = ((cmp_start[None, :] < sel_start[:, None] + L_SEL)
           & (cmp_start[None, :] + L_CMP > sel_start[:, None])).astype(BF16)
    cthr = (cmp_start + L_CMP - 1)[:, None] - jnp.arange(Q_BLK)[None, :]
    per_bh = lambda shape: pl.BlockSpec((1,) + shape, lambda b, i: (b,) + (0,) * len(shape))
    return pl.pallas_call(
        functools.partial(_attn_kernel, seq=seq),
        grid=(bh, seq // Q_BLK),
        in_specs=[pl.BlockSpec((1, GQA, HEAD_DIM, Q_BLK), lambda b, i: (b, 0, 0, i)),
                  per_bh((n_cmp, HEAD_DIM)), per_bh((V_ROWS, n_cmp)),
                  per_bh((seq, HEAD_DIM)), per_bh((V_ROWS, seq)),
                  per_bh((seq, HEAD_DIM)), per_bh((V_ROWS, seq)),
                  pl.BlockSpec((1, GQA, 3, Q_BLK), lambda b, i: (b, 0, 0, i)),
                  pl.BlockSpec((n_sel, n_cmp), lambda b, i: (0, 0)),
                  pl.BlockSpec((n_cmp, Q_BLK), lambda b, i: (0, 0))],
        out_specs=pl.BlockSpec((1, Q_BLK, GQA * HEAD_DIM), lambda b, i: (b // N_KV, i, b % N_KV)),
        out_shape=jax.ShapeDtypeStruct((bsz, seq, N_KV * GQA * HEAD_DIM), F32),
        scratch_shapes=[pltpu.VMEM((n_sel, Q_BLK), F32), pltpu.VMEM((SEL_AHEAD, SEL_SUB, GQA * Q_BLK), F32)],
        compiler_params=_params("parallel", "arbitrary"),
        name="nsa_attention",
    )(qt, kc, vct, ks, vst, kw, vwt, gt, ovt, cthr.astype(jnp.int32))


def _ssm_kernel(u_ref, pt_ref, bq_ref, w_ref, v_ref, a1_ref, a2_ref, o_ref, mt_ref, *, chunks_per_seq):
    width = SSM_CHUNK * SSM_CH
    u = u_ref[0]
    kt = _dot_split(bq_ref[0], pt_ref[0])
    col = lax.broadcasted_iota(jnp.int32, (SSM_CH, width), 1)
    for s in range(SSM_CHUNK):
        shifted = kt if s == 0 else pltpu.roll(kt, SSM_CH * s, axis=1)
        mt_ref[s * SSM_CH:(s + 1) * SSM_CH, :] = jnp.where(col >= SSM_CH * s, shifted, 0.0).astype(BF16)
    y = _dot(u, mt_ref[...])
    x = _dot(u, w_ref[0])
    row = lax.broadcasted_iota(jnp.int32, x.shape, 0) % chunks_per_seq
    steps = int(math.log2(chunks_per_seq))
    for j in range(steps):
        d = 1 << j
        sh = jnp.where(row >= d, pltpu.roll(x, d, axis=0), 0.0)
        x = x + a1_ref[0, j:j + 1, :] * sh + a2_ref[0, j:j + 1, :] * pltpu.roll(sh, SSM_STATE, axis=1)
    prev = jnp.where(row >= 1, pltpu.roll(x, 1, axis=0), 0.0)
    o_ref[0] = y + _dot(prev.astype(BF16), v_ref[0])


def s5_scan(u2, pt, bq, w, v, a1, a2, chunks_per_seq):
    g, n_chunks, width = u2.shape
    per_g = lambda a: pl.BlockSpec((1,) + a.shape[1:], lambda i: (i,) + (0,) * (a.ndim - 1))
    return pl.pallas_call(
        functools.partial(_ssm_kernel, chunks_per_seq=chunks_per_seq),
        grid=(g,),
        in_specs=[per_g(u2), per_g(pt), per_g(bq), per_g(w), per_g(v), per_g(a1), per_g(a2)],
        out_specs=pl.BlockSpec((1, n_chunks, width), lambda i: (i, 0, 0)),
        out_shape=jax.ShapeDtypeStruct((g, n_chunks, width), F32),
        scratch_shapes=[pltpu.VMEM((width, width), BF16)],
        compiler_params=_params("parallel"),
        name="s5_scan",
    )(u2, pt, bq, w, v, a1, a2)


def _s5_tables(log_dt, a_re, a_im, b_re, b_im, c_re, c_im, chunks_per_seq):
    g, n = a_re.shape
    dt = jnp.exp(log_dt)[:, None]
    lam_re, lam_im = dt * a_re, dt * a_im

    def power(k):
        k = k.astype(F32)[None, :, None]
        mag = jnp.exp(k * lam_re[:, None, :])
        return mag * jnp.cos(k * lam_im[:, None, :]), mag * jnp.sin(k * lam_im[:, None, :])

    abar_re, abar_im = (x[:, 0] for x in power(jnp.ones((1,))))
    den = a_re * a_re + a_im * a_im
    nr = abar_re - 1.0
    f_re = (nr * a_re + abar_im * a_im) / den
    f_im = (abar_im * a_re - nr * a_im) / den
    bb_re = f_re[..., None] * b_re - f_im[..., None] * b_im
    bb_im = f_re[..., None] * b_im + f_im[..., None] * b_re
    steps = jnp.arange(SSM_CHUNK)
    width = SSM_CHUNK * SSM_CH
    e_re, e_im = power(steps)
    ca_re = c_re[:, None] * e_re[:, :, None, :] - c_im[:, None] * e_im[:, :, None, :]
    ca_im = c_re[:, None] * e_im[:, :, None, :] + c_im[:, None] * e_re[:, :, None, :]
    pt = jnp.concatenate([ca_re, -ca_im], axis=-1).reshape(g, width, 2 * n).transpose(0, 2, 1)
    bq = jnp.concatenate([bb_re, bb_im], axis=1).transpose(0, 2, 1)
    r_re, r_im = power(SSM_CHUNK - 1 - steps)
    bt_re, bt_im = bb_re.transpose(0, 2, 1)[:, None], bb_im.transpose(0, 2, 1)[:, None]
    w_re = r_re[:, :, None, :] * bt_re - r_im[:, :, None, :] * bt_im
    w_im = r_re[:, :, None, :] * bt_im + r_im[:, :, None, :] * bt_re
    w = jnp.concatenate([w_re, w_im], axis=-1).reshape(g, width, 2 * n).astype(BF16)
    n_re, n_im = power(steps + 1)
    cv_re = c_re[:, None] * n_re[:, :, None, :] - c_im[:, None] * n_im[:, :, None, :]
    cv_im = c_re[:, None] * n_im[:, :, None, :] + c_im[:, None] * n_re[:, :, None, :]
    v = jnp.concatenate([cv_re, -cv_im], axis=-1).reshape(g, width, 2 * n).transpose(0, 2, 1).astype(BF16)
    n_steps = int(math.log2(chunks_per_seq))
    s_re, s_im = power(SSM_CHUNK * (2 ** jnp.arange(max(n_steps, 1))))
    a1 = jnp.concatenate([s_re, s_re], axis=-1)
    a2 = jnp.concatenate([-s_im, s_im], axis=-1)
    return pt, bq, w, v, a1, a2


def _slot_masks(rows):
    slot = lax.broadcasted_iota(jnp.int32, (rows, LANES), 1) // SSM_CH
    return [slot == k for k in range(LANES // SSM_CH)]


def _s5_pack_kernel(*refs):
    u_refs, o_ref = refs[:-1], refs[-1]
    cb = o_ref.shape[1]
    slots = LANES // SSM_CH
    masks = _slot_masks(cb)
    for b, u_ref in enumerate(u_refs):
        for m in range(SSM_CHUNK // slots):
            rot = [pltpu.roll(u_ref[pl.ds(m * slots + j, cb, stride=SSM_CHUNK), :], SSM_CH * j, axis=1)
                   for j in range(slots)]
            for g in range(slots):
                mix = rot[0]
                for j in range(1, slots):
                    mix = jnp.where(masks[(g + j) % slots], rot[j], mix)
                out = mix if g == 0 else pltpu.roll(mix, LANES - SSM_CH * g, axis=1)
                o_ref[b * slots + g, :, m * LANES:(m + 1) * LANES] = out.astype(o_ref.dtype)


def s5_pack(u):
    t, d = u.shape
    n_groups = d // SSM_CH
    n_chunks = t // SSM_CHUNK
    cb = min(PACK_CHUNKS, n_chunks)
    width = SSM_CHUNK * SSM_CH
    return pl.pallas_call(
        _s5_pack_kernel,
        grid=(n_chunks // cb,),
        in_specs=[pl.BlockSpec((cb * SSM_CHUNK, LANES), functools.partial(lambda b, i: (i, b), b))
                  for b in range(d // LANES)],
        out_specs=pl.BlockSpec((n_groups, cb, width), lambda i: (0, i, 0)),
        out_shape=jax.ShapeDtypeStruct((n_groups, n_chunks, width), BF16),
        compiler_params=_params("parallel"),
        name="s5_pack",
    )(*[u] * (d // LANES))


def _s5_unpack_kernel(y_ref, *o_refs):
    cb = y_ref.shape[1]
    slots = LANES // SSM_CH
    masks = _slot_masks(cb)
    for b, o_ref in enumerate(o_refs):
        for m in range(SSM_CHUNK // slots):
            rot = [pltpu.roll(y_ref[b * slots + g, :, m * LANES:(m + 1) * LANES], SSM_CH * g, axis=1)
                   if g else y_ref[b * slots, :, m * LANES:(m + 1) * LANES] for g in range(slots)]
            for j in range(slots):
                mix = rot[0]
                for g in range(1, slots):
                    mix = jnp.where(masks[(g + j) % slots], rot[g], mix)
                out = mix if j == 0 else pltpu.roll(mix, LANES - SSM_CH * j, axis=1)
                o_ref[pl.ds(m * slots + j, cb, stride=SSM_CHUNK), :] = out


def s5_unpack(y2):
    n_groups, n_chunks, width = y2.shape
    d = n_groups * SSM_CH
    cb = min(PACK_CHUNKS, n_chunks)
    return pl.pallas_call(
        _s5_unpack_kernel,
        grid=(n_chunks // cb,),
        in_specs=[pl.BlockSpec((n_groups, cb, width), lambda i: (0, i, 0))],
        out_specs=[pl.BlockSpec((cb * SSM_CHUNK, LANES), lambda i: (i, 0))] * (d // LANES),
        out_shape=[jax.ShapeDtypeStruct((n_chunks * SSM_CHUNK, LANES), F32)] * (d // LANES),
        compiler_params=_params("parallel"),
        name="s5_unpack",
    )(y2)


def _mix_out_kernel(x_ref, oa_ref, u_ref, dsk_ref, wglu_ref, bglu_ref, na_ref, ns_ref, woa_ref, wos_ref,
                    *refs):
    ys_refs, o_ref = refs[:-1], refs[-1]
    d_ssm = u_ref.shape[1]
    y_scan = jnp.concatenate([r[...] for r in ys_refs], axis=1)
    y = jax.nn.gelu(y_scan + dsk_ref[...] * u_ref[...])
    z = _dot(y.astype(BF16), wglu_ref[...]) + bglu_ref[...]
    o_ssm = z[:, :d_ssm] * jax.nn.sigmoid(z[:, d_ssm:])
    att_n = _rms(oa_ref[...], na_ref[...]).astype(BF16)
    ssm_n = _rms(o_ssm, ns_ref[...]).astype(BF16)
    o_ref[...] = x_ref[...] + _dot(att_n, woa_ref[...]) + _dot(ssm_n, wos_ref[...])


def mix_out(x2, o_att, y_ssm, u, d_skip, w_glu, b_glu, n_att, n_ssm, w_out):
    t, d = x2.shape
    d_att = o_att.shape[1]
    d_ssm = u.shape[1]
    tm = _row_tile(t)
    rows = lambda w: pl.BlockSpec((tm, w), lambda i: (i, 0))
    full = lambda a: pl.BlockSpec(a.shape, lambda i: (0,) * a.ndim)
    consts = [d_skip.reshape(1, d_ssm), w_glu.astype(BF16), b_glu.reshape(1, 2 * d_ssm),
              n_att.reshape(1, d_att), n_ssm.reshape(1, d_ssm),
              w_out[:d_att].astype(BF16), w_out[d_att:].astype(BF16)]
    return pl.pallas_call(
        _mix_out_kernel,
        grid=(t // tm,),
        in_specs=[rows(d), rows(d_att), rows(d_ssm)] + [full(c) for c in consts] + [rows(LANES)] * len(y_ssm),
        out_specs=rows(d),
        out_shape=jax.ShapeDtypeStruct((t, d), F32),
        compiler_params=_params("parallel"),
        name="mix_out",
    )(x2, o_att, u, *consts, *y_ssm)


def _ffn_kernel(x_ref, g_ref, wg_ref, wu_ref, wd_ref, fg_ref, o_ref, acc_ref, *, f_chunk, final_norm):
    x = x_ref[...]
    h = _rms(x, g_ref[...]).astype(BF16)
    d_ff = wg_ref.shape[1]
    acc_ref[...] = x
    for c in range(d_ff // f_chunk):
        sl = slice(c * f_chunk, (c + 1) * f_chunk)
        a = jax.nn.silu(_dot(h, wg_ref[:, sl])) * _dot(h, wu_ref[:, sl])
        acc_ref[...] += _dot(a.astype(BF16), wd_ref[sl, :])
    out = acc_ref[...]
    if final_norm:
        out = _rms(out, fg_ref[...])
    o_ref[...] = out


def ffn(x2, g, w_gate, w_up, w_down, final_g, final_norm):
    t, d = x2.shape
    d_ff = w_gate.shape[1]
    tm = _row_tile(t)
    f_chunk = 256 if d_ff % 256 == 0 else d_ff
    full = lambda a: pl.BlockSpec(a.shape, lambda i: (0,) * a.ndim)
    consts = [g.reshape(1, d), w_gate.astype(BF16), w_up.astype(BF16), w_down.astype(BF16),
              final_g.reshape(1, d)]
    return pl.pallas_call(
        functools.partial(_ffn_kernel, f_chunk=f_chunk, final_norm=final_norm),
        grid=(t // tm,),
        in_specs=[pl.BlockSpec((tm, d), lambda i: (i, 0))] + [full(c) for c in consts],
        out_specs=pl.BlockSpec((tm, d), lambda i: (i, 0)),
        out_shape=jax.ShapeDtypeStruct((t, d), F32),
        scratch_shapes=[pltpu.VMEM((tm, d), F32)],
        compiler_params=_params("parallel"),
        name="ffn",
    )(x2, *consts)


def _mixer(x2, bsz, seq, attn_norm_g, w_in, cmp_pe, cmp_w1, cmp_b1, cmp_w2, cmp_b2, log_dt, a_re, a_im,
           b_re, b_im, c_re, c_im, d_skip, w_glu, b_glu, mix_norm_att, mix_norm_ssm, w_out):
    t, d = x2.shape
    d_ssm = d - D_ATT
    bh = bsz * N_KV
    qt, vst, vwt, gt, kc, vc, ks, kw, u = in_proj(x2.reshape(bsz, seq, d), attn_norm_g, w_in)
    k_c = compress(kc, cmp_pe[0], cmp_w1[0], cmp_b1[0], cmp_w2[0], cmp_b2[0], transposed=False)
    v_ct = compress(vc, cmp_pe[1], cmp_w1[1], cmp_b1[1], cmp_w2[1], cmp_b2[1], transposed=True)
    o_att = nsa_attention(qt.reshape(bh, GQA, HEAD_DIM, seq), k_c, v_ct,
                          ks.reshape(bh, seq, HEAD_DIM), vst.reshape(bh, V_ROWS, seq),
                          kw.reshape(bh, seq, HEAD_DIM), vwt.reshape(bh, V_ROWS, seq),
                          gt.reshape(bh, GQA, 3, seq), bsz).reshape(t, D_ATT)

    u = u.reshape(t, d_ssm)
    chunks_per_seq = seq // SSM_CHUNK
    tables = _s5_tables(log_dt, a_re, a_im, b_re, b_im, c_re, c_im, chunks_per_seq)
    y_ssm = s5_unpack(s5_scan(s5_pack(u), *tables, chunks_per_seq))

    return mix_out(x2, o_att, y_ssm, u, d_skip, w_glu, b_glu, mix_norm_att, mix_norm_ssm, w_out)


def kernel(x, attn_norm_g, w_in, cmp_pe, cmp_w1, cmp_b1, cmp_w2, cmp_b2, log_dt, a_re, a_im, b_re, b_im, c_re, c_im, d_skip, w_glu, b_glu, mix_norm_att, mix_norm_ssm, w_out, ffn_norm_g, w_gate, w_up, w_down, final_norm_g):
    bsz, seq, d = x.shape
    depth = w_in.shape[0]
    x2 = x.reshape(bsz * seq, d)
    for l in range(depth):
        x2 = _mixer(x2, bsz, seq, attn_norm_g[l], w_in[l], cmp_pe[l], cmp_w1[l], cmp_b1[l], cmp_w2[l],
                    cmp_b2[l], log_dt[l], a_re[l], a_im[l], b_re[l], b_im[l], c_re[l], c_im[l], d_skip[l],
                    w_glu[l], b_glu[l], mix_norm_att[l], mix_norm_ssm[l], w_out[l])
        x2 = ffn(x2, ffn_norm_g[l], w_gate[l], w_up[l], w_down[l], final_norm_g, final_norm=(l == depth - 1))
    return x2.reshape(bsz, seq, d)
```

```python
import functools
import math

import jax
import jax.numpy as jnp
from jax import lax
from jax.experimental import pallas as pl
from jax.experimental.pallas import tpu as pltpu

HEAD_DIM = 64
N_KV = 3
GQA = 4
N_HEADS = N_KV * GQA
D_ATT = N_HEADS * HEAD_DIM
SSM_CH = 16
SSM_STATE = 64
L_CMP = 32
STRIDE_CMP = 16
CMP_HID = 256
L_SEL = 64
TOP_K = 16
N_FORCED = 3
WINDOW = 512
Q_BLK = 128
TINY = 1e-30
EPS = 1e-6
LOG2E = 1.4426950408889634

SUBLANES = 8
LANES = 128
V_ROWS = HEAD_DIM + 2 * SUBLANES
SSM_CHUNK = 64
PACK_CHUNKS = 64
SEL_SUB = 256
SEL_LOOP = 32
SEL_DIAG = 16
FRONT_PARTS = 4
SEL_AHEAD = 3
VMEM_LIMIT = 56 * 1024 * 1024

F32 = jnp.float32
BF16 = jnp.bfloat16
NEG_INF = float("-inf")


def _dot(a, b):
    return jnp.dot(a, b, preferred_element_type=F32)


def _dot_nt(a, b):
    return lax.dot_general(a, b, (((1,), (1,)), ((), ())), preferred_element_type=F32)


def _dot_split(a, b):
    a_hi = a.astype(BF16)
    a_lo = (a - a_hi.astype(F32)).astype(BF16)
    b_hi = b.astype(BF16)
    b_lo = (b - b_hi.astype(F32)).astype(BF16)
    return _dot(a_hi, b_hi) + _dot(a_hi, b_lo) + _dot(a_lo, b_hi)


def _params(*sem):
    return pltpu.CompilerParams(dimension_semantics=sem, vmem_limit_bytes=VMEM_LIMIT)


def _row_tile(t):
    for tm in (512, 256, 128, 64, 32, 16, 8):
        if t % tm == 0:
            return tm
    raise ValueError(f"token count {t} must be a multiple of 8")


def _rms(x, g):
    ms = jnp.mean(x * x, axis=-1, keepdims=True)
    return x * lax.rsqrt(ms + EPS) * g


def _in_proj_kernel(x_ref, g_ref, wn_ref, wt_ref, qt_ref, vst_ref, vwt_ref, gt_ref, kc_ref, vc_ref,
                    ks_ref, kw_ref, u_ref):
    p_kv = N_KV * HEAD_DIM
    h = _rms(x_ref[0], g_ref[...]).astype(BF16)
    tm = h.shape[0]
    nat = _dot(h, wn_ref[...])
    tr = _dot_nt(wt_ref[...], h)
    qt_ref[0] = (tr[:D_ATT] * (LOG2E * HEAD_DIM ** -0.5)).astype(BF16)
    ones_rows = (lax.broadcasted_iota(jnp.int32, (V_ROWS - HEAD_DIM, tm), 0) == 0).astype(BF16)
    for hd in range(N_KV):
        rows = slice(hd * HEAD_DIM, (hd + 1) * HEAD_DIM)
        for k, ref in enumerate((vst_ref, vwt_ref)):
            off = D_ATT + k * p_kv
            ref[0, hd, :HEAD_DIM, :] = tr[off + hd * HEAD_DIM:off + (hd + 1) * HEAD_DIM].astype(BF16)
            ref[0, hd, HEAD_DIM:, :] = ones_rows
        kc_ref[0, hd] = nat[:, rows]
        vc_ref[0, hd] = nat[:, p_kv + hd * HEAD_DIM:p_kv + (hd + 1) * HEAD_DIM]
        ks_ref[0, hd] = nat[:, 2 * p_kv + hd * HEAD_DIM:2 * p_kv + (hd + 1) * HEAD_DIM].astype(BF16)
        kw_ref[0, hd] = nat[:, 3 * p_kv + hd * HEAD_DIM:3 * p_kv + (hd + 1) * HEAD_DIM].astype(BF16)
    gt_ref[0] = tr[D_ATT + 2 * p_kv:D_ATT + 2 * p_kv + 3 * N_HEADS]
    u_ref[0] = nat[:, 4 * p_kv:]


def in_proj(x3, g, w_in):
    bsz, seq, d = x3.shape
    p_kv = N_KV * HEAD_DIM
    p_gate = 3 * N_HEADS
    d_ssm = d - D_ATT
    o = [0, D_ATT]
    for width in [p_kv] * 6 + [p_gate, d_ssm]:
        o.append(o[-1] + width)
    q, kc, vc, ksl, vsl, kwn, vwn, gl, u = (w_in[:, o[i]:o[i + 1]] for i in range(9))
    w_nat = jnp.concatenate([kc, vc, ksl, kwn, u], axis=1).astype(BF16)
    n_tr = D_ATT + 2 * p_kv + p_gate
    w_tr = jnp.pad(jnp.concatenate([q, vsl, vwn, gl], axis=1), ((0, 0), (0, -n_tr % SUBLANES))).T.astype(BF16)
    tm = _row_tile(seq)
    full = lambda a: pl.BlockSpec(a.shape, lambda b, i: (0,) * a.ndim)
    feat = lambda rows: pl.BlockSpec((1, rows, tm), lambda b, i: (b, 0, i))
    vals = pl.BlockSpec((1, N_KV, V_ROWS, tm), lambda b, i: (b, 0, 0, i))
    keys = pl.BlockSpec((1, N_KV, tm, HEAD_DIM), lambda b, i: (b, 0, i, 0))
    sds = jax.ShapeDtypeStruct
    g2 = g.reshape(1, d)
    return pl.pallas_call(
        _in_proj_kernel,
        grid=(bsz, seq // tm),
        in_specs=[pl.BlockSpec((1, tm, d), lambda b, i: (b, i, 0)), full(g2), full(w_nat), full(w_tr)],
        out_specs=[feat(D_ATT), vals, vals, feat(p_gate), keys, keys, keys, keys,
                   pl.BlockSpec((1, tm, d_ssm), lambda b, i: (b, i, 0))],
        out_shape=[sds((bsz, D_ATT, seq), BF16), sds((bsz, N_KV, V_ROWS, seq), BF16),
                   sds((bsz, N_KV, V_ROWS, seq), BF16), sds((bsz, p_gate, seq), F32),
                   sds((bsz, N_KV, seq, HEAD_DIM), F32), sds((bsz, N_KV, seq, HEAD_DIM), F32),
                   sds((bsz, N_KV, seq, HEAD_DIM), BF16), sds((bsz, N_KV, seq, HEAD_DIM), BF16),
                   sds((bsz, seq, d_ssm), F32)],
        compiler_params=_params("parallel", "parallel"),
        name="in_proj",
    )(x3, g2, w_nat, w_tr)


def _compress_kernel(x_ref, pe_ref, w1_ref, b1_ref, w2_ref, b2_ref, o_ref, *, transposed):
    seq = x_ref.shape[2]
    n = seq // STRIDE_CMP
    first = jnp.zeros((n, CMP_HID), F32)
    second = jnp.zeros((n, CMP_HID), F32)
    for j in range(STRIDE_CMP):
        tok = x_ref[0, 0, pl.ds(j, n, stride=STRIDE_CMP), :]
        w_a = w1_ref[j * HEAD_DIM:(j + 1) * HEAD_DIM, :]
        w_b = w1_ref[(STRIDE_CMP + j) * HEAD_DIM:(STRIDE_CMP + j + 1) * HEAD_DIM, :]
        first = first + _dot((tok + pe_ref[j:j + 1, :]).astype(BF16), w_a)
        second = second + _dot((tok + pe_ref[STRIDE_CMP + j:STRIDE_CMP + j + 1, :]).astype(BF16), w_b)
    hid = first + pltpu.roll(second, n - 1, axis=0) + b1_ref[...]
    hid = jax.nn.gelu(hid).astype(BF16)
    if transposed:
        o_ref[0] = (_dot_nt(w2_ref[...], hid) + b2_ref[...]).astype(o_ref.dtype)
    else:
        o_ref[0] = (_dot(hid, w2_ref[...]) + b2_ref[...]).astype(o_ref.dtype)


def compress(x, pe, w1, b1, w2, b2, transposed):
    bsz, n_kv, seq, _ = x.shape
    bh = bsz * n_kv
    n = seq // STRIDE_CMP
    w1k = w1.astype(BF16)
    if transposed:
        pad = V_ROWS - HEAD_DIM
        w2k = jnp.pad(w2.T, ((0, pad), (0, 0))).astype(BF16)
        b2k = jnp.concatenate([b2, jnp.ones((1,), F32), jnp.zeros((pad - 1,), F32)]).reshape(V_ROWS, 1)
        out_block, out_shape = (1, V_ROWS, n), (bh, V_ROWS, n)
    else:
        w2k = w2.astype(BF16)
        b2k = b2.reshape(1, HEAD_DIM)
        out_block, out_shape = (1, n, HEAD_DIM), (bh, n, HEAD_DIM)
    full = lambda a: pl.BlockSpec(a.shape, lambda i: (0,) * a.ndim)
    b1k = b1.reshape(1, CMP_HID)
    return pl.pallas_call(
        functools.partial(_compress_kernel, transposed=transposed),
        grid=(bh,),
        in_specs=[pl.BlockSpec((1, 1, seq, HEAD_DIM), lambda i: (i // n_kv, i % n_kv, 0, 0)),
                  full(pe), full(w1k), full(b1k), full(w2k), full(b2k)],
        out_specs=pl.BlockSpec(out_block, lambda i: (i, 0, 0)),
        out_shape=jax.ShapeDtypeStruct(out_shape, BF16),
        compiler_params=_params("parallel"),
        name="compress_t" if transposed else "compress_n",
    )(x, pe, w1k, b1k, w2k, b2k)


def _heads(x):
    return [x[:, g * Q_BLK:(g + 1) * Q_BLK] for g in range(GQA)]


def _exp2_safe_max(m):
    return jnp.where(m == NEG_INF, 0.0, m)


def _aligned(x, m):
    return x if isinstance(x, int) else pl.multiple_of(x, m)


def _normalise(pv):
    return pv[:HEAD_DIM] / jnp.maximum(pv[HEAD_DIM:HEAD_DIM + 1], TINY)


def _attn_kernel(qt_ref, kc_ref, vct_ref, ks_ref, vst_ref, kw_ref, vwt_ref, gt_ref, ovt_ref, cthr_ref,
                 o_ref, selb_ref, s_ref, *, seq):
    qi = pl.program_id(1)
    n_sel = seq // L_SEL
    lanes = GQA * Q_BLK
    win_keys = WINDOW + Q_BLK

    qt = jnp.concatenate([qt_ref[0, g] for g in range(GQA)], axis=1)
    q0 = qi * Q_BLK
    lane_q = lax.broadcasted_iota(jnp.int32, (1, Q_BLK), 1)
    t_q = q0 + lane_q
    t_all = jnp.concatenate([t_q] * GQA, axis=1)

    def sel_scores(k0):
        return _dot(ks_ref[0, pl.ds(k0, SEL_SUB), :], qt)

    def front(n_rows, full_band, n_static):
        n_blk = n_rows * STRIDE_CMP // L_SEL
        s_c = _dot(kc_ref[0, :n_rows, :], qt)
        for a in range(SEL_AHEAD):
            s_ref[a] = sel_scores(a * SEL_SUB)
        vis = cthr_ref[:n_rows, :] <= q0
        p_heads = []
        for s_g in _heads(s_c):
            s_g = jnp.where(vis, s_g, NEG_INF)
            m_g = _exp2_safe_max(jnp.max(s_g, axis=0, keepdims=True))
            p_heads.append(jnp.exp2(s_g - m_g).astype(BF16))
        p_c = jnp.concatenate(p_heads, axis=1)
        pv_c = _dot(vct_ref[0, :, :n_rows], p_c)
        r_c = 1.0 / jnp.maximum(pv_c[HEAD_DIM:HEAD_DIM + 1], TINY)
        o_c = pv_c[:HEAD_DIM] * r_c
        imp_all = _dot(ovt_ref[:n_blk, :n_rows], p_c) * r_c
        imp = functools.reduce(jnp.add, _heads(imp_all))

        blk = lax.broadcasted_iota(jnp.int32, (n_blk, Q_BLK), 0).astype(F32)
        cur = (t_q // L_SEL).astype(F32)
        forced = (blk == 0.0) | (blk == cur) | (blk == cur - 1.0)
        free = (blk <= cur) & jnp.logical_not(forced)
        cand = jnp.where(free, imp, NEG_INF)
        n_pick = min(TOP_K, n_sel) - N_FORCED

        def rank_exact(val):
            for _ in range(n_pick):
                mx = jnp.max(val, axis=0, keepdims=True)
                first = jnp.min(jnp.where(val == mx, blk, float(n_blk)), axis=0, keepdims=True)
                val = jnp.where(blk == first, NEG_INF, val)
            return val

        val_fast = cand
        for _ in range(n_pick):
            val_fast = jnp.where(val_fast == jnp.max(val_fast, axis=0, keepdims=True), NEG_INF, val_fast)
        n_free = jnp.sum(free.astype(F32), axis=0, keepdims=True)
        n_got = jnp.sum((free & (val_fast == NEG_INF)).astype(F32), axis=0, keepdims=True)
        tied = jnp.max(jnp.where(n_got != jnp.minimum(n_free, float(n_pick)), 1.0, 0.0)) > 0.0

        start = pl.multiple_of(jnp.maximum(q0 - WINDOW, 0), Q_BLK)
        s_w = _dot(kw_ref[0, pl.ds(start, win_keys), :], qt)
        r_minus_c = (lax.broadcasted_iota(jnp.int32, (Q_BLK, Q_BLK), 0)
                     - lax.broadcasted_iota(jnp.int32, (Q_BLK, Q_BLK), 1))
        chunks = []
        n_chunks = win_keys // Q_BLK
        for i in range(n_chunks):
            rows = s_w[i * Q_BLK:(i + 1) * Q_BLK]
            if not full_band or i in (0, n_chunks - 1):
                hi = q0 - start - i * Q_BLK
                mask = (r_minus_c <= hi) & (r_minus_c > hi - WINDOW)
                rows = jnp.concatenate([jnp.where(mask, x, NEG_INF) for x in _heads(rows)], axis=1)
            chunks.append(rows)
        m_w = _exp2_safe_max(functools.reduce(jnp.maximum, [jnp.max(x, axis=0, keepdims=True) for x in chunks]))
        p_w = jnp.concatenate([jnp.exp2(x - m_w).astype(BF16) for x in chunks], axis=0)
        o_w = _normalise(_dot(vwt_ref[0, :, pl.ds(start, win_keys)], p_w))
        val = lax.cond(tied, lambda: rank_exact(cand), lambda: val_fast)
        selb_ref[:n_blk, :] = jnp.where(forced | (free & (val == NEG_INF)), 0.0, NEG_INF)
        carry = (jnp.full((1, lanes), NEG_INF, F32), jnp.zeros((V_ROWS, lanes), F32))
        if n_static:
            carry = sel_run(0, *carry, n_static, False)
        return (o_c, o_w) + tuple(carry)

    row_iota = lax.broadcasted_iota(jnp.int32, (L_SEL, lanes), 0)

    def sel_update(s_t, k0, m_prev, acc_prev, causal):
        blk0 = k0 // L_SEL
        blocks, biases, part_max = [], [], None
        for j in range(SEL_SUB // L_SEL):
            bias = jnp.concatenate([selb_ref[pl.ds(blk0 + j, 1), :]] * GQA, axis=1)
            s_b = s_t[j * L_SEL:(j + 1) * L_SEL]
            if causal:
                s_b = jnp.where(k0 + j * L_SEL + row_iota <= t_all, s_b, NEG_INF)
            s8 = functools.reduce(jnp.maximum, [s_b[r:r + SUBLANES] for r in range(0, L_SEL, SUBLANES)])
            cand = s8 + bias
            part_max = cand if part_max is None else jnp.maximum(part_max, cand)
            blocks.append(s_b)
            biases.append(bias)
        m_new = jnp.maximum(m_prev, jnp.max(part_max, axis=0, keepdims=True))
        m_safe = _exp2_safe_max(m_new)
        alpha = jnp.exp2(m_prev - m_safe)
        p = jnp.concatenate([jnp.exp2(s_b - (m_safe - bias)).astype(BF16)
                             for s_b, bias in zip(blocks, biases)], axis=0)
        pv = _dot(vst_ref[0, :, pl.ds(k0, SEL_SUB)], p)
        return m_new, alpha * acc_prev + pv

    def sel_run(k0, m_prev, acc_prev, n_sub, diagonal):
        queue = [s_ref[a] for a in range(min(SEL_AHEAD, n_sub) if diagonal else SEL_AHEAD)]
        for i in range(n_sub):
            ahead = i + SEL_AHEAD
            if not diagonal or ahead < n_sub:
                queue.append(sel_scores(_aligned(k0 + ahead * SEL_SUB, SEL_SUB)))
            m_prev, acc_prev = sel_update(queue.pop(0), _aligned(k0 + i * SEL_SUB, SEL_SUB),
                                          m_prev, acc_prev, causal=diagonal and i == n_sub - 1)
        if not diagonal:
            for a in range(SEL_AHEAD):
                s_ref[a] = queue[a]
        return m_prev, acc_prev

    n_cmp = seq // STRIDE_CMP
    parts = FRONT_PARTS if n_cmp % (FRONT_PARTS * Q_BLK) == 0 else 1
    part_subs = seq // parts // SEL_SUB
    part = q0 // (seq // parts)
    o_c, o_w, *carry = lax.switch(
        part, [functools.partial(front, n_cmp * (k + 1) // parts, k * (seq // parts) >= WINDOW, k * part_subs)
               for k in range(parts)])

    n_below = q0 // SEL_SUB
    done = part * part_subs
    if part_subs > SEL_LOOP:
        trips = (n_below - done) // SEL_LOOP
        carry = lax.fori_loop(
            0, trips, lambda kt, c: sel_run(_aligned((done + kt * SEL_LOOP) * SEL_SUB, SEL_SUB), *c, SEL_LOOP, False),
            tuple(carry))
        done = done + trips * SEL_LOOP
    run = min(SEL_LOOP, part_subs) // 2
    while run >= SEL_DIAG:
        take = ((n_below - done) // run) == 1
        k_run = _aligned(done * SEL_SUB, SEL_SUB)
        carry = lax.cond(take, functools.partial(lambda k, n, m, a: sel_run(k, m, a, n, False), k_run, run),
                         lambda m, a: (m, a), *carry)
        done = done + jnp.where(take, run, 0)
        run //= 2
    k_last = _aligned(done * SEL_SUB, SEL_SUB)

    def finish(n_diag, m_s, acc_s):
        o_s = _normalise(sel_run(k_last, m_s, acc_s, n_diag, True)[1])
        outs = []
        for g in range(GQA):
            sl = slice(g * Q_BLK, (g + 1) * Q_BLK)
            gate = jax.nn.sigmoid(gt_ref[0, g])
            outs.append(gate[0:1] * o_c[:, sl] + gate[1:2] * o_s[:, sl] + gate[2:3] * o_w[:, sl])
        pairs = [jnp.concatenate(outs[g:g + 2], axis=0).T for g in range(0, GQA, 2)]
        o_ref[0] = jnp.concatenate(pairs, axis=1)

    lax.switch(n_below - done, [functools.partial(finish, n + 1) for n in range(SEL_DIAG)], *carry)


def nsa_attention(qt, kc, vct, ks, vst, kw, vwt, gt, bsz):
    bh, _, _, seq = qt.shape
    assert SEL_LOOP % SEL_DIAG == 0 and seq % (SEL_DIAG * SEL_SUB) == 0 and seq >= WINDOW + Q_BLK
    n_cmp = seq // STRIDE_CMP
    n_sel = seq // L_SEL
    cmp_start = jnp.arange(n_cmp) * STRIDE_CMP
    sel_start = jnp.arange(n_sel) * L_SEL
    ovt = ((cmp_start[None, :] < sel_start[:, None] + L_SEL)
           & (cmp_start[None, :] + L_CMP > sel_start[:, None])).astype(BF16)
    cthr = (cmp_start + L_CMP - 1)[:, None] - jnp.arange(Q_BLK)[None, :]
    per_bh = lambda shape: pl.BlockSpec((1,) + shape, lambda b, i: (b,) + (0,) * len(shape))
    return pl.pallas_call(
        functools.partial(_attn_kernel, seq=seq),
        grid=(bh, seq // Q_BLK),
        in_specs=[pl.BlockSpec((1, GQA, HEAD_DIM, Q_BLK), lambda b, i: (b, 0, 0, i)),
                  per_bh((n_cmp, HEAD_DIM)), per_bh((V_ROWS, n_cmp)),
                  per_bh((seq, HEAD_DIM)), per_bh((V_ROWS, seq)),
                  per_bh((seq, HEAD_DIM)), per_bh((V_ROWS, seq)),
                  pl.BlockSpec((1, GQA, 3, Q_BLK), lambda b, i: (b, 0, 0, i)),
                  pl.BlockSpec((n_sel, n_cmp), lambda b, i: (0, 0)),
                  pl.BlockSpec((n_cmp, Q_BLK), lambda b, i: (0, 0))],
        out_specs=pl.BlockSpec((1, Q_BLK, GQA * HEAD_DIM), lambda b, i: (b // N_KV, i, b % N_KV)),
        out_shape=jax.ShapeDtypeStruct((bsz, seq, N_KV * GQA * HEAD_DIM), F32),
        scratch_shapes=[pltpu.VMEM((n_sel, Q_BLK), F32), pltpu.VMEM((SEL_AHEAD, SEL_SUB, GQA * Q_BLK), F32)],
        compiler_params=_params("parallel", "arbitrary"),
        name="nsa_attention",
    )(qt, kc, vct, ks, vst, kw, vwt, gt, ovt, cthr.astype(jnp.int32))


def _ssm_kernel(u_ref, pt_ref, bq_ref, w_ref, v_ref, a1_ref, a2_ref, o_ref, mt_ref, *, chunks_per_seq):
    width = SSM_CHUNK * SSM_CH
    u = u_ref[0]
    kt = _dot_split(bq_ref[0], pt_ref[0])
    col = lax.broadcasted_iota(jnp.int32, (SSM_CH, width), 1)
    for s in range(SSM_CHUNK):
        shifted = kt if s == 0 else pltpu.roll(kt, SSM_CH * s, axis=1)
        mt_ref[s * SSM_CH:(s + 1) * SSM_CH, :] = jnp.where(col >= SSM_CH * s, shifted, 0.0).astype(BF16)
    y = _dot(u, mt_ref[...])
    x = _dot(u, w_ref[0])
    row = lax.broadcasted_iota(jnp.int32, x.shape, 0) % chunks_per_seq
    steps = int(math.log2(chunks_per_seq))
    for j in range(steps):
        d = 1 << j
        sh = jnp.where(row >= d, pltpu.roll(x, d, axis=0), 0.0)
        x = x + a1_ref[0, j:j + 1, :] * sh + a2_ref[0, j:j + 1, :] * pltpu.roll(sh, SSM_STATE, axis=1)
    prev = jnp.where(row >= 1, pltpu.roll(x, 1, axis=0), 0.0)
    o_ref[0] = y + _dot(prev.astype(BF16), v_ref[0])


def s5_scan(u2, pt, bq, w, v, a1, a2, chunks_per_seq):
    g, n_chunks, width = u2.shape
    per_g = lambda a: pl.BlockSpec((1,) + a.shape[1:], lambda i: (i,) + (0,) * (a.ndim - 1))
    return pl.pallas_call(
        functools.partial(_ssm_kernel, chunks_per_seq=chunks_per_seq),
        grid=(g,),
        in_specs=[per_g(u2), per_g(pt), per_g(bq), per_g(w), per_g(v), per_g(a1), per_g(a2)],
        out_specs=pl.BlockSpec((1, n_chunks, width), lambda i: (i, 0, 0)),
        out_shape=jax.ShapeDtypeStruct((g, n_chunks, width), F32),
        scratch_shapes=[pltpu.VMEM((width, width), BF16)],
        compiler_params=_params("parallel"),
        name="s5_scan",
    )(u2, pt, bq, w, v, a1, a2)


def _s5_tables(log_dt, a_re, a_im, b_re, b_im, c_re, c_im, chunks_per_seq):
    g, n = a_re.shape
    dt = jnp.exp(log_dt)[:, None]
    lam_re, lam_im = dt * a_re, dt * a_im

    def power(k):
        k = k.astype(F32)[None, :, None]
        mag = jnp.exp(k * lam_re[:, None, :])
        return mag * jnp.cos(k * lam_im[:, None, :]), mag * jnp.sin(k * lam_im[:, None, :])

    abar_re, abar_im = (x[:, 0] for x in power(jnp.ones((1,))))
    den = a_re * a_re + a_im * a_im
    nr = abar_re - 1.0
    f_re = (nr * a_re + abar_im * a_im) / den
    f_im = (abar_im * a_re - nr * a_im) / den
    bb_re = f_re[..., None] * b_re - f_im[..., None] * b_im
    bb_im = f_re[..., None] * b_im + f_im[..., None] * b_re
    steps = jnp.arange(SSM_CHUNK)
    width = SSM_CHUNK * SSM_CH
    e_re, e_im = power(steps)
    ca_re = c_re[:, None] * e_re[:, :, None, :] - c_im[:, None] * e_im[:, :, None, :]
    ca_im = c_re[:, None] * e_im[:, :, None, :] + c_im[:, None] * e_re[:, :, None, :]
    pt = jnp.concatenate([ca_re, -ca_im], axis=-1).reshape(g, width, 2 * n).transpose(0, 2, 1)
    bq = jnp.concatenate([bb_re, bb_im], axis=1).transpose(0, 2, 1)
    r_re, r_im = power(SSM_CHUNK - 1 - steps)
    bt_re, bt_im = bb_re.transpose(0, 2, 1)[:, None], bb_im.transpose(0, 2, 1)[:, None]
    w_re = r_re[:, :, None, :] * bt_re - r_im[:, :, None, :] * bt_im
    w_im = r_re[:, :, None, :] * bt_im + r_im[:, :, None, :] * bt_re
    w = jnp.concatenate([w_re, w_im], axis=-1).reshape(g, width, 2 * n).astype(BF16)
    n_re, n_im = power(steps + 1)
    cv_re = c_re[:, None] * n_re[:, :, None, :] - c_im[:, None] * n_im[:, :, None, :]
    cv_im = c_re[:, None] * n_im[:, :, None, :] + c_im[:, None] * n_re[:, :, None, :]
    v = jnp.concatenate([cv_re, -cv_im], axis=-1).reshape(g, width, 2 * n).transpose(0, 2, 1).astype(BF16)
    n_steps = int(math.log2(chunks_per_seq))
    s_re, s_im = power(SSM_CHUNK * (2 ** jnp.arange(max(n_steps, 1))))
    a1 = jnp.concatenate([s_re, s_re], axis=-1)
    a2 = jnp.concatenate([-s_im, s_im], axis=-1)
    return pt, bq, w, v, a1, a2


def _slot_masks(rows):
    slot = lax.broadcasted_iota(jnp.int32, (rows, LANES), 1) // SSM_CH
    return [slot == k for k in range(LANES // SSM_CH)]


def _s5_pack_kernel(*refs):
    u_refs, o_ref = refs[:-1], refs[-1]
    cb = o_ref.shape[1]
    slots = LANES // SSM_CH
    masks = _slot_masks(cb)
    for b, u_ref in enumerate(u_refs):
        for m in range(SSM_CHUNK // slots):
            rot = [pltpu.roll(u_ref[pl.ds(m * slots + j, cb, stride=SSM_CHUNK), :], SSM_CH * j, axis=1)
                   for j in range(slots)]
            for g in range(slots):
                mix = rot[0]
                for j in range(1, slots):
                    mix = jnp.where(masks[(g + j) % slots], rot[j], mix)
                out = mix if g == 0 else pltpu.roll(mix, LANES - SSM_CH * g, axis=1)
                o_ref[b * slots + g, :, m * LANES:(m + 1) * LANES] = out.astype(o_ref.dtype)


def s5_pack(u):
    t, d = u.shape
    n_groups = d // SSM_CH
    n_chunks = t // SSM_CHUNK
    cb = min(PACK_CHUNKS, n_chunks)
    width = SSM_CHUNK * SSM_CH
    return pl.pallas_call(
        _s5_pack_kernel,
        grid=(n_chunks // cb,),
        in_specs=[pl.BlockSpec((cb * SSM_CHUNK, LANES), functools.partial(lambda b, i: (i, b), b))
                  for b in range(d // LANES)],
        out_specs=pl.BlockSpec((n_groups, cb, width), lambda i: (0, i, 0)),
        out_shape=jax.ShapeDtypeStruct((n_groups, n_chunks, width), BF16),
        compiler_params=_params("parallel"),
        name="s5_pack",
    )(*[u] * (d // LANES))


def _s5_unpack_kernel(y_ref, *o_refs):
    cb = y_ref.shape[1]
    slots = LANES // SSM_CH
    masks = _slot_masks(cb)
    for b, o_ref in enumerate(o_refs):
        for m in range(SSM_CHUNK // slots):
            rot = [pltpu.roll(y_ref[b * slots + g, :, m * LANES:(m + 1) * LANES], SSM_CH * g, axis=1)
                   if g else y_ref[b * slots, :, m * LANES:(m + 1) * LANES] for g in range(slots)]
            for j in range(slots):
                mix = rot[0]
                for g in range(1, slots):
                    mix = jnp.where(masks[(g + j) % slots], rot[g], mix)
                out = mix if j == 0 else pltpu.roll(mix, LANES - SSM_CH * j, axis=1)
                o_ref[pl.ds(m * slots + j, cb, stride=SSM_CHUNK), :] = out


def s5_unpack(y2):
    n_groups, n_chunks, width = y2.shape
    d = n_groups * SSM_CH
    cb = min(PACK_CHUNKS, n_chunks)
    return pl.pallas_call(
        _s5_unpack_kernel,
        grid=(n_chunks // cb,),
        in_specs=[pl.BlockSpec((n_groups, cb, width), lambda i: (0, i, 0))],
        out_specs=[pl.BlockSpec((cb * SSM_CHUNK, LANES), lambda i: (i, 0))] * (d // LANES),
        out_shape=[jax.ShapeDtypeStruct((n_chunks * SSM_CHUNK, LANES), F32)] * (d // LANES),
        compiler_params=_params("parallel"),
        name="s5_unpack",
    )(y2)


def _mix_out_kernel(x_ref, oa_ref, u_ref, dsk_ref, wglu_ref, bglu_ref, na_ref, ns_ref, woa_ref, wos_ref,
                    *refs):
    ys_refs, o_ref = refs[:-1], refs[-1]
    d_ssm = u_ref.shape[1]
    y_scan = jnp.concatenate([r[...] for r in ys_refs], axis=1)
    y = jax.nn.gelu(y_scan + dsk_ref[...] * u_ref[...])
    z = _dot(y.astype(BF16), wglu_ref[...]) + bglu_ref[...]
    o_ssm = z[:, :d_ssm] * jax.nn.sigmoid(z[:, d_ssm:])
    att_n = _rms(oa_ref[...], na_ref[...]).astype(BF16)
    ssm_n = _rms(o_ssm, ns_ref[...]).astype(BF16)
    o_ref[...] = x_ref[...] + _dot(att_n, woa_ref[...]) + _dot(ssm_n, wos_ref[...])


def mix_out(x2, o_att, y_ssm, u, d_skip, w_glu, b_glu, n_att, n_ssm, w_out):
    t, d = x2.shape
    d_att = o_att.shape[1]
    d_ssm = u.shape[1]
    tm = _row_tile(t)
    rows = lambda w: pl.BlockSpec((tm, w), lambda i: (i, 0))
    full = lambda a: pl.BlockSpec(a.shape, lambda i: (0,) * a.ndim)
    consts = [d_skip.reshape(1, d_ssm), w_glu.astype(BF16), b_glu.reshape(1, 2 * d_ssm),
              n_att.reshape(1, d_att), n_ssm.reshape(1, d_ssm),
              w_out[:d_att].astype(BF16), w_out[d_att:].astype(BF16)]
    return pl.pallas_call(
        _mix_out_kernel,
        grid=(t // tm,),
        in_specs=[rows(d), rows(d_att), rows(d_ssm)] + [full(c) for c in consts] + [rows(LANES)] * len(y_ssm),
        out_specs=rows(d),
        out_shape=jax.ShapeDtypeStruct((t, d), F32),
        compiler_params=_params("parallel"),
        name="mix_out",
    )(x2, o_att, u, *consts, *y_ssm)


def _ffn_kernel(x_ref, g_ref, wg_ref, wu_ref, wd_ref, fg_ref, o_ref, acc_ref, *, f_chunk, final_norm):
    x = x_ref[...]
    h = _rms(x, g_ref[...]).astype(BF16)
    d_ff = wg_ref.shape[1]
    acc_ref[...] = x
    for c in range(d_ff // f_chunk):
        sl = slice(c * f_chunk, (c + 1) * f_chunk)
        a = jax.nn.silu(_dot(h, wg_ref[:, sl])) * _dot(h, wu_ref[:, sl])
        acc_ref[...] += _dot(a.astype(BF16), wd_ref[sl, :])
    out = acc_ref[...]
    if final_norm:
        out = _rms(out, fg_ref[...])
    o_ref[...] = out


def ffn(x2, g, w_gate, w_up, w_down, final_g, final_norm):
    t, d = x2.shape
    d_ff = w_gate.shape[1]
    tm = _row_tile(t)
    f_chunk = 256 if d_ff % 256 == 0 else d_ff
    full = lambda a: pl.BlockSpec(a.shape, lambda i: (0,) * a.ndim)
    consts = [g.reshape(1, d), w_gate.astype(BF16), w_up.astype(BF16), w_down.astype(BF16),
              final_g.reshape(1, d)]
    return pl.pallas_call(
        functools.partial(_ffn_kernel, f_chunk=f_chunk, final_norm=final_norm),
        grid=(t // tm,),
        in_specs=[pl.BlockSpec((tm, d), lambda i: (i, 0))] + [full(c) for c in consts],
        out_specs=pl.BlockSpec((tm, d), lambda i: (i, 0)),
        out_shape=jax.ShapeDtypeStruct((t, d), F32),
        scratch_shapes=[pltpu.VMEM((tm, d), F32)],
        compiler_params=_params("parallel"),
        name="ffn",
    )(x2, *consts)


def _mixer(x2, bsz, seq, attn_norm_g, w_in, cmp_pe, cmp_w1, cmp_b1, cmp_w2, cmp_b2, log_dt, a_re, a_im,
           b_re, b_im, c_re, c_im, d_skip, w_glu, b_glu, mix_norm_att, mix_norm_ssm, w_out):
    t, d = x2.shape
    d_ssm = d - D_ATT
    bh = bsz * N_KV
    qt, vst, vwt, gt, kc, vc, ks, kw, u = in_proj(x2.reshape(bsz, seq, d), attn_norm_g, w_in)
    k_c = compress(kc, cmp_pe[0], cmp_w1[0], cmp_b1[0], cmp_w2[0], cmp_b2[0], transposed=False)
    v_ct = compress(vc, cmp_pe[1], cmp_w1[1], cmp_b1[1], cmp_w2[1], cmp_b2[1], transposed=True)
    o_att = nsa_attention(qt.reshape(bh, GQA, HEAD_DIM, seq), k_c, v_ct,
                          ks.reshape(bh, seq, HEAD_DIM), vst.reshape(bh, V_ROWS, seq),
                          kw.reshape(bh, seq, HEAD_DIM), vwt.reshape(bh, V_ROWS, seq),
                          gt.reshape(bh, GQA, 3, seq), bsz).reshape(t, D_ATT)

    u = u.reshape(t, d_ssm)
    chunks_per_seq = seq // SSM_CHUNK
    tables = _s5_tables(log_dt, a_re, a_im, b_re, b_im, c_re, c_im, chunks_per_seq)
    y_ssm = s5_unpack(s5_scan(s5_pack(u), *tables, chunks_per_seq))

    return mix_out(x2, o_att, y_ssm, u, d_skip, w_glu, b_glu, mix_norm_att, mix_norm_ssm, w_out)


def kernel(x, attn_norm_g, w_in, cmp_pe, cmp_w1, cmp_b1, cmp_w2, cmp_b2, log_dt, a_re, a_im, b_re, b_im, c_re, c_im, d_skip, w_glu, b_glu, mix_norm_att, mix_norm_ssm, w_out, ffn_norm_g, w_gate, w_up, w_down, final_norm_g):
    bsz, seq, d = x.shape
    depth = w_in.shape[0]
    x2 = x.reshape(bsz * seq, d)
    for l in range(depth):
        x2 = _mixer(x2, bsz, seq, attn_norm_g[l], w_in[l], cmp_pe[l], cmp_w1[l], cmp_b1[l], cmp_w2[l],
                    cmp_b2[l], log_dt[l], a_re[l], a_im[l], b_re[l], b_im[l], c_re[l], c_im[l], d_skip[l],
                    w_glu[l], b_glu[l], mix_norm_att[l], mix_norm_ssm[l], w_out[l])
        x2 = ffn(x2, ffn_norm_g[l], w_gate[l], w_up[l], w_down[l], final_norm_g, final_norm=(l == depth - 1))
    return x2.reshape(bsz, seq, d)
```

```python
import functools
import math

import jax
import jax.numpy as jnp
from jax import lax
from jax.experimental import pallas as pl
from jax.experimental.pallas import tpu as pltpu

HEAD_DIM = 64
N_KV = 3
GQA = 4
N_HEADS = N_KV * GQA
D_ATT = N_HEADS * HEAD_DIM
SSM_CH = 16
SSM_STATE = 64
L_CMP = 32
STRIDE_CMP = 16
CMP_HID = 256
L_SEL = 64
TOP_K = 16
N_FORCED = 3
WINDOW = 512
Q_BLK = 128
TINY = 1e-30
EPS = 1e-6
LOG2E = 1.4426950408889634

SUBLANES = 8
LANES = 128
V_ROWS = HEAD_DIM + 2 * SUBLANES
SSM_CHUNK = 64
PACK_CHUNKS = 64
SEL_SUB = 256
SEL_LOOP = 32
SEL_DIAG = 16
FRONT_PARTS = 4
SEL_AHEAD = 3
VMEM_LIMIT = 56 * 1024 * 1024

F32 = jnp.float32
BF16 = jnp.bfloat16
NEG_INF = float("-inf")


def _dot(a, b):
    return jnp.dot(a, b, preferred_element_type=F32)


def _dot_nt(a, b):
    return lax.dot_general(a, b, (((1,), (1,)), ((), ())), preferred_element_type=F32)


def _dot_split(a, b):
    a_hi = a.astype(BF16)
    a_lo = (a - a_hi.astype(F32)).astype(BF16)
    b_hi = b.astype(BF16)
    b_lo = (b - b_hi.astype(F32)).astype(BF16)
    return _dot(a_hi, b_hi) + _dot(a_hi, b_lo) + _dot(a_lo, b_hi)


def _params(*sem):
    return pltpu.CompilerParams(dimension_semantics=sem, vmem_limit_bytes=VMEM_LIMIT)


def _row_tile(t):
    for tm in (512, 256, 128, 64, 32, 16, 8):
        if t % tm == 0:
            return tm
    raise ValueError(f"token count {t} must be a multiple of 8")


def _rms(x, g):
    ms = jnp.mean(x * x, axis=-1, keepdims=True)
    return x * lax.rsqrt(ms + EPS) * g


def _in_proj_kernel(x_ref, g_ref, wn_ref, wt_ref, qt_ref, vst_ref, vwt_ref, gt_ref, kc_ref, vc_ref,
                    ks_ref, kw_ref, u_ref):
    p_kv = N_KV * HEAD_DIM
    h = _rms(x_ref[0], g_ref[...]).astype(BF16)
    tm = h.shape[0]
    nat = _dot(h, wn_ref[...])
    tr = _dot_nt(wt_ref[...], h)
    qt_ref[0] = (tr[:D_ATT] * (LOG2E * HEAD_DIM ** -0.5)).astype(BF16)
    ones_rows = (lax.broadcasted_iota(jnp.int32, (V_ROWS - HEAD_DIM, tm), 0) == 0).astype(BF16)
    for hd in range(N_KV):
        rows = slice(hd * HEAD_DIM, (hd + 1) * HEAD_DIM)
        for k, ref in enumerate((vst_ref, vwt_ref)):
            off = D_ATT + k * p_kv
            ref[0, hd, :HEAD_DIM, :] = tr[off + hd * HEAD_DIM:off + (hd + 1) * HEAD_DIM].astype(BF16)
            ref[0, hd, HEAD_DIM:, :] = ones_rows
        kc_ref[0, hd] = nat[:, rows]
        vc_ref[0, hd] = nat[:, p_kv + hd * HEAD_DIM:p_kv + (hd + 1) * HEAD_DIM]
        ks_ref[0, hd] = nat[:, 2 * p_kv + hd * HEAD_DIM:2 * p_kv + (hd + 1) * HEAD_DIM].astype(BF16)
        kw_ref[0, hd] = nat[:, 3 * p_kv + hd * HEAD_DIM:3 * p_kv + (hd + 1) * HEAD_DIM].astype(BF16)
    gt_ref[0] = tr[D_ATT + 2 * p_kv:D_ATT + 2 * p_kv + 3 * N_HEADS]
    u_ref[0] = nat[:, 4 * p_kv:]


def in_proj(x3, g, w_in):
    bsz, seq, d = x3.shape
    p_kv = N_KV * HEAD_DIM
    p_gate = 3 * N_HEADS
    d_ssm = d - D_ATT
    o = [0, D_ATT]
    for width in [p_kv] * 6 + [p_gate, d_ssm]:
        o.append(o[-1] + width)
    q, kc, vc, ksl, vsl, kwn, vwn, gl, u = (w_in[:, o[i]:o[i + 1]] for i in range(9))
    w_nat = jnp.concatenate([kc, vc, ksl, kwn, u], axis=1).astype(BF16)
    n_tr = D_ATT + 2 * p_kv + p_gate
    w_tr = jnp.pad(jnp.concatenate([q, vsl, vwn, gl], axis=1), ((0, 0), (0, -n_tr % SUBLANES))).T.astype(BF16)
    tm = _row_tile(seq)
    full = lambda a: pl.BlockSpec(a.shape, lambda b, i: (0,) * a.ndim)
    feat = lambda rows: pl.BlockSpec((1, rows, tm), lambda b, i: (b, 0, i))
    vals = pl.BlockSpec((1, N_KV, V_ROWS, tm), lambda b, i: (b, 0, 0, i))
    keys = pl.BlockSpec((1, N_KV, tm, HEAD_DIM), lambda b, i: (b, 0, i, 0))
    sds = jax.ShapeDtypeStruct
    g2 = g.reshape(1, d)
    return pl.pallas_call(
        _in_proj_kernel,
        grid=(bsz, seq // tm),
        in_specs=[pl.BlockSpec((1, tm, d), lambda b, i: (b, i, 0)), full(g2), full(w_nat), full(w_tr)],
        out_specs=[feat(D_ATT), vals, vals, feat(p_gate), keys, keys, keys, keys,
                   pl.BlockSpec((1, tm, d_ssm), lambda b, i: (b, i, 0))],
        out_shape=[sds((bsz, D_ATT, seq), BF16), sds((bsz, N_KV, V_ROWS, seq), BF16),
                   sds((bsz, N_KV, V_ROWS, seq), BF16), sds((bsz, p_gate, seq), F32),
                   sds((bsz, N_KV, seq, HEAD_DIM), F32), sds((bsz, N_KV, seq, HEAD_DIM), F32),
                   sds((bsz, N_KV, seq, HEAD_DIM), BF16), sds((bsz, N_KV, seq, HEAD_DIM), BF16),
                   sds((bsz, seq, d_ssm), F32)],
        compiler_params=_params("parallel", "parallel"),
        name="in_proj",
    )(x3, g2, w_nat, w_tr)


def _compress_kernel(x_ref, pe_ref, w1_ref, b1_ref, w2_ref, b2_ref, o_ref, *, transposed):
    seq = x_ref.shape[2]
    n = seq // STRIDE_CMP
    first = jnp.zeros((n, CMP_HID), F32)
    second = jnp.zeros((n, CMP_HID), F32)
    for j in range(STRIDE_CMP):
        tok = x_ref[0, 0, pl.ds(j, n, stride=STRIDE_CMP), :]
        w_a = w1_ref[j * HEAD_DIM:(j + 1) * HEAD_DIM, :]
        w_b = w1_ref[(STRIDE_CMP + j) * HEAD_DIM:(STRIDE_CMP + j + 1) * HEAD_DIM, :]
        first = first + _dot((tok + pe_ref[j:j + 1, :]).astype(BF16), w_a)
        second = second + _dot((tok + pe_ref[STRIDE_CMP + j:STRIDE_CMP + j + 1, :]).astype(BF16), w_b)
    hid = first + pltpu.roll(second, n - 1, axis=0) + b1_ref[...]
    hid = jax.nn.gelu(hid).astype(BF16)
    if transposed:
        o_ref[0] = (_dot_nt(w2_ref[...], hid) + b2_ref[...]).astype(o_ref.dtype)
    else:
        o_ref[0] = (_dot(hid, w2_ref[...]) + b2_ref[...]).astype(o_ref.dtype)


def compress(x, pe, w1, b1, w2, b2, transposed):
    bsz, n_kv, seq, _ = x.shape
    bh = bsz * n_kv
    n = seq // STRIDE_CMP
    w1k = w1.astype(BF16)
    if transposed:
        pad = V_ROWS - HEAD_DIM
        w2k = jnp.pad(w2.T, ((0, pad), (0, 0))).astype(BF16)
        b2k = jnp.concatenate([b2, jnp.ones((1,), F32), jnp.zeros((pad - 1,), F32)]).reshape(V_ROWS, 1)
        out_block, out_shape = (1, V_ROWS, n), (bh, V_ROWS, n)
    else:
        w2k = w2.astype(BF16)
        b2k = b2.reshape(1, HEAD_DIM)
        out_block, out_shape = (1, n, HEAD_DIM), (bh, n, HEAD_DIM)
    full = lambda a: pl.BlockSpec(a.shape, lambda i: (0,) * a.ndim)
    b1k = b1.reshape(1, CMP_HID)
    return pl.pallas_call(
        functools.partial(_compress_kernel, transposed=transposed),
        grid=(bh,),
        in_specs=[pl.BlockSpec((1, 1, seq, HEAD_DIM), lambda i: (i // n_kv, i % n_kv, 0, 0)),
                  full(pe), full(w1k), full(b1k), full(w2k), full(b2k)],
        out_specs=pl.BlockSpec(out_block, lambda i: (i, 0, 0)),
        out_shape=jax.ShapeDtypeStruct(out_shape, BF16),
        compiler_params=_params("parallel"),
        name="compress_t" if transposed else "compress_n",
    )(x, pe, w1k, b1k, w2k, b2k)


def _heads(x):
    return [x[:, g * Q_BLK:(g + 1) * Q_BLK] for g in range(GQA)]


def _exp2_safe_max(m):
    return jnp.where(m == NEG_INF, 0.0, m)


def _aligned(x, m):
    return x if isinstance(x, int) else pl.multiple_of(x, m)


def _switch(index, branches, *operands):
    if len(branches) == 1:
        return branches[0](*operands)
    half = len(branches) // 2
    return lax.cond(index < half,
                    lambda *a: _switch(index, branches[:half], *a),
                    lambda *a: _switch(index - half, branches[half:], *a), *operands)


def _normalise(pv):
    return pv[:HEAD_DIM] / jnp.maximum(pv[HEAD_DIM:HEAD_DIM + 1], TINY)


def _attn_kernel(qt_ref, kc_ref, vct_ref, ks_ref, vst_ref, kw_ref, vwt_ref, gt_ref, ovt_ref, cthr_ref,
                 o_ref, selb_ref, s_ref, *, seq):
    qi = pl.program_id(1)
    n_sel = seq // L_SEL
    lanes = GQA * Q_BLK
    win_keys = WINDOW + Q_BLK

    qt = jnp.concatenate([qt_ref[0, g] for g in range(GQA)], axis=1)
    q0 = qi * Q_BLK
    lane_q = lax.broadcasted_iota(jnp.int32, (1, Q_BLK), 1)
    t_q = q0 + lane_q
    t_all = jnp.concatenate([t_q] * GQA, axis=1)

    def sel_scores(k0):
        return _dot(ks_ref[0, pl.ds(k0, SEL_SUB), :], qt)

    def front(n_rows, full_band, n_static):
        n_blk = n_rows * STRIDE_CMP // L_SEL
        s_c = _dot(kc_ref[0, :n_rows, :], qt)
        for a in range(SEL_AHEAD):
            s_ref[a] = sel_scores(a * SEL_SUB)
        vis = cthr_ref[:n_rows, :] <= q0
        p_heads = []
        for s_g in _heads(s_c):
            s_g = jnp.where(vis, s_g, NEG_INF)
            m_g = _exp2_safe_max(jnp.max(s_g, axis=0, keepdims=True))
            p_heads.append(jnp.exp2(s_g - m_g).astype(BF16))
        p_c = jnp.concatenate(p_heads, axis=1)
        pv_c = _dot(vct_ref[0, :, :n_rows], p_c)
        r_c = 1.0 / jnp.maximum(pv_c[HEAD_DIM:HEAD_DIM + 1], TINY)
        o_c = pv_c[:HEAD_DIM] * r_c
        imp_all = _dot(ovt_ref[:n_blk, :n_rows], p_c) * r_c
        imp = functools.reduce(jnp.add, _heads(imp_all))

        blk = lax.broadcasted_iota(jnp.int32, (n_blk, Q_BLK), 0).astype(F32)
        cur = (t_q // L_SEL).astype(F32)
        forced = (blk == 0.0) | (blk == cur) | (blk == cur - 1.0)
        free = (blk <= cur) & jnp.logical_not(forced)
        cand = jnp.where(free, imp, NEG_INF)
        n_pick = min(TOP_K, n_sel) - N_FORCED

        def rank_exact(val):
            for _ in range(n_pick):
                mx = jnp.max(val, axis=0, keepdims=True)
                first = jnp.min(jnp.where(val == mx, blk, float(n_blk)), axis=0, keepdims=True)
                val = jnp.where(blk == first, NEG_INF, val)
            return val

        val_fast = cand
        for _ in range(n_pick):
            val_fast = jnp.where(val_fast == jnp.max(val_fast, axis=0, keepdims=True), NEG_INF, val_fast)
        n_free = jnp.sum(free.astype(F32), axis=0, keepdims=True)
        n_got = jnp.sum((free & (val_fast == NEG_INF)).astype(F32), axis=0, keepdims=True)
        tied = jnp.max(jnp.where(n_got != jnp.minimum(n_free, float(n_pick)), 1.0, 0.0)) > 0.0

        start = pl.multiple_of(jnp.maximum(q0 - WINDOW, 0), Q_BLK)
        s_w = _dot(kw_ref[0, pl.ds(start, win_keys), :], qt)
        r_minus_c = (lax.broadcasted_iota(jnp.int32, (Q_BLK, Q_BLK), 0)
                     - lax.broadcasted_iota(jnp.int32, (Q_BLK, Q_BLK), 1))
        chunks = []
        n_chunks = win_keys // Q_BLK
        for i in range(n_chunks):
            rows = s_w[i * Q_BLK:(i + 1) * Q_BLK]
            if not full_band or i in (0, n_chunks - 1):
                hi = q0 - start - i * Q_BLK
                mask = (r_minus_c <= hi) & (r_minus_c > hi - WINDOW)
                rows = jnp.concatenate([jnp.where(mask, x, NEG_INF) for x in _heads(rows)], axis=1)
            chunks.append(rows)
        m_w = _exp2_safe_max(functools.reduce(jnp.maximum, [jnp.max(x, axis=0, keepdims=True) for x in chunks]))
        p_w = jnp.concatenate([jnp.exp2(x - m_w).astype(BF16) for x in chunks], axis=0)
        o_w = _normalise(_dot(vwt_ref[0, :, pl.ds(start, win_keys)], p_w))
        val = lax.cond(tied, lambda: rank_exact(cand), lambda: val_fast)
        selb_ref[:n_blk, :] = jnp.where(forced | (free & (val == NEG_INF)), 0.0, NEG_INF)
        carry = (jnp.full((1, lanes), NEG_INF, F32), jnp.zeros((V_ROWS, lanes), F32))
        if n_static:
            carry = sel_run(0, *carry, n_static, False)
        return (o_c, o_w) + tuple(carry)

    row_iota = lax.broadcasted_iota(jnp.int32, (L_SEL, lanes), 0)

    def sel_update(s_t, k0, m_prev, acc_prev, causal):
        blk0 = k0 // L_SEL
        blocks, biases, part_max = [], [], None
        for j in range(SEL_SUB // L_SEL):
            bias = jnp.concatenate([selb_ref[pl.ds(blk0 + j, 1), :]] * GQA, axis=1)
            s_b = s_t[j * L_SEL:(j + 1) * L_SEL]
            if causal:
                s_b = jnp.where(k0 + j * L_SEL + row_iota <= t_all, s_b, NEG_INF)
            s8 = functools.reduce(jnp.maximum, [s_b[r:r + SUBLANES] for r in range(0, L_SEL, SUBLANES)])
            cand = s8 + bias
            part_max = cand if part_max is None else jnp.maximum(part_max, cand)
            blocks.append(s_b)
            biases.append(bias)
        m_new = jnp.maximum(m_prev, jnp.max(part_max, axis=0, keepdims=True))
        m_safe = _exp2_safe_max(m_new)
        alpha = jnp.exp2(m_prev - m_safe)
        p = jnp.concatenate([jnp.exp2(s_b - (m_safe - bias)).astype(BF16)
                             for s_b, bias in zip(blocks, biases)], axis=0)
        pv = _dot(vst_ref[0, :, pl.ds(k0, SEL_SUB)], p)
        return m_new, alpha * acc_prev + pv

    def sel_run(k0, m_prev, acc_prev, n_sub, diagonal):
        queue = [s_ref[a] for a in range(min(SEL_AHEAD, n_sub) if diagonal else SEL_AHEAD)]
        for i in range(n_sub):
            ahead = i + SEL_AHEAD
            if not diagonal or ahead < n_sub:
                queue.append(sel_scores(_aligned(k0 + ahead * SEL_SUB, SEL_SUB)))
            m_prev, acc_prev = sel_update(queue.pop(0), _aligned(k0 + i * SEL_SUB, SEL_SUB),
                                          m_prev, acc_prev, causal=diagonal and i == n_sub - 1)
        if not diagonal:
            for a in range(SEL_AHEAD):
                s_ref[a] = queue[a]
        return m_prev, acc_prev

    n_cmp = seq // STRIDE_CMP
    parts = FRONT_PARTS if n_cmp % (FRONT_PARTS * Q_BLK) == 0 else 1
    part_subs = seq // parts // SEL_SUB
    part = q0 // (seq // parts)
    o_c, o_w, *carry = _switch(
        part, [functools.partial(front, n_cmp * (k + 1) // parts, k * (seq // parts) >= WINDOW, k * part_subs)
               for k in range(parts)])

    n_below = q0 // SEL_SUB
    done = part * part_subs
    if part_subs > SEL_LOOP:
        trips = (n_below - done) // SEL_LOOP
        carry = lax.fori_loop(
            0, trips, lambda kt, c: sel_run(_aligned((done + kt * SEL_LOOP) * SEL_SUB, SEL_SUB), *c, SEL_LOOP, False),
            tuple(carry))
        done = done + trips * SEL_LOOP
    run = min(SEL_LOOP, part_subs) // 2
    while run >= SEL_DIAG:
        take = ((n_below - done) // run) == 1
        k_run = _aligned(done * SEL_SUB, SEL_SUB)
        carry = lax.cond(take, functools.partial(lambda k, n, m, a: sel_run(k, m, a, n, False), k_run, run),
                         lambda m, a: (m, a), *carry)
        done = done + jnp.where(take, run, 0)
        run //= 2
    k_last = _aligned(done * SEL_SUB, SEL_SUB)

    def finish(n_diag, m_s, acc_s):
        o_s = _normalise(sel_run(k_last, m_s, acc_s, n_diag, True)[1])
        outs = []
        for g in range(GQA):
            sl = slice(g * Q_BLK, (g + 1) * Q_BLK)
            gate = jax.nn.sigmoid(gt_ref[0, g])
            outs.append(gate[0:1] * o_c[:, sl] + gate[1:2] * o_s[:, sl] + gate[2:3] * o_w[:, sl])
        pairs = [jnp.concatenate(outs[g:g + 2], axis=0).T for g in range(0, GQA, 2)]
        o_ref[0] = jnp.concatenate(pairs, axis=1)

    _switch(n_below - done, [functools.partial(finish, n + 1) for n in range(SEL_DIAG)], *carry)


def nsa_attention(qt, kc, vct, ks, vst, kw, vwt, gt, bsz):
    bh, _, _, seq = qt.shape
    assert SEL_LOOP % SEL_DIAG == 0 and seq % (SEL_DIAG * SEL_SUB) == 0 and seq >= WINDOW + Q_BLK
    n_cmp = seq // STRIDE_CMP
    n_sel = seq // L_SEL
    cmp_start = jnp.arange(n_cmp) * STRIDE_CMP
    sel_start = jnp.arange(n_sel) * L_SEL
    ovt = ((cmp_start[None, :] < sel_start[:, None] + L_SEL)
           & (cmp_start[None, :] + L_CMP > sel_start[:, None])).astype(BF16)
    cthr = (cmp_start + L_CMP - 1)[:, None] - jnp.arange(Q_BLK)[None, :]
    per_bh = lambda shape: pl.BlockSpec((1,) + shape, lambda b, i: (b,) + (0,) * len(shape))
    return pl.pallas_call(
        functools.partial(_attn_kernel, seq=seq),
        grid=(bh, seq // Q_BLK),
        in_specs=[pl.BlockSpec((1, GQA, HEAD_DIM, Q_BLK), lambda b, i: (b, 0, 0, i)),
                  per_bh((n_cmp, HEAD_DIM)), per_bh((V_ROWS, n_cmp)),
                  per_bh((seq, HEAD_DIM)), per_bh((V_ROWS, seq)),
                  per_bh((seq, HEAD_DIM)), per_bh((V_ROWS, seq)),
                  pl.BlockSpec((1, GQA, 3, Q_BLK), lambda b, i: (b, 0, 0, i)),
                  pl.BlockSpec((n_sel, n_cmp), lambda b, i: (0, 0)),
                  pl.BlockSpec((n_cmp, Q_BLK), lambda b, i: (0, 0))],
        out_specs=pl.BlockSpec((1, Q_BLK, GQA * HEAD_DIM), lambda b, i: (b // N_KV, i, b % N_KV)),
        out_shape=jax.ShapeDtypeStruct((bsz, seq, N_KV * GQA * HEAD_DIM), F32),
        scratch_shapes=[pltpu.VMEM((n_sel, Q_BLK), F32), pltpu.VMEM((SEL_AHEAD, SEL_SUB, GQA * Q_BLK), F32)],
        compiler_params=_params("parallel", "arbitrary"),
        name="nsa_attention",
    )(qt, kc, vct, ks, vst, kw, vwt, gt, ovt, cthr.astype(jnp.int32))


def _ssm_kernel(u_ref, pt_ref, bq_ref, w_ref, v_ref, a1_ref, a2_ref, o_ref, mt_ref, *, chunks_per_seq):
    width = SSM_CHUNK * SSM_CH
    u = u_ref[0]
    kt = _dot_split(bq_ref[0], pt_ref[0])
    col = lax.broadcasted_iota(jnp.int32, (SSM_CH, width), 1)
    for s in range(SSM_CHUNK):
        shifted = kt if s == 0 else pltpu.roll(kt, SSM_CH * s, axis=1)
        mt_ref[s * SSM_CH:(s + 1) * SSM_CH, :] = jnp.where(col >= SSM_CH * s, shifted, 0.0).astype(BF16)
    y = _dot(u, mt_ref[...])
    x = _dot(u, w_ref[0])
    row = lax.broadcasted_iota(jnp.int32, x.shape, 0) % chunks_per_seq
    steps = int(math.log2(chunks_per_seq))
    for j in range(steps):
        d = 1 << j
        sh = jnp.where(row >= d, pltpu.roll(x, d, axis=0), 0.0)
        x = x + a1_ref[0, j:j + 1, :] * sh + a2_ref[0, j:j + 1, :] * pltpu.roll(sh, SSM_STATE, axis=1)
    prev = jnp.where(row >= 1, pltpu.roll(x, 1, axis=0), 0.0)
    o_ref[0] = y + _dot(prev.astype(BF16), v_ref[0])


def s5_scan(u2, pt, bq, w, v, a1, a2, chunks_per_seq):
    g, n_chunks, width = u2.shape
    per_g = lambda a: pl.BlockSpec((1,) + a.shape[1:], lambda i: (i,) + (0,) * (a.ndim - 1))
    return pl.pallas_call(
        functools.partial(_ssm_kernel, chunks_per_seq=chunks_per_seq),
        grid=(g,),
        in_specs=[per_g(u2), per_g(pt), per_g(bq), per_g(w), per_g(v), per_g(a1), per_g(a2)],
        out_specs=pl.BlockSpec((1, n_chunks, width), lambda i: (i, 0, 0)),
        out_shape=jax.ShapeDtypeStruct((g, n_chunks, width), F32),
        scratch_shapes=[pltpu.VMEM((width, width), BF16)],
        compiler_params=_params("parallel"),
        name="s5_scan",
    )(u2, pt, bq, w, v, a1, a2)


def _s5_tables(log_dt, a_re, a_im, b_re, b_im, c_re, c_im, chunks_per_seq):
    g, n = a_re.shape
    dt = jnp.exp(log_dt)[:, None]
    lam_re, lam_im = dt * a_re, dt * a_im

    def power(k):
        k = k.astype(F32)[None, :, None]
        mag = jnp.exp(k * lam_re[:, None, :])
        return mag * jnp.cos(k * lam_im[:, None, :]), mag * jnp.sin(k * lam_im[:, None, :])

    abar_re, abar_im = (x[:, 0] for x in power(jnp.ones((1,))))
    den = a_re * a_re + a_im * a_im
    nr = abar_re - 1.0
    f_re = (nr * a_re + abar_im * a_im) / den
    f_im = (abar_im * a_re - nr * a_im) / den
    bb_re = f_re[..., None] * b_re - f_im[..., None] * b_im
    bb_im = f_re[..., None] * b_im + f_im[..., None] * b_re
    steps = jnp.arange(SSM_CHUNK)
    width = SSM_CHUNK * SSM_CH
    e_re, e_im = power(steps)
    ca_re = c_re[:, None] * e_re[:, :, None, :] - c_im[:, None] * e_im[:, :, None, :]
    ca_im = c_re[:, None] * e_im[:, :, None, :] + c_im[:, None] * e_re[:, :, None, :]
    pt = jnp.concatenate([ca_re, -ca_im], axis=-1).reshape(g, width, 2 * n).transpose(0, 2, 1)
    bq = jnp.concatenate([bb_re, bb_im], axis=1).transpose(0, 2, 1)
    r_re, r_im = power(SSM_CHUNK - 1 - steps)
    bt_re, bt_im = bb_re.transpose(0, 2, 1)[:, None], bb_im.transpose(0, 2, 1)[:, None]
    w_re = r_re[:, :, None, :] * bt_re - r_im[:, :, None, :] * bt_im
    w_im = r_re[:, :, None, :] * bt_im + r_im[:, :, None, :] * bt_re
    w = jnp.concatenate([w_re, w_im], axis=-1).reshape(g, width, 2 * n).astype(BF16)
    n_re, n_im = power(steps + 1)
    cv_re = c_re[:, None] * n_re[:, :, None, :] - c_im[:, None] * n_im[:, :, None, :]
    cv_im = c_re[:, None] * n_im[:, :, None, :] + c_im[:, None] * n_re[:, :, None, :]
    v = jnp.concatenate([cv_re, -cv_im], axis=-1).reshape(g, width, 2 * n).transpose(0, 2, 1).astype(BF16)
    n_steps = int(math.log2(chunks_per_seq))
    s_re, s_im = power(SSM_CHUNK * (2 ** jnp.arange(max(n_steps, 1))))
    a1 = jnp.concatenate([s_re, s_re], axis=-1)
    a2 = jnp.concatenate([-s_im, s_im], axis=-1)
    return pt, bq, w, v, a1, a2


def _slot_masks(rows):
    slot = lax.broadcasted_iota(jnp.int32, (rows, LANES), 1) // SSM_CH
    return [slot == k for k in range(LANES // SSM_CH)]


def _s5_pack_kernel(*refs):
    u_refs, o_ref = refs[:-1], refs[-1]
    cb = o_ref.shape[1]
    slots = LANES // SSM_CH
    masks = _slot_masks(cb)
    for b, u_ref in enumerate(u_refs):
        for m in range(SSM_CHUNK // slots):
            rot = [pltpu.roll(u_ref[pl.ds(m * slots + j, cb, stride=SSM_CHUNK), :], SSM_CH * j, axis=1)
                   for j in range(slots)]
            for g in range(slots):
                mix = rot[0]
                for j in range(1, slots):
                    mix = jnp.where(masks[(g + j) % slots], rot[j], mix)
                out = mix if g == 0 else pltpu.roll(mix, LANES - SSM_CH * g, axis=1)
                o_ref[b * slots + g, :, m * LANES:(m + 1) * LANES] = out.astype(o_ref.dtype)


def s5_pack(u):
    t, d = u.shape
    n_groups = d // SSM_CH
    n_chunks = t // SSM_CHUNK
    cb = min(PACK_CHUNKS, n_chunks)
    width = SSM_CHUNK * SSM_CH
    return pl.pallas_call(
        _s5_pack_kernel,
        grid=(n_chunks // cb,),
        in_specs=[pl.BlockSpec((cb * SSM_CHUNK, LANES), functools.partial(lambda b, i: (i, b), b))
                  for b in range(d // LANES)],
        out_specs=pl.BlockSpec((n_groups, cb, width), lambda i: (0, i, 0)),
        out_shape=jax.ShapeDtypeStruct((n_groups, n_chunks, width), BF16),
        compiler_params=_params("parallel"),
        name="s5_pack",
    )(*[u] * (d // LANES))


def _s5_unpack_kernel(y_ref, *o_refs):
    cb = y_ref.shape[1]
    slots = LANES // SSM_CH
    masks = _slot_masks(cb)
    for b, o_ref in enumerate(o_refs):
        for m in range(SSM_CHUNK // slots):
            rot = [pltpu.roll(y_ref[b * slots + g, :, m * LANES:(m + 1) * LANES], SSM_CH * g, axis=1)
                   if g else y_ref[b * slots, :, m * LANES:(m + 1) * LANES] for g in range(slots)]
            for j in range(slots):
                mix = rot[0]
                for g in range(1, slots):
                    mix = jnp.where(masks[(g + j) % slots], rot[g], mix)
                out = mix if j == 0 else pltpu.roll(mix, LANES - SSM_CH * j, axis=1)
                o_ref[pl.ds(m * slots + j, cb, stride=SSM_CHUNK), :] = out


def s5_unpack(y2):
    n_groups, n_chunks, width = y2.shape
    d = n_groups * SSM_CH
    cb = min(PACK_CHUNKS, n_chunks)
    return pl.pallas_call(
        _s5_unpack_kernel,
        grid=(n_chunks // cb,),
        in_specs=[pl.BlockSpec((n_groups, cb, width), lambda i: (0, i, 0))],
        out_specs=[pl.BlockSpec((cb * SSM_CHUNK, LANES), lambda i: (i, 0))] * (d // LANES),
        out_shape=[jax.ShapeDtypeStruct((n_chunks * SSM_CHUNK, LANES), F32)] * (d // LANES),
        compiler_params=_params("parallel"),
        name="s5_unpack",
    )(y2)


def _mix_out_kernel(x_ref, oa_ref, u_ref, dsk_ref, wglu_ref, bglu_ref, na_ref, ns_ref, woa_ref, wos_ref,
                    *refs):
    ys_refs, o_ref = refs[:-1], refs[-1]
    d_ssm = u_ref.shape[1]
    y_scan = jnp.concatenate([r[...] for r in ys_refs], axis=1)
    y = jax.nn.gelu(y_scan + dsk_ref[...] * u_ref[...])
    z = _dot(y.astype(BF16), wglu_ref[...]) + bglu_ref[...]
    o_ssm = z[:, :d_ssm] * jax.nn.sigmoid(z[:, d_ssm:])
    att_n = _rms(oa_ref[...], na_ref[...]).astype(BF16)
    ssm_n = _rms(o_ssm, ns_ref[...]).astype(BF16)
    o_ref[...] = x_ref[...] + _dot(att_n, woa_ref[...]) + _dot(ssm_n, wos_ref[...])


def mix_out(x2, o_att, y_ssm, u, d_skip, w_glu, b_glu, n_att, n_ssm, w_out):
    t, d = x2.shape
    d_att = o_att.shape[1]
    d_ssm = u.shape[1]
    tm = _row_tile(t)
    rows = lambda w: pl.BlockSpec((tm, w), lambda i: (i, 0))
    full = lambda a: pl.BlockSpec(a.shape, lambda i: (0,) * a.ndim)
    consts = [d_skip.reshape(1, d_ssm), w_glu.astype(BF16), b_glu.reshape(1, 2 * d_ssm),
              n_att.reshape(1, d_att), n_ssm.reshape(1, d_ssm),
              w_out[:d_att].astype(BF16), w_out[d_att:].astype(BF16)]
    return pl.pallas_call(
        _mix_out_kernel,
        grid=(t // tm,),
        in_specs=[rows(d), rows(d_att), rows(d_ssm)] + [full(c) for c in consts] + [rows(LANES)] * len(y_ssm),
        out_specs=rows(d),
        out_shape=jax.ShapeDtypeStruct((t, d), F32),
        compiler_params=_params("parallel"),
        name="mix_out",
    )(x2, o_att, u, *consts, *y_ssm)


def _ffn_kernel(x_ref, g_ref, wg_ref, wu_ref, wd_ref, fg_ref, o_ref, acc_ref, *, f_chunk, final_norm):
    x = x_ref[...]
    h = _rms(x, g_ref[...]).astype(BF16)
    d_ff = wg_ref.shape[1]
    acc_ref[...] = x
    for c in range(d_ff // f_chunk):
        sl = slice(c * f_chunk, (c + 1) * f_chunk)
        a = jax.nn.silu(_dot(h, wg_ref[:, sl])) * _dot(h, wu_ref[:, sl])
        acc_ref[...] += _dot(a.astype(BF16), wd_ref[sl, :])
    out = acc_ref[...]
    if final_norm:
        out = _rms(out, fg_ref[...])
    o_ref[...] = out


def ffn(x2, g, w_gate, w_up, w_down, final_g, final_norm):
    t, d = x2.shape
    d_ff = w_gate.shape[1]
    tm = _row_tile(t)
    f_chunk = 256 if d_ff % 256 == 0 else d_ff
    full = lambda a: pl.BlockSpec(a.shape, lambda i: (0,) * a.ndim)
    consts = [g.reshape(1, d), w_gate.astype(BF16), w_up.astype(BF16), w_down.astype(BF16),
              final_g.reshape(1, d)]
    return pl.pallas_call(
        functools.partial(_ffn_kernel, f_chunk=f_chunk, final_norm=final_norm),
        grid=(t // tm,),
        in_specs=[pl.BlockSpec((tm, d), lambda i: (i, 0))] + [full(c) for c in consts],
        out_specs=pl.BlockSpec((tm, d), lambda i: (i, 0)),
        out_shape=jax.ShapeDtypeStruct((t, d), F32),
        scratch_shapes=[pltpu.VMEM((tm, d), F32)],
        compiler_params=_params("parallel"),
        name="ffn",
    )(x2, *consts)


def _mixer(x2, bsz, seq, attn_norm_g, w_in, cmp_pe, cmp_w1, cmp_b1, cmp_w2, cmp_b2, log_dt, a_re, a_im,
           b_re, b_im, c_re, c_im, d_skip, w_glu, b_glu, mix_norm_att, mix_norm_ssm, w_out):
    t, d = x2.shape
    d_ssm = d - D_ATT
    bh = bsz * N_KV
    qt, vst, vwt, gt, kc, vc, ks, kw, u = in_proj(x2.reshape(bsz, seq, d), attn_norm_g, w_in)
    k_c = compress(kc, cmp_pe[0], cmp_w1[0], cmp_b1[0], cmp_w2[0], cmp_b2[0], transposed=False)
    v_ct = compress(vc, cmp_pe[1], cmp_w1[1], cmp_b1[1], cmp_w2[1], cmp_b2[1], transposed=True)
    o_att = nsa_attention(qt.reshape(bh, GQA, HEAD_DIM, seq), k_c, v_ct,
                          ks.reshape(bh, seq, HEAD_DIM), vst.reshape(bh, V_ROWS, seq),
                          kw.reshape(bh, seq, HEAD_DIM), vwt.reshape(bh, V_ROWS, seq),
                          gt.reshape(bh, GQA, 3, seq), bsz).reshape(t, D_ATT)

    u = u.reshape(t, d_ssm)
    chunks_per_seq = seq // SSM_CHUNK
    tables = _s5_tables(log_dt, a_re, a_im, b_re, b_im, c_re, c_im, chunks_per_seq)
    y_ssm = s5_unpack(s5_scan(s5_pack(u), *tables, chunks_per_seq))

    return mix_out(x2, o_att, y_ssm, u, d_skip, w_glu, b_glu, mix_norm_att, mix_norm_ssm, w_out)


def kernel(x, attn_norm_g, w_in, cmp_pe, cmp_w1, cmp_b1, cmp_w2, cmp_b2, log_dt, a_re, a_im, b_re, b_im, c_re, c_im, d_skip, w_glu, b_glu, mix_norm_att, mix_norm_ssm, w_out, ffn_norm_g, w_gate, w_up, w_down, final_norm_g):
    bsz, seq, d = x.shape
    depth = w_in.shape[0]
    x2 = x.reshape(bsz * seq, d)
    for l in range(depth):
        x2 = _mixer(x2, bsz, seq, attn_norm_g[l], w_in[l], cmp_pe[l], cmp_w1[l], cmp_b1[l], cmp_w2[l],
                    cmp_b2[l], log_dt[l], a_re[l], a_im[l], b_re[l], b_im[l], c_re[l], c_im[l], d_skip[l],
                    w_glu[l], b_glu[l], mix_norm_att[l], mix_norm_ssm[l], w_out[l])
        x2 = ffn(x2, ffn_norm_g[l], w_gate[l], w_up[l], w_down[l], final_norm_g, final_norm=(l == depth - 1))
    return x2.reshape(bsz, seq, d)
```

```python
import functools
import math

import jax
import jax.numpy as jnp
from jax import lax
from jax.experimental import pallas as pl
from jax.experimental.pallas import tpu as pltpu

HEAD_DIM = 64
N_KV = 3
GQA = 4
N_HEADS = N_KV * GQA
D_ATT = N_HEADS * HEAD_DIM
SSM_CH = 16
SSM_STATE = 64
L_CMP = 32
STRIDE_CMP = 16
CMP_HID = 256
L_SEL = 64
TOP_K = 16
N_FORCED = 3
WINDOW = 512
Q_BLK = 128
TINY = 1e-30
EPS = 1e-6
LOG2E = 1.4426950408889634

SUBLANES = 8
LANES = 128
V_ROWS = HEAD_DIM + 2 * SUBLANES
SSM_CHUNK = 64
PACK_CHUNKS = 64
MIX_ROWS = 1024
SEL_SUB = 256
SEL_LOOP = 32
SEL_DIAG = 16
FRONT_PARTS = 4
SEL_AHEAD = 3
VMEM_LIMIT = 56 * 1024 * 1024

F32 = jnp.float32
BF16 = jnp.bfloat16
NEG_INF = float("-inf")


def _dot(a, b):
    return jnp.dot(a, b, preferred_element_type=F32)


def _dot_nt(a, b):
    return lax.dot_general(a, b, (((1,), (1,)), ((), ())), preferred_element_type=F32)


def _dot_split(a, b):
    a_hi = a.astype(BF16)
    a_lo = (a - a_hi.astype(F32)).astype(BF16)
    b_hi = b.astype(BF16)
    b_lo = (b - b_hi.astype(F32)).astype(BF16)
    return _dot(a_hi, b_hi) + _dot(a_hi, b_lo) + _dot(a_lo, b_hi)


def _params(*sem):
    return pltpu.CompilerParams(dimension_semantics=sem, vmem_limit_bytes=VMEM_LIMIT)


def _row_tile(t, largest=512):
    tm = largest
    while tm >= SUBLANES:
        if t % tm == 0:
            return tm
        tm //= 2
    raise ValueError(f"token count {t} must be a multiple of {SUBLANES}")


def _rms(x, g):
    ms = jnp.mean(x * x, axis=-1, keepdims=True)
    return x * lax.rsqrt(ms + EPS) * g


def _in_proj_kernel(x_ref, g_ref, wn_ref, wt_ref, qt_ref, vst_ref, vwt_ref, gt_ref, kc_ref, vc_ref,
                    ks_ref, kw_ref, u_ref):
    p_kv = N_KV * HEAD_DIM
    h = _rms(x_ref[0], g_ref[...]).astype(BF16)
    tm = h.shape[0]
    nat = _dot(h, wn_ref[...])
    tr = _dot_nt(wt_ref[...], h)
    qt_ref[0] = (tr[:D_ATT] * (LOG2E * HEAD_DIM ** -0.5)).astype(BF16)
    ones_rows = (lax.broadcasted_iota(jnp.int32, (V_ROWS - HEAD_DIM, tm), 0) == 0).astype(BF16)
    for hd in range(N_KV):
        rows = slice(hd * HEAD_DIM, (hd + 1) * HEAD_DIM)
        for k, ref in enumerate((vst_ref, vwt_ref)):
            off = D_ATT + k * p_kv
            ref[0, hd, :HEAD_DIM, :] = tr[off + hd * HEAD_DIM:off + (hd + 1) * HEAD_DIM].astype(BF16)
            ref[0, hd, HEAD_DIM:, :] = ones_rows
        kc_ref[0, hd] = nat[:, rows]
        vc_ref[0, hd] = nat[:, p_kv + hd * HEAD_DIM:p_kv + (hd + 1) * HEAD_DIM]
        ks_ref[0, hd] = nat[:, 2 * p_kv + hd * HEAD_DIM:2 * p_kv + (hd + 1) * HEAD_DIM].astype(BF16)
        kw_ref[0, hd] = nat[:, 3 * p_kv + hd * HEAD_DIM:3 * p_kv + (hd + 1) * HEAD_DIM].astype(BF16)
    gt_ref[0] = tr[D_ATT + 2 * p_kv:D_ATT + 2 * p_kv + 3 * N_HEADS]
    u_ref[0] = nat[:, 4 * p_kv:]


def in_proj(x3, g, w_in):
    bsz, seq, d = x3.shape
    p_kv = N_KV * HEAD_DIM
    p_gate = 3 * N_HEADS
    d_ssm = d - D_ATT
    o = [0, D_ATT]
    for width in [p_kv] * 6 + [p_gate, d_ssm]:
        o.append(o[-1] + width)
    q, kc, vc, ksl, vsl, kwn, vwn, gl, u = (w_in[:, o[i]:o[i + 1]] for i in range(9))
    w_nat = jnp.concatenate([kc, vc, ksl, kwn, u], axis=1).astype(BF16)
    n_tr = D_ATT + 2 * p_kv + p_gate
    w_tr = jnp.pad(jnp.concatenate([q, vsl, vwn, gl], axis=1), ((0, 0), (0, -n_tr % SUBLANES))).T.astype(BF16)
    tm = _row_tile(seq)
    full = lambda a: pl.BlockSpec(a.shape, lambda b, i: (0,) * a.ndim)
    feat = lambda rows: pl.BlockSpec((1, rows, tm), lambda b, i: (b, 0, i))
    vals = pl.BlockSpec((1, N_KV, V_ROWS, tm), lambda b, i: (b, 0, 0, i))
    keys = pl.BlockSpec((1, N_KV, tm, HEAD_DIM), lambda b, i: (b, 0, i, 0))
    sds = jax.ShapeDtypeStruct
    g2 = g.reshape(1, d)
    return pl.pallas_call(
        _in_proj_kernel,
        grid=(bsz, seq // tm),
        in_specs=[pl.BlockSpec((1, tm, d), lambda b, i: (b, i, 0)), full(g2), full(w_nat), full(w_tr)],
        out_specs=[feat(D_ATT), vals, vals, feat(p_gate), keys, keys, keys, keys,
                   pl.BlockSpec((1, tm, d_ssm), lambda b, i: (b, i, 0))],
        out_shape=[sds((bsz, D_ATT, seq), BF16), sds((bsz, N_KV, V_ROWS, seq), BF16),
                   sds((bsz, N_KV, V_ROWS, seq), BF16), sds((bsz, p_gate, seq), F32),
                   sds((bsz, N_KV, seq, HEAD_DIM), F32), sds((bsz, N_KV, seq, HEAD_DIM), F32),
                   sds((bsz, N_KV, seq, HEAD_DIM), BF16), sds((bsz, N_KV, seq, HEAD_DIM), BF16),
                   sds((bsz, seq, d_ssm), F32)],
        compiler_params=_params("parallel", "parallel"),
        name="in_proj",
    )(x3, g2, w_nat, w_tr)


def _compress_kernel(x_ref, pe_ref, w1_ref, b1_ref, w2_ref, b2_ref, o_ref, *, transposed):
    seq = x_ref.shape[2]
    n = seq // STRIDE_CMP
    first = jnp.zeros((n, CMP_HID), F32)
    second = jnp.zeros((n, CMP_HID), F32)
    for j in range(STRIDE_CMP):
        tok = x_ref[0, 0, pl.ds(j, n, stride=STRIDE_CMP), :]
        w_a = w1_ref[j * HEAD_DIM:(j + 1) * HEAD_DIM, :]
        w_b = w1_ref[(STRIDE_CMP + j) * HEAD_DIM:(STRIDE_CMP + j + 1) * HEAD_DIM, :]
        first = first + _dot((tok + pe_ref[j:j + 1, :]).astype(BF16), w_a)
        second = second + _dot((tok + pe_ref[STRIDE_CMP + j:STRIDE_CMP + j + 1, :]).astype(BF16), w_b)
    hid = first + pltpu.roll(second, n - 1, axis=0) + b1_ref[...]
    hid = jax.nn.gelu(hid).astype(BF16)
    if transposed:
        o_ref[0] = (_dot_nt(w2_ref[...], hid) + b2_ref[...]).astype(o_ref.dtype)
    else:
        o_ref[0] = (_dot(hid, w2_ref[...]) + b2_ref[...]).astype(o_ref.dtype)


def compress(x, pe, w1, b1, w2, b2, transposed):
    bsz, n_kv, seq, _ = x.shape
    bh = bsz * n_kv
    n = seq // STRIDE_CMP
    w1k = w1.astype(BF16)
    if transposed:
        pad = V_ROWS - HEAD_DIM
        w2k = jnp.pad(w2.T, ((0, pad), (0, 0))).astype(BF16)
        b2k = jnp.concatenate([b2, jnp.ones((1,), F32), jnp.zeros((pad - 1,), F32)]).reshape(V_ROWS, 1)
        out_block, out_shape = (1, V_ROWS, n), (bh, V_ROWS, n)
    else:
        w2k = w2.astype(BF16)
        b2k = b2.reshape(1, HEAD_DIM)
        out_block, out_shape = (1, n, HEAD_DIM), (bh, n, HEAD_DIM)
    full = lambda a: pl.BlockSpec(a.shape, lambda i: (0,) * a.ndim)
    b1k = b1.reshape(1, CMP_HID)
    return pl.pallas_call(
        functools.partial(_compress_kernel, transposed=transposed),
        grid=(bh,),
        in_specs=[pl.BlockSpec((1, 1, seq, HEAD_DIM), lambda i: (i // n_kv, i % n_kv, 0, 0)),
                  full(pe), full(w1k), full(b1k), full(w2k), full(b2k)],
        out_specs=pl.BlockSpec(out_block, lambda i: (i, 0, 0)),
        out_shape=jax.ShapeDtypeStruct(out_shape, BF16),
        compiler_params=_params("parallel"),
        name="compress_t" if transposed else "compress_n",
    )(x, pe, w1k, b1k, w2k, b2k)


def _heads(x):
    return [x[:, g * Q_BLK:(g + 1) * Q_BLK] for g in range(GQA)]


def _exp2_safe_max(m):
    return jnp.where(m == NEG_INF, 0.0, m)


def _aligned(x, m):
    return x if isinstance(x, int) else pl.multiple_of(x, m)


def _switch(index, branches, *operands):
    if len(branches) == 1:
        return branches[0](*operands)
    half = len(branches) // 2
    return lax.cond(index < half,
                    lambda *a: _switch(index, branches[:half], *a),
                    lambda *a: _switch(index - half, branches[half:], *a), *operands)


def _normalise(pv):
    return pv[:HEAD_DIM] / jnp.maximum(pv[HEAD_DIM:HEAD_DIM + 1], TINY)


def _attn_kernel(qt_ref, kc_ref, vct_ref, ks_ref, vst_ref, kw_ref, vwt_ref, gt_ref, ovt_ref, cthr_ref,
                 o_ref, selb_ref, s_ref, *, seq):
    qi = pl.program_id(1)
    n_sel = seq // L_SEL
    lanes = GQA * Q_BLK
    win_keys = WINDOW + Q_BLK

    qt = jnp.concatenate([qt_ref[0, g] for g in range(GQA)], axis=1)
    q0 = qi * Q_BLK
    lane_q = lax.broadcasted_iota(jnp.int32, (1, Q_BLK), 1)
    t_q = q0 + lane_q
    t_all = jnp.concatenate([t_q] * GQA, axis=1)

    def sel_scores(k0):
        return _dot(ks_ref[0, pl.ds(k0, SEL_SUB), :], qt)

    def front(n_rows, full_band, n_static):
        n_blk = n_rows * STRIDE_CMP // L_SEL
        s_c = _dot(kc_ref[0, :n_rows, :], qt)
        for a in range(SEL_AHEAD):
            s_ref[a] = sel_scores(a * SEL_SUB)
        vis = cthr_ref[:n_rows, :] <= q0
        p_heads = []
        for s_g in _heads(s_c):
            s_g = jnp.where(vis, s_g, NEG_INF)
            m_g = _exp2_safe_max(jnp.max(s_g, axis=0, keepdims=True))
            p_heads.append(jnp.exp2(s_g - m_g).astype(BF16))
        p_c = jnp.concatenate(p_heads, axis=1)
        pv_c = _dot(vct_ref[0, :, :n_rows], p_c)
        r_c = 1.0 / jnp.maximum(pv_c[HEAD_DIM:HEAD_DIM + 1], TINY)
        o_c = pv_c[:HEAD_DIM] * r_c
        imp_all = _dot(ovt_ref[:n_blk, :n_rows], p_c) * r_c
        imp = functools.reduce(jnp.add, _heads(imp_all))

        blk = lax.broadcasted_iota(jnp.int32, (n_blk, Q_BLK), 0).astype(F32)
        cur = (t_q // L_SEL).astype(F32)
        forced = (blk == 0.0) | (blk == cur) | (blk == cur - 1.0)
        free = (blk <= cur) & jnp.logical_not(forced)
        cand = jnp.where(free, imp, NEG_INF)
        n_pick = min(TOP_K, n_sel) - N_FORCED

        def rank_exact(val):
            for _ in range(n_pick):
                mx = jnp.max(val, axis=0, keepdims=True)
                first = jnp.min(jnp.where(val == mx, blk, float(n_blk)), axis=0, keepdims=True)
                val = jnp.where(blk == first, NEG_INF, val)
            return val

        val_fast = cand
        for _ in range(n_pick):
            val_fast = jnp.where(val_fast == jnp.max(val_fast, axis=0, keepdims=True), NEG_INF, val_fast)
        n_free = jnp.sum(free.astype(F32), axis=0, keepdims=True)
        n_got = jnp.sum((free & (val_fast == NEG_INF)).astype(F32), axis=0, keepdims=True)
        tied = jnp.max(jnp.where(n_got != jnp.minimum(n_free, float(n_pick)), 1.0, 0.0)) > 0.0

        start = pl.multiple_of(jnp.maximum(q0 - WINDOW, 0), Q_BLK)
        s_w = _dot(kw_ref[0, pl.ds(start, win_keys), :], qt)
        r_minus_c = (lax.broadcasted_iota(jnp.int32, (Q_BLK, Q_BLK), 0)
                     - lax.broadcasted_iota(jnp.int32, (Q_BLK, Q_BLK), 1))
        chunks = []
        n_chunks = win_keys // Q_BLK
        for i in range(n_chunks):
            rows = s_w[i * Q_BLK:(i + 1) * Q_BLK]
            if not full_band or i in (0, n_chunks - 1):
                hi = q0 - start - i * Q_BLK
                mask = (r_minus_c <= hi) & (r_minus_c > hi - WINDOW)
                rows = jnp.concatenate([jnp.where(mask, x, NEG_INF) for x in _heads(rows)], axis=1)
            chunks.append(rows)
        m_w = _exp2_safe_max(functools.reduce(jnp.maximum, [jnp.max(x, axis=0, keepdims=True) for x in chunks]))
        p_w = jnp.concatenate([jnp.exp2(x - m_w).astype(BF16) for x in chunks], axis=0)
        o_w = _normalise(_dot(vwt_ref[0, :, pl.ds(start, win_keys)], p_w))
        val = lax.cond(tied, lambda: rank_exact(cand), lambda: val_fast)
        selb_ref[:n_blk, :] = jnp.where(forced | (free & (val == NEG_INF)), 0.0, NEG_INF)
        carry = (jnp.full((1, lanes), NEG_INF, F32), jnp.zeros((V_ROWS, lanes), F32))
        if n_static:
            carry = sel_run(0, *carry, n_static, False)
        return (o_c, o_w) + tuple(carry)

    row_iota = lax.broadcasted_iota(jnp.int32, (L_SEL, lanes), 0)

    def sel_update(s_t, k0, m_prev, acc_prev, causal):
        blk0 = k0 // L_SEL
        blocks, biases, part_max = [], [], None
        for j in range(SEL_SUB // L_SEL):
            bias = jnp.concatenate([selb_ref[pl.ds(blk0 + j, 1), :]] * GQA, axis=1)
            s_b = s_t[j * L_SEL:(j + 1) * L_SEL]
            if causal:
                s_b = jnp.where(k0 + j * L_SEL + row_iota <= t_all, s_b, NEG_INF)
            s8 = functools.reduce(jnp.maximum, [s_b[r:r + SUBLANES] for r in range(0, L_SEL, SUBLANES)])
            cand = s8 + bias
            part_max = cand if part_max is None else jnp.maximum(part_max, cand)
            blocks.append(s_b)
            biases.append(bias)
        m_new = jnp.maximum(m_prev, jnp.max(part_max, axis=0, keepdims=True))
        m_safe = _exp2_safe_max(m_new)
        alpha = jnp.exp2(m_prev - m_safe)
        p = jnp.concatenate([jnp.exp2(s_b - (m_safe - bias)).astype(BF16)
                             for s_b, bias in zip(blocks, biases)], axis=0)
        pv = _dot(vst_ref[0, :, pl.ds(k0, SEL_SUB)], p)
        return m_new, alpha * acc_prev + pv

    def sel_run(k0, m_prev, acc_prev, n_sub, diagonal):
        queue = [s_ref[a] for a in range(min(SEL_AHEAD, n_sub) if diagonal else SEL_AHEAD)]
        for i in range(n_sub):
            ahead = i + SEL_AHEAD
            if not diagonal or ahead < n_sub:
                queue.append(sel_scores(_aligned(k0 + ahead * SEL_SUB, SEL_SUB)))
            m_prev, acc_prev = sel_update(queue.pop(0), _aligned(k0 + i * SEL_SUB, SEL_SUB),
                                          m_prev, acc_prev, causal=diagonal and i == n_sub - 1)
        if not diagonal:
            for a in range(SEL_AHEAD):
                s_ref[a] = queue[a]
        return m_prev, acc_prev

    n_cmp = seq // STRIDE_CMP
    parts = FRONT_PARTS if n_cmp % (FRONT_PARTS * Q_BLK) == 0 else 1
    part_subs = seq // parts // SEL_SUB
    part = q0 // (seq // parts)
    o_c, o_w, *carry = _switch(
        part, [functools.partial(front, n_cmp * (k + 1) // parts, k * (seq // parts) >= WINDOW, k * part_subs)
               for k in range(parts)])

    n_below = q0 // SEL_SUB
    done = part * part_subs
    if part_subs > SEL_LOOP:
        trips = (n_below - done) // SEL_LOOP
        carry = lax.fori_loop(
            0, trips, lambda kt, c: sel_run(_aligned((done + kt * SEL_LOOP) * SEL_SUB, SEL_SUB), *c, SEL_LOOP, False),
            tuple(carry))
        done = done + trips * SEL_LOOP
    run = min(SEL_LOOP, part_subs) // 2
    while run >= SEL_DIAG:
        take = ((n_below - done) // run) == 1
        k_run = _aligned(done * SEL_SUB, SEL_SUB)
        carry = lax.cond(take, functools.partial(lambda k, n, m, a: sel_run(k, m, a, n, False), k_run, run),
                         lambda m, a: (m, a), *carry)
        done = done + jnp.where(take, run, 0)
        run //= 2
    k_last = _aligned(done * SEL_SUB, SEL_SUB)

    def finish(n_diag, m_s, acc_s):
        o_s = _normalise(sel_run(k_last, m_s, acc_s, n_diag, True)[1])
        outs = []
        for g in range(GQA):
            sl = slice(g * Q_BLK, (g + 1) * Q_BLK)
            gate = jax.nn.sigmoid(gt_ref[0, g])
            outs.append(gate[0:1] * o_c[:, sl] + gate[1:2] * o_s[:, sl] + gate[2:3] * o_w[:, sl])
        pairs = [jnp.concatenate(outs[g:g + 2], axis=0).T for g in range(0, GQA, 2)]
        o_ref[0] = jnp.concatenate(pairs, axis=1).astype(o_ref.dtype)

    _switch(n_below - done, [functools.partial(finish, n + 1) for n in range(SEL_DIAG)], *carry)


def nsa_attention(qt, kc, vct, ks, vst, kw, vwt, gt, bsz):
    bh, _, _, seq = qt.shape
    assert SEL_LOOP % SEL_DIAG == 0 and seq % (SEL_DIAG * SEL_SUB) == 0 and seq >= WINDOW + Q_BLK
    n_cmp = seq // STRIDE_CMP
    n_sel = seq // L_SEL
    cmp_start = jnp.arange(n_cmp) * STRIDE_CMP
    sel_start = jnp.arange(n_sel) * L_SEL
    ovt = ((cmp_start[None, :] < sel_start[:, None] + L_SEL)
           & (cmp_start[None, :] + L_CMP > sel_start[:, None])).astype(BF16)
    cthr = (cmp_start + L_CMP - 1)[:, None] - jnp.arange(Q_BLK)[None, :]
    per_bh = lambda shape: pl.BlockSpec((1,) + shape, lambda b, i: (b,) + (0,) * len(shape))
    return pl.pallas_call(
        functools.partial(_attn_kernel, seq=seq),
        grid=(bh, seq // Q_BLK),
        in_specs=[pl.BlockSpec((1, GQA, HEAD_DIM, Q_BLK), lambda b, i: (b, 0, 0, i)),
                  per_bh((n_cmp, HEAD_DIM)), per_bh((V_ROWS, n_cmp)),
                  per_bh((seq, HEAD_DIM)), per_bh((V_ROWS, seq)),
                  per_bh((seq, HEAD_DIM)), per_bh((V_ROWS, seq)),
                  pl.BlockSpec((1, GQA, 3, Q_BLK), lambda b, i: (b, 0, 0, i)),
                  pl.BlockSpec((n_sel, n_cmp), lambda b, i: (0, 0)),
                  pl.BlockSpec((n_cmp, Q_BLK), lambda b, i: (0, 0))],
        out_specs=pl.BlockSpec((1, Q_BLK, GQA * HEAD_DIM), lambda b, i: (b // N_KV, i, b % N_KV)),
        out_shape=jax.ShapeDtypeStruct((bsz, seq, N_KV * GQA * HEAD_DIM), BF16),
        scratch_shapes=[pltpu.VMEM((n_sel, Q_BLK), F32), pltpu.VMEM((SEL_AHEAD, SEL_SUB, GQA * Q_BLK), F32)],
        compiler_params=_params("parallel", "arbitrary"),
        name="nsa_attention",
    )(qt, kc, vct, ks, vst, kw, vwt, gt, ovt, cthr.astype(jnp.int32))


def _ssm_kernel(u_ref, pt_ref, bq_ref, w_ref, v_ref, a1_ref, a2_ref, o_ref, mt_ref, *, chunks_per_seq):
    width = SSM_CHUNK * SSM_CH
    u = u_ref[0]
    kt = _dot_split(bq_ref[0], pt_ref[0])
    col = lax.broadcasted_iota(jnp.int32, (SSM_CH, width), 1)
    for s in range(SSM_CHUNK):
        shifted = kt if s == 0 else pltpu.roll(kt, SSM_CH * s, axis=1)
        mt_ref[s * SSM_CH:(s + 1) * SSM_CH, :] = jnp.where(col >= SSM_CH * s, shifted, 0.0).astype(BF16)
    y = _dot(u, mt_ref[...])
    x = _dot(u, w_ref[0])
    row = lax.broadcasted_iota(jnp.int32, x.shape, 0) % chunks_per_seq
    steps = int(math.log2(chunks_per_seq))
    for j in range(steps):
        d = 1 << j
        sh = jnp.where(row >= d, pltpu.roll(x, d, axis=0), 0.0)
        x = x + a1_ref[0, j:j + 1, :] * sh + a2_ref[0, j:j + 1, :] * pltpu.roll(sh, SSM_STATE, axis=1)
    prev = jnp.where(row >= 1, pltpu.roll(x, 1, axis=0), 0.0)
    o_ref[0] = y + _dot(prev.astype(BF16), v_ref[0])


def s5_scan(u2, pt, bq, w, v, a1, a2, chunks_per_seq):
    g, n_chunks, width = u2.shape
    per_g = lambda a: pl.BlockSpec((1,) + a.shape[1:], lambda i: (i,) + (0,) * (a.ndim - 1))
    return pl.pallas_call(
        functools.partial(_ssm_kernel, chunks_per_seq=chunks_per_seq),
        grid=(g,),
        in_specs=[per_g(u2), per_g(pt), per_g(bq), per_g(w), per_g(v), per_g(a1), per_g(a2)],
        out_specs=pl.BlockSpec((1, n_chunks, width), lambda i: (i, 0, 0)),
        out_shape=jax.ShapeDtypeStruct((g, n_chunks, width), F32),
        scratch_shapes=[pltpu.VMEM((width, width), BF16)],
        compiler_params=_params("parallel"),
        name="s5_scan",
    )(u2, pt, bq, w, v, a1, a2)


def _s5_tables(log_dt, a_re, a_im, b_re, b_im, c_re, c_im, chunks_per_seq):
    g, n = a_re.shape
    dt = jnp.exp(log_dt)[:, None]
    lam_re, lam_im = dt * a_re, dt * a_im

    def power(k):
        k = k.astype(F32)[None, :, None]
        mag = jnp.exp(k * lam_re[:, None, :])
        return mag * jnp.cos(k * lam_im[:, None, :]), mag * jnp.sin(k * lam_im[:, None, :])

    abar_re, abar_im = (x[:, 0] for x in power(jnp.ones((1,))))
    den = a_re * a_re + a_im * a_im
    nr = abar_re - 1.0
    f_re = (nr * a_re + abar_im * a_im) / den
    f_im = (abar_im * a_re - nr * a_im) / den
    bb_re = f_re[..., None] * b_re - f_im[..., None] * b_im
    bb_im = f_re[..., None] * b_im + f_im[..., None] * b_re
    steps = jnp.arange(SSM_CHUNK)
    width = SSM_CHUNK * SSM_CH
    e_re, e_im = power(steps)
    ca_re = c_re[:, None] * e_re[:, :, None, :] - c_im[:, None] * e_im[:, :, None, :]
    ca_im = c_re[:, None] * e_im[:, :, None, :] + c_im[:, None] * e_re[:, :, None, :]
    pt = jnp.concatenate([ca_re, -ca_im], axis=-1).reshape(g, width, 2 * n).transpose(0, 2, 1)
    bq = jnp.concatenate([bb_re, bb_im], axis=1).transpose(0, 2, 1)
    r_re, r_im = power(SSM_CHUNK - 1 - steps)
    bt_re, bt_im = bb_re.transpose(0, 2, 1)[:, None], bb_im.transpose(0, 2, 1)[:, None]
    w_re = r_re[:, :, None, :] * bt_re - r_im[:, :, None, :] * bt_im
    w_im = r_re[:, :, None, :] * bt_im + r_im[:, :, None, :] * bt_re
    w = jnp.concatenate([w_re, w_im], axis=-1).reshape(g, width, 2 * n).astype(BF16)
    n_re, n_im = power(steps + 1)
    cv_re = c_re[:, None] * n_re[:, :, None, :] - c_im[:, None] * n_im[:, :, None, :]
    cv_im = c_re[:, None] * n_im[:, :, None, :] + c_im[:, None] * n_re[:, :, None, :]
    v = jnp.concatenate([cv_re, -cv_im], axis=-1).reshape(g, width, 2 * n).transpose(0, 2, 1).astype(BF16)
    n_steps = int(math.log2(chunks_per_seq))
    s_re, s_im = power(SSM_CHUNK * (2 ** jnp.arange(max(n_steps, 1))))
    a1 = jnp.concatenate([s_re, s_re], axis=-1)
    a2 = jnp.concatenate([-s_im, s_im], axis=-1)
    return pt, bq, w, v, a1, a2


def _slot_masks(rows):
    slot = lax.broadcasted_iota(jnp.int32, (rows, LANES), 1) // SSM_CH
    return [slot == k for k in range(LANES // SSM_CH)]


def _s5_pack_kernel(*refs):
    u_refs, o_ref = refs[:-1], refs[-1]
    cb = o_ref.shape[1]
    slots = LANES // SSM_CH
    masks = _slot_masks(cb)
    for b, u_ref in enumerate(u_refs):
        for m in range(SSM_CHUNK // slots):
            rot = [pltpu.roll(u_ref[pl.ds(m * slots + j, cb, stride=SSM_CHUNK), :], SSM_CH * j, axis=1)
                   for j in range(slots)]
            for g in range(slots):
                mix = rot[0]
                for j in range(1, slots):
                    mix = jnp.where(masks[(g + j) % slots], rot[j], mix)
                out = mix if g == 0 else pltpu.roll(mix, LANES - SSM_CH * g, axis=1)
                o_ref[b * slots + g, :, m * LANES:(m + 1) * LANES] = out.astype(o_ref.dtype)


def s5_pack(u):
    t, d = u.shape
    n_groups = d // SSM_CH
    n_chunks = t // SSM_CHUNK
    cb = min(PACK_CHUNKS, n_chunks)
    width = SSM_CHUNK * SSM_CH
    return pl.pallas_call(
        _s5_pack_kernel,
        grid=(n_chunks // cb,),
        in_specs=[pl.BlockSpec((cb * SSM_CHUNK, LANES), functools.partial(lambda b, i: (i, b), b))
                  for b in range(d // LANES)],
        out_specs=pl.BlockSpec((n_groups, cb, width), lambda i: (0, i, 0)),
        out_shape=jax.ShapeDtypeStruct((n_groups, n_chunks, width), BF16),
        compiler_params=_params("parallel"),
        name="s5_pack",
    )(*[u] * (d // LANES))


def _s5_unpack_kernel(y_ref, *o_refs):
    cb = y_ref.shape[1]
    slots = LANES // SSM_CH
    masks = _slot_masks(cb)
    for b, o_ref in enumerate(o_refs):
        for m in range(SSM_CHUNK // slots):
            rot = [pltpu.roll(y_ref[b * slots + g, :, m * LANES:(m + 1) * LANES], SSM_CH * g, axis=1)
                   if g else y_ref[b * slots, :, m * LANES:(m + 1) * LANES] for g in range(slots)]
            for j in range(slots):
                mix = rot[0]
                for g in range(1, slots):
                    mix = jnp.where(masks[(g + j) % slots], rot[g], mix)
                out = mix if j == 0 else pltpu.roll(mix, LANES - SSM_CH * j, axis=1)
                o_ref[pl.ds(m * slots + j, cb, stride=SSM_CHUNK), :] = out


def s5_unpack(y2):
    n_groups, n_chunks, width = y2.shape
    d = n_groups * SSM_CH
    cb = min(PACK_CHUNKS, n_chunks)
    return pl.pallas_call(
        _s5_unpack_kernel,
        grid=(n_chunks // cb,),
        in_specs=[pl.BlockSpec((n_groups, cb, width), lambda i: (0, i, 0))],
        out_specs=[pl.BlockSpec((cb * SSM_CHUNK, LANES), lambda i: (i, 0))] * (d // LANES),
        out_shape=[jax.ShapeDtypeStruct((n_chunks * SSM_CHUNK, LANES), F32)] * (d // LANES),
        compiler_params=_params("parallel"),
        name="s5_unpack",
    )(y2)


def _mix_out_kernel(x_ref, oa_ref, u_ref, dsk_ref, wglu_ref, bglu_ref, na_ref, ns_ref, woa_ref, wos_ref,
                    *refs):
    ys_refs, o_ref = refs[:-1], refs[-1]
    d_ssm = u_ref.shape[1]
    y_scan = jnp.concatenate([r[...] for r in ys_refs], axis=1)
    y = jax.nn.gelu(y_scan + dsk_ref[...] * u_ref[...])
    z = _dot(y.astype(BF16), wglu_ref[...]) + bglu_ref[...]
    o_ssm = z[:, :d_ssm] * jax.nn.sigmoid(z[:, d_ssm:])
    att_n = _rms(oa_ref[...].astype(F32), na_ref[...]).astype(BF16)
    ssm_n = _rms(o_ssm, ns_ref[...]).astype(BF16)
    o_ref[...] = x_ref[...] + _dot(att_n, woa_ref[...]) + _dot(ssm_n, wos_ref[...])


def mix_out(x2, o_att, y_ssm, u, d_skip, w_glu, b_glu, n_att, n_ssm, w_out):
    t, d = x2.shape
    d_att = o_att.shape[1]
    d_ssm = u.shape[1]
    tm = _row_tile(t, MIX_ROWS)
    rows = lambda w: pl.BlockSpec((tm, w), lambda i: (i, 0))
    full = lambda a: pl.BlockSpec(a.shape, lambda i: (0,) * a.ndim)
    consts = [d_skip.reshape(1, d_ssm), w_glu.astype(BF16), b_glu.reshape(1, 2 * d_ssm),
              n_att.reshape(1, d_att), n_ssm.reshape(1, d_ssm),
              w_out[:d_att].astype(BF16), w_out[d_att:].astype(BF16)]
    return pl.pallas_call(
        _mix_out_kernel,
        grid=(t // tm,),
        in_specs=[rows(d), rows(d_att), rows(d_ssm)] + [full(c) for c in consts] + [rows(LANES)] * len(y_ssm),
        out_specs=rows(d),
        out_shape=jax.ShapeDtypeStruct((t, d), F32),
        compiler_params=_params("parallel"),
        name="mix_out",
    )(x2, o_att, u, *consts, *y_ssm)


def _ffn_kernel(x_ref, g_ref, wg_ref, wu_ref, wd_ref, fg_ref, o_ref, acc_ref, *, f_chunk, final_norm):
    x = x_ref[...]
    h = _rms(x, g_ref[...]).astype(BF16)
    d_ff = wg_ref.shape[1]
    acc_ref[...] = x
    for c in range(d_ff // f_chunk):
        sl = slice(c * f_chunk, (c + 1) * f_chunk)
        a = jax.nn.silu(_dot(h, wg_ref[:, sl])) * _dot(h, wu_ref[:, sl])
        acc_ref[...] += _dot(a.astype(BF16), wd_ref[sl, :])
    out = acc_ref[...]
    if final_norm:
        out = _rms(out, fg_ref[...])
    o_ref[...] = out


def ffn(x2, g, w_gate, w_up, w_down, final_g, final_norm):
    t, d = x2.shape
    d_ff = w_gate.shape[1]
    tm = _row_tile(t)
    f_chunk = 256 if d_ff % 256 == 0 else d_ff
    full = lambda a: pl.BlockSpec(a.shape, lambda i: (0,) * a.ndim)
    consts = [g.reshape(1, d), w_gate.astype(BF16), w_up.astype(BF16), w_down.astype(BF16),
              final_g.reshape(1, d)]
    return pl.pallas_call(
        functools.partial(_ffn_kernel, f_chunk=f_chunk, final_norm=final_norm),
        grid=(t // tm,),
        in_specs=[pl.BlockSpec((tm, d), lambda i: (i, 0))] + [full(c) for c in consts],
        out_specs=pl.BlockSpec((tm, d), lambda i: (i, 0)),
        out_shape=jax.ShapeDtypeStruct((t, d), F32),
        scratch_shapes=[pltpu.VMEM((tm, d), F32)],
        compiler_params=_params("parallel"),
        name="ffn",
    )(x2, *consts)


def _mixer(x2, bsz, seq, attn_norm_g, w_in, cmp_pe, cmp_w1, cmp_b1, cmp_w2, cmp_b2, log_dt, a_re, a_im,
           b_re, b_im, c_re, c_im, d_skip, w_glu, b_glu, mix_norm_att, mix_norm_ssm, w_out):
    t, d = x2.shape
    d_ssm = d - D_ATT
    bh = bsz * N_KV
    qt, vst, vwt, gt, kc, vc, ks, kw, u = in_proj(x2.reshape(bsz, seq, d), attn_norm_g, w_in)
    k_c = compress(kc, cmp_pe[0], cmp_w1[0], cmp_b1[0], cmp_w2[0], cmp_b2[0], transposed=False)
    v_ct = compress(vc, cmp_pe[1], cmp_w1[1], cmp_b1[1], cmp_w2[1], cmp_b2[1], transposed=True)
    o_att = nsa_attention(qt.reshape(bh, GQA, HEAD_DIM, seq), k_c, v_ct,
                          ks.reshape(bh, seq, HEAD_DIM), vst.reshape(bh, V_ROWS, seq),
                          kw.reshape(bh, seq, HEAD_DIM), vwt.reshape(bh, V_ROWS, seq),
                          gt.reshape(bh, GQA, 3, seq), bsz).reshape(t, D_ATT)

    u = u.reshape(t, d_ssm)
    chunks_per_seq = seq // SSM_CHUNK
    tables = _s5_tables(log_dt, a_re, a_im, b_re, b_im, c_re, c_im, chunks_per_seq)
    y_ssm = s5_unpack(s5_scan(s5_pack(u), *tables, chunks_per_seq))

    return mix_out(x2, o_att, y_ssm, u, d_skip, w_glu, b_glu, mix_norm_att, mix_norm_ssm, w_out)


def kernel(x, attn_norm_g, w_in, cmp_pe, cmp_w1, cmp_b1, cmp_w2, cmp_b2, log_dt, a_re, a_im, b_re, b_im, c_re, c_im, d_skip, w_glu, b_glu, mix_norm_att, mix_norm_ssm, w_out, ffn_norm_g, w_gate, w_up, w_down, final_norm_g):
    bsz, seq, d = x.shape
    depth = w_in.shape[0]
    x2 = x.reshape(bsz * seq, d)
    for l in range(depth):
        x2 = _mixer(x2, bsz, seq, attn_norm_g[l], w_in[l], cmp_pe[l], cmp_w1[l], cmp_b1[l], cmp_w2[l],
                    cmp_b2[l], log_dt[l], a_re[l], a_im[l], b_re[l], b_im[l], c_re[l], c_im[l], d_skip[l],
                    w_glu[l], b_glu[l], mix_norm_att[l], mix_norm_ssm[l], w_out[l])
        x2 = ffn(x2, ffn_norm_g[l], w_gate[l], w_up[l], w_down[l], final_norm_g, final_norm=(l == depth - 1))
    return x2.reshape(bsz, seq, d)
```
